```python
import jax, jax.numpy as jnp
from jax import lax
import numpy as np

D_MODEL = 1024
BATCH = 16
SEQ = 2048
DEPTH = 4

CHUNK = 64
N_MIXERS = 3
N_RGLRU_LAYERS = (DEPTH + 2) // 3
N_RWKV_LAYERS = (DEPTH + 1) // 3
N_ATTN_LAYERS = DEPTH // 3

DEEPNORM_ALPHA = (2 * DEPTH) ** 0.25
DEEPNORM_BETA = (8 * DEPTH) ** -0.25
LN_EPS = 1e-5

D_RNN = 1344
LRU_BLOCKS = 16
LRU_BLOCK_SIZE = D_RNN // LRU_BLOCKS
CONV_WIDTH = 4
RG_LRU_C = 8.0

RW_HEAD_SIZE = 64
RW_HEADS = D_MODEL // RW_HEAD_SIZE
RW_DECAY_LORA = 64
RW_AAA_LORA = 64
RW_GATE_LORA = 128
RW_GN_EPS = 64e-5

ATT_HEADS = 16
ATT_HEAD_DIM = D_MODEL // ATT_HEADS
BAND_CHUNKS = 9
BAND = BAND_CHUNKS * CHUNK
MAX_REL = 2 * CHUNK
NEG_INF = -1e30

MEM_TOKENS = 256
MEM_HEADS = 4
MEM_HEAD_DIM = D_MODEL // MEM_HEADS

D_FF = 4 * D_MODEL

kernel_name = "hybrid_rglru_rwkv7_chunkattn_deepnorm"


def _layer_norm(x, g, b):
    xf = x.astype(jnp.float32)
    mu = jnp.mean(xf, axis=-1, keepdims=True)
    var = jnp.mean(jnp.square(xf - mu), axis=-1, keepdims=True)
    return ((xf - mu) * lax.rsqrt(var + LN_EPS) * g + b).astype(x.dtype)


def _causal_depthwise_conv(x, w, b):
    k, c = w.shape
    y = lax.conv_general_dilated(x, w[:, None, :], window_strides=(1,), padding=[(k - 1, 0)],
                                 dimension_numbers=('NWC', 'WIO', 'NWC'), feature_group_count=c)
    return y + b


def _linear_recurrence_combine(left, right):
    a_l, b_l = left
    a_r, b_r = right
    return a_l * a_r, a_r * b_l + b_r


def rglru_block(x, w_in, conv_w, conv_b, gate_w, gate_b, lam, w_out):
    bsz, seq, _ = x.shape
    u = x @ w_in
    gate_branch = jax.nn.gelu(u[..., :D_RNN])
    xr = _causal_depthwise_conv(u[..., D_RNN:], conv_w, conv_b)
    xb = xr.reshape(bsz, seq, LRU_BLOCKS, LRU_BLOCK_SIZE)
    gates = jnp.einsum('bsnc,gncd->bsgnd', xb, gate_w).reshape(bsz, seq, 2, D_RNN) + gate_b
    gates = jax.nn.sigmoid(gates.astype(jnp.float32))
    r_gate, i_gate = gates[:, :, 0], gates[:, :, 1]
    log_a = -RG_LRU_C * r_gate * jax.nn.softplus(-lam.astype(jnp.float32))
    a = jnp.exp(log_a)
    bterm = jnp.sqrt(-jnp.expm1(2.0 * log_a)) * (i_gate * xr.astype(jnp.float32))
    _, h = lax.associative_scan(_linear_recurrence_combine, (a, bterm), axis=1)
    return (h.astype(x.dtype) * gate_branch) @ w_out


def rwkv7_time_mix(x, mu, w_r, w_k, w_v, w0, w1, w2, a0, a1, a2, g1, g2, k_k, k_a, r_k,
                   lnx_g, lnx_b, w_o):
    bsz, seq, d = x.shape
    f32 = jnp.float32
    xx = jnp.pad(x, ((0, 0), (1, 0), (0, 0)))[:, :-1] - x
    xr, xw, xk, xv, xa, xg = [x + xx * mu[c] for c in range(6)]
    r = (xr @ w_r).astype(f32)
    k = (xk @ w_k).astype(f32)
    v = (xv @ w_v).astype(f32)
    w_log = -jax.nn.softplus(-(w0 + jnp.tanh(xw @ w1) @ w2).astype(f32)) - 0.5
    decay = jnp.exp(-jnp.exp(w_log))
    a = jax.nn.sigmoid((a0 + (xa @ a1) @ a2).astype(f32))
    g = (jax.nn.sigmoid(xg @ g1) @ g2).astype(f32)
    kk = (k * k_k).reshape(bsz, seq, RW_HEADS, RW_HEAD_SIZE)
    kk = kk / jnp.maximum(jnp.linalg.norm(kk, axis=-1, keepdims=True), 1e-12)
    k = k * (1.0 + (a - 1.0) * k_a)

    heads = lambda t: jnp.moveaxis(t.reshape(bsz, seq, RW_HEADS, RW_HEAD_SIZE), 1, 0)
    a_h = a.reshape(bsz, seq, RW_HEADS, RW_HEAD_SIZE)
    seq_inputs = (heads(r), heads(decay), heads(k), heads(v),
                  jnp.moveaxis(-kk, 1, 0), jnp.moveaxis(kk * a_h, 1, 0))

    def step(state, inp):
        r_t, w_t, k_t, v_t, aa_t, bb_t = inp
        sa = jnp.einsum('bhij,bhj->bhi', state, aa_t)
        state = (state * w_t[:, :, None, :] + sa[..., None] * bb_t[:, :, None, :]
                 + v_t[..., None] * k_t[:, :, None, :])
        return state, jnp.einsum('bhij,bhj->bhi', state, r_t)

    state0 = jnp.zeros((bsz, RW_HEADS, RW_HEAD_SIZE, RW_HEAD_SIZE), f32)
    _, y = lax.scan(step, state0, seq_inputs)
    y = jnp.moveaxis(y, 0, 1)
    mu_y = jnp.mean(y, axis=-1, keepdims=True)
    var_y = jnp.mean(jnp.square(y - mu_y), axis=-1, keepdims=True)
    y = ((y - mu_y) * lax.rsqrt(var_y + RW_GN_EPS)).reshape(bsz, seq, d) * lnx_g + lnx_b
    rh = r.reshape(bsz, seq, RW_HEADS, RW_HEAD_SIZE)
    kh = k.reshape(bsz, seq, RW_HEADS, RW_HEAD_SIZE)
    vh = v.reshape(bsz, seq, RW_HEADS, RW_HEAD_SIZE)
    bonus = (jnp.sum(rh * kh * r_k, axis=-1, keepdims=True) * vh).reshape(bsz, seq, d)
    return ((y + bonus) * g).astype(x.dtype) @ w_o


def chunk_relpos_attention(x, w_qkv, rel_bias, w_o):
    bsz, seq, d = x.shape
    n_chunks = seq // CHUNK
    left = (BAND_CHUNKS - 1) * CHUNK
    qkv = (x @ w_qkv).reshape(bsz, seq, 3, ATT_HEADS, ATT_HEAD_DIM)
    q = qkv[:, :, 0] * (ATT_HEAD_DIM ** -0.5)
    kp = jnp.pad(qkv[:, :, 1], ((0, 0), (left, 0), (0, 0), (0, 0)))
    vp = jnp.pad(qkv[:, :, 2], ((0, 0), (left, 0), (0, 0), (0, 0)))
    rel = (left + jnp.arange(CHUNK))[:, None] - jnp.arange(BAND)[None, :]
    bias = rel_bias[:, jnp.clip(rel, -MAX_REL, MAX_REL) + MAX_REL].astype(jnp.float32)

    def one_chunk(c):
        start = c * CHUNK
        qc = lax.dynamic_slice_in_dim(q, start, CHUNK, axis=1)
        kc = lax.dynamic_slice_in_dim(kp, start, BAND, axis=1)
        vc = lax.dynamic_slice_in_dim(vp, start, BAND, axis=1)
        s = jnp.einsum('bqhd,bkhd->bhqk', qc, kc).astype(jnp.float32) + bias
        valid = (start - left + jnp.arange(BAND)) >= 0
        s = jnp.where(valid[None, None, None, :], s, NEG_INF)
        p = jax.nn.softmax(s, axis=-1).astype(vc.dtype)
        return jnp.einsum('bhqk,bkhd->bqhd', p, vc)

    out = lax.map(one_chunk, jnp.arange(n_chunks))
    out = jnp.transpose(out, (1, 0, 2, 3, 4)).reshape(bsz, seq, d)
    return out @ w_o


def memory_cross_attention(x, mem, w_q, w_kv, w_o):
    bsz, seq, d = x.shape
    q = (x @ w_q).reshape(bsz, seq, MEM_HEADS, MEM_HEAD_DIM)
    kv = (mem @ w_kv).reshape(bsz, mem.shape[1], 2, MEM_HEADS, MEM_HEAD_DIM)
    s = jnp.einsum('bqhd,bkhd->bhqk', q, kv[:, :, 0]).astype(jnp.float32) * (MEM_HEAD_DIM ** -0.5)
    p = jax.nn.softmax(s, axis=-1).astype(x.dtype)
    o = jnp.einsum('bhqk,bkhd->bqhd', p, kv[:, :, 1]).reshape(bsz, seq, d)
    return o @ w_o


def squared_relu_mlp(x, w1, w2):
    return jnp.square(jax.nn.relu(x @ w1)) @ w2


def _normal(key, shape, scale):
    return scale * jax.random.normal(key, shape, jnp.float32)


def setup_inputs(seed: int = 0) -> dict:
    key = jax.random.key(seed)
    ks = iter(jax.random.split(key, 40))
    d = D_MODEL
    n_a, n_b, n_c = N_RGLRU_LAYERS, N_RWKV_LAYERS, N_ATTN_LAYERS
    beta = DEEPNORM_BETA
    lam_u = jax.random.uniform(next(ks), (n_a, D_RNN), jnp.float32, minval=0.9, maxval=0.999)
    lam_s = lam_u ** (1.0 / RG_LRU_C)
    return {
        "x": _normal(next(ks), (BATCH, SEQ, d), 1.0),
        "mem": _normal(next(ks), (BATCH, MEM_TOKENS, d), 1.0),
        "ln_g": 1.0 + _normal(next(ks), (DEPTH, 3, d), 0.02),
        "ln_b": _normal(next(ks), (DEPTH, 3, d), 0.02),
        "lru_w_in": _normal(next(ks), (n_a, d, 2 * D_RNN), d ** -0.5),
        "lru_conv_w": _normal(next(ks), (n_a, CONV_WIDTH, D_RNN), CONV_WIDTH ** -0.5),
        "lru_conv_b": _normal(next(ks), (n_a, D_RNN), 0.02),
        "lru_gate_w": _normal(next(ks), (n_a, 2, LRU_BLOCKS, LRU_BLOCK_SIZE, LRU_BLOCK_SIZE), LRU_BLOCK_SIZE ** -0.5),
        "lru_gate_b": _normal(next(ks), (n_a, 2, D_RNN), 0.02),
        "lru_lambda": jnp.log(lam_s) - jnp.log1p(-lam_s),
        "lru_w_out": _normal(next(ks), (n_a, D_RNN, d), beta * D_RNN ** -0.5),
        "rw_mu": jax.random.uniform(next(ks), (n_b, 6, d), jnp.float32),
        "rw_w_r": _normal(next(ks), (n_b, d, d), d ** -0.5),
        "rw_w_k": _normal(next(ks), (n_b, d, d), d ** -0.5),
        "rw_w_v": _normal(next(ks), (n_b, d, d), d ** -0.5),
        "rw_w0": jax.random.uniform(next(ks), (n_b, d), jnp.float32, minval=-6.0, maxval=-1.0),
        "rw_w1": _normal(next(ks), (n_b, d, RW_DECAY_LORA), d ** -0.5),
        "rw_w2": _normal(next(ks), (n_b, RW_DECAY_LORA, d), 0.5 * RW_DECAY_LORA ** -0.5),
        "rw_a0": _normal(next(ks), (n_b, d), 0.1),
        "rw_a1": _normal(next(ks), (n_b, d, RW_AAA_LORA), d ** -0.5),
        "rw_a2": _normal(next(ks), (n_b, RW_AAA_LORA, d), RW_AAA_LORA ** -0.5),
        "rw_g1": _normal(next(ks), (n_b, d, RW_GATE_LORA), d ** -0.5),
        "rw_g2": _normal(next(ks), (n_b, RW_GATE_LORA, d), RW_GATE_LORA ** -0.5),
        "rw_k_k": 0.85 + _normal(next(ks), (n_b, d), 0.02),
        "rw_k_a": 1.0 + _normal(next(ks), (n_b, d), 0.02),
        "rw_r_k": _normal(next(ks), (n_b, RW_HEADS, RW_HEAD_SIZE), 0.1),
        "rw_lnx_g": 1.0 + _normal(next(ks), (n_b, d), 0.02),
        "rw_lnx_b": _normal(next(ks), (n_b, d), 0.02),
        "rw_w_o": _normal(next(ks), (n_b, d, d), beta * d ** -0.5),
        "ca_w_qkv": _normal(next(ks), (n_c, d, 3 * d), d ** -0.5),
        "ca_rel_bias": _normal(next(ks), (n_c, ATT_HEADS, 2 * MAX_REL + 1), 0.2),
        "ca_w_o": _normal(next(ks), (n_c, d, d), beta * d ** -0.5),
        "mx_w_q": _normal(next(ks), (DEPTH, d, d), d ** -0.5),
        "mx_w_kv": _normal(next(ks), (DEPTH, d, 2 * d), d ** -0.5),
        "mx_w_o": _normal(next(ks), (DEPTH, d, d), beta * d ** -0.5),
        "mlp_w1": _normal(next(ks), (DEPTH, d, D_FF), d ** -0.5),
        "mlp_w2": _normal(next(ks), (DEPTH, D_FF, d), beta * D_FF ** -0.5),
    }


def reference(x, mem, ln_g, ln_b,
              lru_w_in, lru_conv_w, lru_conv_b, lru_gate_w, lru_gate_b, lru_lambda, lru_w_out,
              rw_mu, rw_w_r, rw_w_k, rw_w_v, rw_w0, rw_w1, rw_w2, rw_a0, rw_a1, rw_a2, rw_g1, rw_g2,
              rw_k_k, rw_k_a, rw_r_k, rw_lnx_g, rw_lnx_b, rw_w_o,
              ca_w_qkv, ca_rel_bias, ca_w_o,
              mx_w_q, mx_w_kv, mx_w_o, mlp_w1, mlp_w2):
    h = x
    for i in range(DEPTH):
        kind, j = i % N_MIXERS, i // N_MIXERS
        if kind == 0:
            y = rglru_block(h, lru_w_in[j], lru_conv_w[j], lru_conv_b[j], lru_gate_w[j],
                            lru_gate_b[j], lru_lambda[j], lru_w_out[j])
        elif kind == 1:
            y = rwkv7_time_mix(h, rw_mu[j], rw_w_r[j], rw_w_k[j], rw_w_v[j], rw_w0[j], rw_w1[j],
                               rw_w2[j], rw_a0[j], rw_a1[j], rw_a2[j], rw_g1[j], rw_g2[j],
                               rw_k_k[j], rw_k_a[j], rw_r_k[j], rw_lnx_g[j], rw_lnx_b[j], rw_w_o[j])
        else:
            y = chunk_relpos_attention(h, ca_w_qkv[j], ca_rel_bias[j], ca_w_o[j])
        h = _layer_norm(DEEPNORM_ALPHA * h + y, ln_g[i, 0], ln_b[i, 0])
        y = memory_cross_attention(h, mem, mx_w_q[i], mx_w_kv[i], mx_w_o[i])
        h = _layer_norm(DEEPNORM_ALPHA * h + y, ln_g[i, 1], ln_b[i, 1])
        y = squared_relu_mlp(h, mlp_w1[i], mlp_w2[i])
        h = _layer_norm(DEEPNORM_ALPHA * h + y, ln_g[i, 2], ln_b[i, 2])
    return h
```

```python
import functools

import jax
import jax.numpy as jnp
from jax import lax
from jax.experimental import pallas as pl
from jax.experimental.pallas import tpu as pltpu

F32 = jnp.float32
BF16 = jnp.bfloat16

D_MODEL = 1024
DEPTH = 4
CHUNK = 64
N_MIXERS = 3
DEEPNORM_ALPHA = (2 * DEPTH) ** 0.25
LN_EPS = 1e-5

D_RNN = 1344
LRU_BLOCKS = 16
LRU_BLOCK_SIZE = D_RNN // LRU_BLOCKS
CONV_WIDTH = 4
RG_LRU_C = 8.0
LANES = 128
SUBLANES = 8
D_RNN_PAD = -(-D_RNN // LANES) * LANES

RW_HEAD_SIZE = 64
RW_HEADS = D_MODEL // RW_HEAD_SIZE
RW_PAIRS = D_MODEL // LANES
RW_GN_EPS = 64e-5
RW_CHUNK = 64

ATT_HEADS = 16
ATT_HEAD_DIM = D_MODEL // ATT_HEADS
ATT_PAIRS = D_MODEL // LANES
BAND_CHUNKS = 9
MAX_REL = 2 * CHUNK
NEG_INF = -1e30
ATT_TQ = 256
ATT_WIN = ATT_TQ + (BAND_CHUNKS - 1) * CHUNK
ATT_TBL = ATT_WIN + (BAND_CHUNKS - 1) * CHUNK

MEM_HEADS = 4
MEM_HEAD_DIM = D_MODEL // MEM_HEADS
D_FF = 4 * D_MODEL

VMEM_LIMIT = 56 * 1024 * 1024


def _cparams(*sem):
    return pltpu.CompilerParams(dimension_semantics=sem, vmem_limit_bytes=VMEM_LIMIT)


def _residual_ln(hres, y, g, b):
    z = DEEPNORM_ALPHA * hres + y
    mu = jnp.mean(z, axis=-1, keepdims=True)
    zc = z - mu
    var = jnp.mean(zc * zc, axis=-1, keepdims=True)
    return zc * lax.rsqrt(var + LN_EPS) * g + b


def _gelu_tanh(x):
    return 0.5 * x * (1.0 + jnp.tanh(0.7978845608028654 * (x + 0.044715 * (x * x * x))))


def _softplus(x):
    return jnp.maximum(x, 0.0) + jnp.log1p(jnp.exp(-jnp.abs(x)))


def _sigmoid(x):
    return 1.0 / (1.0 + jnp.exp(-x))


def _split3(x):
    hi = x.astype(BF16)
    r1 = x - hi.astype(F32)
    mid = r1.astype(BF16)
    lo = (r1 - mid.astype(F32)).astype(BF16)
    return hi, mid, lo


def _proj_kernel(x_ref, w_ref, o_ref, *, chunk):
    x = x_ref[...].astype(BF16)
    n = w_ref.shape[1]
    for n0 in range(0, n, chunk):
        y = jnp.dot(x, w_ref[:, n0:n0 + chunk], preferred_element_type=F32)
        o_ref[:, n0:n0 + chunk] = y.astype(o_ref.dtype)


def _proj(x2d, w, out_dtype, tm=512, chunk=512):
    t, k = x2d.shape
    n = w.shape[1]
    tm = min(tm, t)
    return pl.pallas_call(
        functools.partial(_proj_kernel, chunk=chunk),
        grid=(t // tm,),
        in_specs=[pl.BlockSpec((tm, k), lambda i: (i, 0)),
                  pl.BlockSpec((k, n), lambda i: (0, 0))],
        out_specs=pl.BlockSpec((tm, n), lambda i: (i, 0)),
        out_shape=jax.ShapeDtypeStruct((t, n), out_dtype),
        compiler_params=_cparams("parallel"),
        name="proj",
    )(x2d, w)


def _proj_ln_kernel(y_ref, w_ref, h_ref, g_ref, b_ref, o_ref):
    y = jnp.dot(y_ref[...].astype(BF16), w_ref[...], preferred_element_type=F32)
    o_ref[...] = _residual_ln(h_ref[...], y, g_ref[...], b_ref[...])


def _proj_ln(y2d, w, h2d, g, b, tm=512):
    t, k = y2d.shape
    d = w.shape[1]
    tm = min(tm, t)
    return pl.pallas_call(
        _proj_ln_kernel,
        grid=(t // tm,),
        in_specs=[pl.BlockSpec((tm, k), lambda i: (i, 0)),
                  pl.BlockSpec((k, d), lambda i: (0, 0)),
                  pl.BlockSpec((tm, d), lambda i: (i, 0)),
                  pl.BlockSpec((1, d), lambda i: (0, 0)),
                  pl.BlockSpec((1, d), lambda i: (0, 0))],
        out_specs=pl.BlockSpec((tm, d), lambda i: (i, 0)),
        out_shape=jax.ShapeDtypeStruct((t, d), F32),
        compiler_params=_cparams("parallel"),
        name="proj_ln",
    )(y2d, w, h2d, g, b)


def _lru_in_kernel(x_ref, w_ref, gate_ref, xpre_ref):
    x = x_ref[...].astype(BF16)
    c = gate_ref.shape[1]
    gate_ref[...] = _gelu_tanh(jnp.dot(x, w_ref[:, :c], preferred_element_type=F32)).astype(gate_ref.dtype)
    xpre_ref[...] = jnp.dot(x, w_ref[:, c:], preferred_element_type=F32)


def _lru_in(x2d, w, tm=512):
    t, k = x2d.shape
    c = w.shape[1] // 2
    tm = min(tm, t)
    return pl.pallas_call(
        _lru_in_kernel,
        grid=(t // tm,),
        in_specs=[pl.BlockSpec((tm, k), lambda i: (i, 0)),
                  pl.BlockSpec((k, 2 * c), lambda i: (0, 0))],
        out_specs=[pl.BlockSpec((tm, c), lambda i: (i, 0)),
                   pl.BlockSpec((tm, c), lambda i: (i, 0))],
        out_shape=[jax.ShapeDtypeStruct((t, c), BF16), jax.ShapeDtypeStruct((t, c), F32)],
        compiler_params=_cparams("parallel"),
        name="lru_in",
    )(x2d, w)


def _lru_scan_kernel(xpre_ref, gate_ref, h_ref, cw_ref, cb_ref, gwr_ref, gwi_ref, gb_ref, lam_ref,
                     wout_ref, g_ref, b_ref, o_ref, xbuf, hcarry):
    ts = xpre_ref.shape[1]
    j = pl.program_id(1)

    @pl.when(j == 0)
    def _():
        xbuf[0:SUBLANES, :] = jnp.zeros((SUBLANES, xbuf.shape[1]), F32)
        hcarry[...] = jnp.zeros(hcarry.shape, F32)

    x = xpre_ref[0]
    xbuf[SUBLANES:SUBLANES + ts, :] = x
    xr = cb_ref[...] + cw_ref[3:4, :] * x
    for k in range(CONV_WIDTH - 1):
        off = SUBLANES - (CONV_WIDTH - 1) + k
        xr = xr + cw_ref[k:k + 1, :] * xbuf[off:off + ts, :]
    xbuf[0:SUBLANES, :] = xbuf[ts:ts + SUBLANES, :]

    xb = xr.astype(BF16)
    r_gate = _sigmoid(jnp.dot(xb, gwr_ref[...], preferred_element_type=F32) + gb_ref[0:1, :])
    i_gate = _sigmoid(jnp.dot(xb, gwi_ref[...], preferred_element_type=F32) + gb_ref[1:2, :])
    log_a = (-RG_LRU_C) * r_gate * _softplus(-lam_ref[...])
    a = jnp.exp(log_a)
    bt = jnp.sqrt(-jnp.tanh(log_a) * (a * a + 1.0)) * (i_gate * xr)

    row = lax.broadcasted_iota(jnp.int32, a.shape, 0)
    d = 1
    while d < ts:
        keep = row >= d
        a_sh = jnp.where(keep, pltpu.roll(a, d, 0), 1.0)
        b_sh = jnp.where(keep, pltpu.roll(bt, d, 0), 0.0)
        bt = a * b_sh + bt
        a = a * a_sh
        d *= 2
    h = a * hcarry[0:1, :] + bt
    hcarry[0:1, :] = h[ts - 1:ts, :]

    y = jnp.dot((h * gate_ref[0].astype(F32)).astype(BF16), wout_ref[...], preferred_element_type=F32)
    o_ref[0] = _residual_ln(h_ref[0], y, g_ref[...], b_ref[...])


def _lru_scan(xpre, gate, h, cw, cb, gwr, gwi, gb, lam, wout, g, b, ts=256):
    bsz, seq, c = xpre.shape
    d = h.shape[-1]
    ts = min(ts, seq)
    row = lambda i, j: (i, j, 0)
    fix = lambda i, j: (0, 0)
    return pl.pallas_call(
        _lru_scan_kernel,
        grid=(bsz, seq // ts),
        in_specs=[pl.BlockSpec((1, ts, c), row),
                  pl.BlockSpec((1, ts, c), row),
                  pl.BlockSpec((1, ts, d), row),
                  pl.BlockSpec(cw.shape, fix), pl.BlockSpec(cb.shape, fix),
                  pl.BlockSpec(gwr.shape, fix), pl.BlockSpec(gwi.shape, fix),
                  pl.BlockSpec(gb.shape, fix), pl.BlockSpec(lam.shape, fix),
                  pl.BlockSpec(wout.shape, fix),
                  pl.BlockSpec((1, d), fix), pl.BlockSpec((1, d), fix)],
        out_specs=pl.BlockSpec((1, ts, d), row),
        out_shape=jax.ShapeDtypeStruct((bsz, seq, d), F32),
        scratch_shapes=[pltpu.VMEM((ts + SUBLANES, c), F32), pltpu.VMEM((SUBLANES, c), F32)],
        compiler_params=_cparams("parallel", "arbitrary"),
        name="lru_scan",
    )(xpre, gate, h, cw, cb, gwr, gwi, gb, lam, wout, g, b)


def _pad_last(x, n):
    return jnp.pad(x, [(0, 0)] * (x.ndim - 1) + [(0, n - x.shape[-1])])


def _rglru_sublayer(h, w_in, conv_w, conv_b, gate_w, gate_b, lam, w_out, g, b):
    bsz, seq, d = h.shape
    c = D_RNN_PAD
    w_cat = jnp.concatenate([_pad_last(w_in[:, :D_RNN], c), _pad_last(w_in[:, D_RNN:], c)], axis=1).astype(BF16)
    gate, xpre = _lru_in(h.reshape(bsz * seq, d), w_cat)
    eye = jnp.eye(LRU_BLOCKS, dtype=F32)
    dense = jnp.einsum('gncd,nm->gncmd', gate_w, eye).reshape(2, D_RNN, D_RNN)
    dense = jnp.pad(dense, ((0, 0), (0, c - D_RNN), (0, c - D_RNN))).astype(BF16)
    wout = jnp.pad(w_out, ((0, c - D_RNN), (0, 0))).astype(BF16)
    return _lru_scan(xpre.reshape(bsz, seq, c), gate.reshape(bsz, seq, c), h,
                     _pad_last(conv_w, c), _pad_last(conv_b[None, :], c), dense[0], dense[1],
                     _pad_last(gate_b, c), _pad_last(lam[None, :], c), wout, g, b)


def _rwkv_in_kernel(x_ref, xp_ref, mu_ref, wr_ref, wk_ref, wv_ref, w0_ref, w1_ref, w2_ref,
                    a0_ref, a1_ref, a2_ref, g1_ref, g2_ref,
                    r_ref, k_ref, v_ref, ld_ref, a_ref, g_ref):
    i = pl.program_id(1)
    x = x_ref[0]
    row = lax.broadcasted_iota(jnp.int32, x.shape, 0)
    prev = jnp.where(i > 0, xp_ref[0, SUBLANES - 1:SUBLANES, :], 0.0)
    xprev = jnp.where(row == 0, prev, pltpu.roll(x, 1, 0))
    xx = xprev - x
    mix = lambda c: (x + xx * mu_ref[c:c + 1, :]).astype(BF16)
    dot = lambda p, q: jnp.dot(p, q, preferred_element_type=F32)
    r_ref[0] = dot(mix(0), wr_ref[...])
    k_ref[0] = dot(mix(2), wk_ref[...])
    v_ref[0] = dot(mix(3), wv_ref[...])
    wl = w0_ref[...] + dot(jnp.tanh(dot(mix(1), w1_ref[...])).astype(BF16), w2_ref[...])
    w_log = -_softplus(-wl) - 0.5
    ld_ref[0] = -jnp.exp(w_log)
    a_ref[0] = _sigmoid(a0_ref[...] + dot(dot(mix(4), a1_ref[...]).astype(BF16), a2_ref[...]))
    g_ref[0] = dot(_sigmoid(dot(mix(5), g1_ref[...])).astype(BF16), g2_ref[...])


def _rwkv_in(h, mu, wr, wk, wv, w0, w1, w2, a0, a1, a2, g1, g2, tm=256):
    bsz, seq, d = h.shape
    tm = min(tm, seq)
    row = lambda b, i: (b, i, 0)
    fix = lambda b, i: (0, 0)
    prev = lambda b, i: (b, jnp.maximum(i * (tm // SUBLANES) - 1, 0), 0)
    ws = [mu, wr, wk, wv, w0, w1, w2, a0, a1, a2, g1, g2]
    return pl.pallas_call(
        _rwkv_in_kernel,
        grid=(bsz, seq // tm),
        in_specs=[pl.BlockSpec((1, tm, d), row), pl.BlockSpec((1, SUBLANES, d), prev)]
                 + [pl.BlockSpec(w.shape, fix) for w in ws],
        out_specs=[pl.BlockSpec((1, tm, d), row)] * 6,
        out_shape=[jax.ShapeDtypeStruct((bsz, seq, d), F32)] * 6,
        compiler_params=_cparams("parallel", "parallel"),
        name="rwkv_in",
    )(h, h, *ws)


def _bdot(a, b, dims):
    return jnp.einsum(dims, a, b, preferred_element_type=F32)


def _rwkv_rec_kernel(r_ref, k_ref, v_ref, ld_ref, a_ref, g_ref, kk_ref, ka_ref, rk_ref, lg_ref, lb_ref,
                     o_ref, state):
    L = r_ref.shape[1]
    P = RW_PAIRS
    N = RW_HEAD_SIZE
    j = pl.program_id(1)

    @pl.when(j == 0)
    def _():
        state[...] = jnp.zeros(state.shape, F32)

    pairs = lambda ref: jnp.stack([ref[0, :, LANES * p:LANES * (p + 1)] for p in range(P)])
    vec = lambda ref: jnp.stack([ref[:, LANES * p:LANES * (p + 1)] for p in range(P)])

    lane = lax.broadcasted_iota(jnp.int32, (1, 1, LANES), 2)
    head0 = lane < N
    ri = lax.broadcasted_iota(jnp.int32, (LANES, LANES), 0)
    ci = lax.broadcasted_iota(jnp.int32, (LANES, LANES), 1)
    seg = ((ri // N) == (ci // N)).astype(BF16)

    def segsum(x):
        flat = x.reshape(P * L, LANES)
        out = sum(jnp.dot(t, seg, preferred_element_type=F32) for t in _split3(flat))
        return out.reshape(P, L, LANES)

    r = pairs(r_ref)
    k = pairs(k_ref)
    v = pairs(v_ref)
    a = pairs(a_ref)

    kk = k * vec(kk_ref)
    kk = kk / jnp.maximum(jnp.sqrt(segsum(kk * kk)), 1e-12)
    k = k * (1.0 + (a - 1.0) * vec(ka_ref))
    aa = -kk
    bb = kk * a

    ld_full = ld_ref[0]
    tr = lax.broadcasted_iota(jnp.int32, (L, L), 0)
    tc = lax.broadcasted_iota(jnp.int32, (L, L), 1)
    tri = (tc <= tr).astype(BF16)
    cum_full = sum(jnp.dot(tri, t, preferred_element_type=F32) for t in _split3(ld_full))
    cum = jnp.stack([cum_full[:, LANES * p:LANES * (p + 1)] for p in range(P)])
    ld = pairs(ld_ref)
    gam = jnp.exp(cum)
    gam_prev = jnp.exp(cum - ld)
    cum_end = cum[:, L - 1:L, :]
    r_t = r * gam
    a_t = aa * gam_prev
    k_t = k * jnp.exp(-cum)
    b_t = bb * jnp.exp(-cum)
    k_e = k * jnp.exp(cum_end - cum)
    b_e = bb * jnp.exp(cum_end - cum)

    def stacked(x):
        return jnp.concatenate([jnp.where(head0, x, 0.0), jnp.where(head0, 0.0, x)], axis=1)

    ar = jnp.concatenate([stacked(a_t), stacked(r_t)], axis=1).astype(BF16)
    kb = jnp.concatenate([stacked(k_t), stacked(b_t)], axis=1).astype(BF16)
    v_s = stacked(v)
    s_old = state[...]

    s4 = _bdot(ar, kb, 'pik,pjk->pij')
    ah = _bdot(ar, s_old.astype(BF16), 'pik,plk->pil')
    M = 2 * L
    si = lax.broadcasted_iota(jnp.int32, (1, M, M), 1)
    sj = lax.broadcasted_iota(jnp.int32, (1, M, M), 2)
    same = (si // L) == (sj // L)
    strict = same & ((sj % L) < (si % L))
    incl = same & ((sj % L) <= (si % L))
    a_k = jnp.where(strict, s4[:, :M, :M], 0.0)
    a_b = jnp.where(strict, s4[:, :M, M:], 0.0)
    r_k = jnp.where(incl, s4[:, M:, :M], 0.0)
    r_b = jnp.where(incl, s4[:, M:, M:], 0.0)

    eye = (si == sj).astype(F32)
    lower1 = (si // 2 == sj // 2) & (si % 2 == 1) & (sj % 2 == 0)
    T = eye + jnp.where(lower1, a_b, 0.0)
    s = 2
    while s < L:
        cm = (si // (2 * s) == sj // (2 * s)) & (si % (2 * s) >= s) & (sj % (2 * s) < s)
        c_blk = jnp.where(cm, a_b, 0.0).astype(BF16)
        tb = T.astype(BF16)
        T = T + _bdot(_bdot(tb, c_blk, 'pij,pjk->pik').astype(BF16), tb, 'pij,pjk->pik')
        s *= 2

    x_s = ah[:, :M, :] + _bdot(a_k.astype(BF16), v_s.astype(BF16), 'pij,pjl->pil')
    u_s = _bdot(T.astype(BF16), x_s.astype(BF16), 'pij,pjl->pil')
    vu = jnp.concatenate([v_s, u_s], axis=1).astype(BF16)
    rkb = jnp.concatenate([r_k, r_b], axis=2).astype(BF16)
    y_s = ah[:, M:, :] + _bdot(rkb, vu, 'pij,pjl->pil')
    y = y_s[:, :L, :] + y_s[:, L:, :]

    kb_e = jnp.concatenate([stacked(k_e), stacked(b_e)], axis=1).astype(BF16)
    state[...] = s_old * jnp.exp(cum_end) + _bdot(vu, kb_e, 'pti,ptj->pij')

    mean = segsum(y) * (1.0 / N)
    yc = y - mean
    var = segsum(yc * yc) * (1.0 / N)
    yn = yc * lax.rsqrt(var + RW_GN_EPS) * vec(lg_ref) + vec(lb_ref)
    bonus = segsum(r * k * vec(rk_ref)) * v
    out = (yn + bonus) * pairs(g_ref)
    for p in range(P):
        o_ref[0, :, LANES * p:LANES * (p + 1)] = out[p].astype(o_ref.dtype)


def _rwkv_rec(r, k, v, ld, a, g, k_k, k_a, r_k, lnx_g, lnx_b):
    bsz, seq, d = r.shape
    L = RW_CHUNK
    row = lambda b, j: (b, j, 0)
    fix = lambda b, j: (0, 0)
    return pl.pallas_call(
        _rwkv_rec_kernel,
        grid=(bsz, seq // L),
        in_specs=[pl.BlockSpec((1, L, d), row)] * 6 + [pl.BlockSpec((1, d), fix)] * 5,
        out_specs=pl.BlockSpec((1, L, d), row),
        out_shape=jax.ShapeDtypeStruct((bsz, seq, d), BF16),
        scratch_shapes=[pltpu.VMEM((RW_PAIRS, LANES, LANES), F32)],
        compiler_params=_cparams("parallel", "arbitrary"),
        name="rwkv_rec",
    )(r, k, v, ld, a, g, k_k, k_a, r_k, lnx_g, lnx_b)


def _rwkv_sublayer(h, mu, w_r, w_k, w_v, w0, w1, w2, a0, a1, a2, g1, g2, k_k, k_a, r_k,
                   lnx_g, lnx_b, w_o, g, b):
    bsz, seq, d = h.shape
    bf = lambda w: w.astype(BF16)
    row = lambda w: w.reshape(1, d)
    mu8 = jnp.pad(mu, ((0, SUBLANES - mu.shape[0]), (0, 0)))
    r, k, v, ld, a, gg = _rwkv_in(h, mu8, bf(w_r), bf(w_k), bf(w_v), row(w0), bf(w1), bf(w2),
                                  row(a0), bf(a1), bf(a2), bf(g1), bf(g2))
    y = _rwkv_rec(r, k, v, ld, a, gg, row(k_k), row(k_a), row(r_k), row(lnx_g), row(lnx_b))
    return _proj_ln(y.reshape(bsz * seq, d), bf(w_o), h.reshape(bsz * seq, d), g, b).reshape(bsz, seq, d)


def _chunk_attn_kernel(q_ref, k_ref, v_ref, tbl_ref, o_ref):
    qi = pl.program_id(2)
    blocks_back = (ATT_WIN - ATT_TQ) // ATT_TQ
    back = jnp.minimum(qi, blocks_back)
    ws = pl.multiple_of((qi - back) * ATT_TQ, ATT_TQ)
    off = pl.multiple_of((blocks_back - back) * ATT_TQ, ATT_TQ)
    kw = k_ref[0, pl.ds(ws, ATT_WIN), :]
    vw = v_ref[0, pl.ds(ws, ATT_WIN), :]
    q = q_ref[0]
    lane = lax.broadcasted_iota(jnp.int32, q.shape, 1)
    outs = []
    for hh in range(2):
        in_head = (lane // ATT_HEAD_DIM) == hh
        qh = jnp.where(in_head, q, jnp.zeros_like(q))
        s = lax.dot_general(qh, kw, (((1,), (1,)), ((), ())), preferred_element_type=F32)
        s = s + tbl_ref[0, hh, :, pl.ds(off, ATT_WIN)]
        m = jnp.max(s, axis=-1, keepdims=True)
        e = jnp.exp(s - m)
        l = jnp.sum(e, axis=-1, keepdims=True)
        outs.append(jnp.dot(e.astype(BF16), vw, preferred_element_type=F32) / l)
    o_ref[0] = jnp.where(lane < ATT_HEAD_DIM, outs[0], outs[1]).astype(o_ref.dtype)


def _chunk_attn(qkv, tbl):
    bsz, seq, _ = qkv.shape
    nq = seq // ATT_TQ
    return pl.pallas_call(
        _chunk_attn_kernel,
        grid=(bsz, ATT_PAIRS, nq),
        in_specs=[pl.BlockSpec((1, ATT_TQ, LANES), lambda b, p, i: (b, i, p)),
                  pl.BlockSpec((1, seq, LANES), lambda b, p, i: (b, 0, ATT_PAIRS + p)),
                  pl.BlockSpec((1, seq, LANES), lambda b, p, i: (b, 0, 2 * ATT_PAIRS + p)),
                  pl.BlockSpec((1, 2, ATT_TQ, ATT_TBL), lambda b, p, i: (p, 0, 0, 0))],
        out_specs=pl.BlockSpec((1, ATT_TQ, LANES), lambda b, p, i: (b, i, p)),
        out_shape=jax.ShapeDtypeStruct((bsz, seq, D_MODEL), BF16),
        compiler_params=_cparams("parallel", "parallel", "arbitrary"),
        name="chunk_attn",
    )(qkv, qkv, qkv, tbl)


def _attn_bias_table(rel_bias):
    left = ATT_WIN - ATT_TQ
    i = jnp.arange(ATT_TQ)[:, None]
    m = jnp.arange(ATT_TBL)[None, :]
    rel = left + i - m
    bias = rel_bias[:, jnp.clip(rel, -MAX_REL, MAX_REL) + MAX_REL].astype(F32)
    ic, mc = i // CHUNK, m // CHUNK
    valid = (mc >= ic) & (mc <= ic + BAND_CHUNKS - 1)
    tbl = jnp.where(valid[None], bias, NEG_INF)
    return tbl.reshape(ATT_PAIRS, 2, ATT_TQ, ATT_TBL)


def _attn_sublayer(h, w_qkv, rel_bias, w_o, g, b):
    bsz, seq, d = h.shape
    assert seq % ATT_TQ == 0 and seq >= ATT_WIN
    scale = jnp.concatenate([jnp.full((d,), ATT_HEAD_DIM ** -0.5, F32), jnp.ones((2 * d,), F32)])
    qkv = _proj(h.reshape(bsz * seq, d), (w_qkv * scale).astype(BF16), BF16)
    o = _chunk_attn(qkv.reshape(bsz, seq, 3 * d), _attn_bias_table(rel_bias))
    return _proj_ln(o.reshape(bsz * seq, d), w_o.astype(BF16), h.reshape(bsz * seq, d), g, b).reshape(bsz, seq, d)


def _mem_attn_kernel(h_ref, kv_ref, wq_ref, wo_ref, g_ref, b_ref, o_ref):
    h = h_ref[0]
    q = jnp.dot(h.astype(BF16), wq_ref[...], preferred_element_type=F32).astype(BF16)
    outs = []
    for hh in range(MEM_HEADS):
        lo = hh * MEM_HEAD_DIM
        kh = kv_ref[0, :, lo:lo + MEM_HEAD_DIM]
        vh = kv_ref[0, :, D_MODEL + lo:D_MODEL + lo + MEM_HEAD_DIM]
        s = lax.dot_general(q[:, lo:lo + MEM_HEAD_DIM], kh, (((1,), (1,)), ((), ())),
                            preferred_element_type=F32)
        m = jnp.max(s, axis=-1, keepdims=True)
        e = jnp.exp(s - m)
        l = jnp.sum(e, axis=-1, keepdims=True)
        outs.append((jnp.dot(e.astype(BF16), vh, preferred_element_type=F32) / l).astype(BF16))
    o = jnp.concatenate(outs, axis=1)
    y = jnp.dot(o, wo_ref[...], preferred_element_type=F32)
    o_ref[0] = _residual_ln(h, y, g_ref[...], b_ref[...])


def _mem_sublayer(h, mem, w_q, w_kv, w_o, g, b, tm=512):
    bsz, seq, d = h.shape
    nm = mem.shape[1]
    tm = min(tm, seq)
    kv = _proj(mem.reshape(bsz * nm, d), w_kv.astype(BF16), BF16).reshape(bsz, nm, 2 * d)
    wq = (w_q * (MEM_HEAD_DIM ** -0.5)).astype(BF16)
    row = lambda b_, i: (b_, i, 0)
    fix = lambda b_, i: (0, 0)
    return pl.pallas_call(
        _mem_attn_kernel,
        grid=(bsz, seq // tm),
        in_specs=[pl.BlockSpec((1, tm, d), row),
                  pl.BlockSpec((1, nm, 2 * d), lambda b_, i: (b_, 0, 0)),
                  pl.BlockSpec((d, d), fix), pl.BlockSpec((d, d), fix),
                  pl.BlockSpec((1, d), fix), pl.BlockSpec((1, d), fix)],
        out_specs=pl.BlockSpec((1, tm, d), row),
        out_shape=jax.ShapeDtypeStruct((bsz, seq, d), F32),
        compiler_params=_cparams("parallel", "parallel"),
        name="mem_attn",
    )(h, kv, wq, w_o.astype(BF16), g, b)


def _mlp_kernel(h_ref, w1_ref, w2_ref, g_ref, b_ref, o_ref, acc):
    f = pl.program_id(1)
    u = jnp.dot(h_ref[...].astype(BF16), w1_ref[...], preferred_element_type=F32)
    u = jnp.maximum(u, 0.0)
    part = jnp.dot((u * u).astype(BF16), w2_ref[...], preferred_element_type=F32)

    @pl.when(f == 0)
    def _():
        acc[...] = part

    @pl.when(f > 0)
    def _():
        acc[...] += part

    @pl.when(f == pl.num_programs(1) - 1)
    def _():
        o_ref[...] = _residual_ln(h_ref[...], acc[...], g_ref[...], b_ref[...])


def _mlp_sublayer(h, w1, w2, g, b, tm=512, tf=1024):
    bsz, seq, d = h.shape
    t = bsz * seq
    tm = min(tm, t)
    ff = w1.shape[1]
    out = pl.pallas_call(
        _mlp_kernel,
        grid=(t // tm, ff // tf),
        in_specs=[pl.BlockSpec((tm, d), lambda i, f: (i, 0)),
                  pl.BlockSpec((d, tf), lambda i, f: (0, f)),
                  pl.BlockSpec((tf, d), lambda i, f: (f, 0)),
                  pl.BlockSpec((1, d), lambda i, f: (0, 0)),
                  pl.BlockSpec((1, d), lambda i, f: (0, 0))],
        out_specs=pl.BlockSpec((tm, d), lambda i, f: (i, 0)),
        out_shape=jax.ShapeDtypeStruct((t, d), F32),
        scratch_shapes=[pltpu.VMEM((tm, d), F32)],
        compiler_params=_cparams("parallel", "arbitrary"),
        name="mlp",
    )(h.reshape(t, d), w1.astype(BF16), w2.astype(BF16), g, b)
    return out.reshape(bsz, seq, d)


def kernel(x, mem, ln_g, ln_b, lru_w_in, lru_conv_w, lru_conv_b, lru_gate_w, lru_gate_b, lru_lambda, lru_w_out, rw_mu, rw_w_r, rw_w_k, rw_w_v, rw_w0, rw_w1, rw_w2, rw_a0, rw_a1, rw_a2, rw_g1, rw_g2, rw_k_k, rw_k_a, rw_r_k, rw_lnx_g, rw_lnx_b, rw_w_o, ca_w_qkv, ca_rel_bias, ca_w_o, mx_w_q, mx_w_kv, mx_w_o, mlp_w1, mlp_w2):
    h = x
    for i in range(DEPTH):
        kind, j = i % N_MIXERS, i // N_MIXERS
        g = lambda s: ln_g[i, s][None, :]
        b = lambda s: ln_b[i, s][None, :]
        if kind == 0:
            h = _rglru_sublayer(h, lru_w_in[j], lru_conv_w[j], lru_conv_b[j], lru_gate_w[j], lru_gate_b[j],
                                lru_lambda[j], lru_w_out[j], g(0), b(0))
        elif kind == 1:
            h = _rwkv_sublayer(h, rw_mu[j], rw_w_r[j], rw_w_k[j], rw_w_v[j], rw_w0[j], rw_w1[j], rw_w2[j],
                               rw_a0[j], rw_a1[j], rw_a2[j], rw_g1[j], rw_g2[j], rw_k_k[j], rw_k_a[j],
                               rw_r_k[j], rw_lnx_g[j], rw_lnx_b[j], rw_w_o[j], g(0), b(0))
        else:
            h = _attn_sublayer(h, ca_w_qkv[j], ca_rel_bias[j], ca_w_o[j], g(0), b(0))
        h = _mem_sublayer(h, mem, mx_w_q[i], mx_w_kv[i], mx_w_o[i], g(1), b(1))
        h = _mlp_sublayer(h, mlp_w1[i], mlp_w2[i], g(2), b(2))
    return h
```

```python
import functools

import jax
import jax.numpy as jnp
from jax import lax
from jax.experimental import pallas as pl
from jax.experimental.pallas import tpu as pltpu

F32 = jnp.float32
BF16 = jnp.bfloat16

D_MODEL = 1024
DEPTH = 4
CHUNK = 64
N_MIXERS = 3
DEEPNORM_ALPHA = (2 * DEPTH) ** 0.25
LN_EPS = 1e-5

D_RNN = 1344
LRU_BLOCKS = 16
LRU_BLOCK_SIZE = D_RNN // LRU_BLOCKS
CONV_WIDTH = 4
RG_LRU_C = 8.0
LANES = 128
SUBLANES = 8
D_RNN_PAD = -(-D_RNN // LANES) * LANES

RW_HEAD_SIZE = 64
RW_HEADS = D_MODEL // RW_HEAD_SIZE
RW_PAIRS = D_MODEL // LANES
RW_GN_EPS = 64e-5
RW_CHUNK = 64

ATT_HEADS = 16
ATT_HEAD_DIM = D_MODEL // ATT_HEADS
ATT_PAIRS = D_MODEL // LANES
BAND_CHUNKS = 9
MAX_REL = 2 * CHUNK
NEG_INF = -1e30
ATT_TQ = 256
ATT_WIN = ATT_TQ + (BAND_CHUNKS - 1) * CHUNK
ATT_TBL = ATT_WIN + (BAND_CHUNKS - 1) * CHUNK

MEM_HEADS = 4
MEM_HEAD_DIM = D_MODEL // MEM_HEADS
D_FF = 4 * D_MODEL

VMEM_LIMIT = 56 * 1024 * 1024


def _cparams(*sem):
    return pltpu.CompilerParams(dimension_semantics=sem, vmem_limit_bytes=VMEM_LIMIT)


def _residual_ln(hres, y, g, b):
    z = DEEPNORM_ALPHA * hres + y
    mu = jnp.mean(z, axis=-1, keepdims=True)
    zc = z - mu
    var = jnp.mean(zc * zc, axis=-1, keepdims=True)
    return zc * lax.rsqrt(var + LN_EPS) * g + b


def _gelu_tanh(x):
    return 0.5 * x * (1.0 + jnp.tanh(0.7978845608028654 * (x + 0.044715 * (x * x * x))))


def _softplus(x):
    return jnp.maximum(x, 0.0) + jnp.log1p(jnp.exp(-jnp.abs(x)))


def _sigmoid(x):
    return 0.5 * jnp.tanh(0.5 * x) + 0.5


def _split3(x):
    hi = x.astype(BF16)
    r1 = x - hi.astype(F32)
    mid = r1.astype(BF16)
    lo = (r1 - mid.astype(F32)).astype(BF16)
    return hi, mid, lo


def _proj_kernel(x_ref, w_ref, o_ref, *, chunk):
    x = x_ref[...].astype(BF16)
    n = w_ref.shape[1]
    for n0 in range(0, n, chunk):
        y = jnp.dot(x, w_ref[:, n0:n0 + chunk], preferred_element_type=F32)
        o_ref[:, n0:n0 + chunk] = y.astype(o_ref.dtype)


def _proj(x2d, w, out_dtype, tm=512, chunk=512):
    t, k = x2d.shape
    n = w.shape[1]
    tm = min(tm, t)
    return pl.pallas_call(
        functools.partial(_proj_kernel, chunk=chunk),
        grid=(t // tm,),
        in_specs=[pl.BlockSpec((tm, k), lambda i: (i, 0)),
                  pl.BlockSpec((k, n), lambda i: (0, 0))],
        out_specs=pl.BlockSpec((tm, n), lambda i: (i, 0)),
        out_shape=jax.ShapeDtypeStruct((t, n), out_dtype),
        compiler_params=_cparams("parallel"),
        name="proj",
    )(x2d, w)


def _proj_ln_kernel(y_ref, w_ref, h_ref, g_ref, b_ref, o_ref):
    y = jnp.dot(y_ref[...].astype(BF16), w_ref[...], preferred_element_type=F32)
    o_ref[...] = _residual_ln(h_ref[...], y, g_ref[...], b_ref[...])


def _proj_ln(y2d, w, h2d, g, b, tm=512):
    t, k = y2d.shape
    d = w.shape[1]
    tm = min(tm, t)
    return pl.pallas_call(
        _proj_ln_kernel,
        grid=(t // tm,),
        in_specs=[pl.BlockSpec((tm, k), lambda i: (i, 0)),
                  pl.BlockSpec((k, d), lambda i: (0, 0)),
                  pl.BlockSpec((tm, d), lambda i: (i, 0)),
                  pl.BlockSpec((1, d), lambda i: (0, 0)),
                  pl.BlockSpec((1, d), lambda i: (0, 0))],
        out_specs=pl.BlockSpec((tm, d), lambda i: (i, 0)),
        out_shape=jax.ShapeDtypeStruct((t, d), F32),
        compiler_params=_cparams("parallel"),
        name="proj_ln",
    )(y2d, w, h2d, g, b)


def _lru_in_kernel(x_ref, w_ref, gate_ref, xpre_ref):
    x = x_ref[...].astype(BF16)
    c = gate_ref.shape[1]
    gate_ref[...] = _gelu_tanh(jnp.dot(x, w_ref[:, :c], preferred_element_type=F32)).astype(gate_ref.dtype)
    xpre_ref[...] = jnp.dot(x, w_ref[:, c:], preferred_element_type=F32)


def _lru_in(x2d, w, tm=512):
    t, k = x2d.shape
    c = w.shape[1] // 2
    tm = min(tm, t)
    return pl.pallas_call(
        _lru_in_kernel,
        grid=(t // tm,),
        in_specs=[pl.BlockSpec((tm, k), lambda i: (i, 0)),
                  pl.BlockSpec((k, 2 * c), lambda i: (0, 0))],
        out_specs=[pl.BlockSpec((tm, c), lambda i: (i, 0)),
                   pl.BlockSpec((tm, c), lambda i: (i, 0))],
        out_shape=[jax.ShapeDtypeStruct((t, c), BF16), jax.ShapeDtypeStruct((t, c), F32)],
        compiler_params=_cparams("parallel"),
        name="lru_in",
    )(x2d, w)


def _lru_scan_kernel(xpre_ref, gate_ref, h_ref, cw_ref, cb_ref, gwr_ref, gwi_ref, gb_ref, lam_ref,
                     wout_ref, g_ref, b_ref, o_ref, xbuf, hcarry, ascan, bscan):
    ts = xpre_ref.shape[1]
    j = pl.program_id(1)

    @pl.when(j == 0)
    def _():
        xbuf[0:SUBLANES, :] = jnp.zeros((SUBLANES, xbuf.shape[1]), F32)
        hcarry[...] = jnp.zeros(hcarry.shape, F32)

    x = xpre_ref[0]
    xbuf[SUBLANES:SUBLANES + ts, :] = x
    xr = cb_ref[...] + cw_ref[3:4, :] * x
    for k in range(CONV_WIDTH - 1):
        off = SUBLANES - (CONV_WIDTH - 1) + k
        xr = xr + cw_ref[k:k + 1, :] * xbuf[off:off + ts, :]
    xbuf[0:SUBLANES, :] = xbuf[ts:ts + SUBLANES, :]

    xb = xr.astype(BF16)
    r_gate = _sigmoid(jnp.dot(xb, gwr_ref[...], preferred_element_type=F32) + gb_ref[0:1, :])
    i_gate = _sigmoid(jnp.dot(xb, gwi_ref[...], preferred_element_type=F32) + gb_ref[1:2, :])
    log_a = (-RG_LRU_C) * r_gate * _softplus(-lam_ref[...])
    a = jnp.exp(log_a)
    bt = jnp.sqrt(-jnp.tanh(log_a) * (a * a + 1.0)) * (i_gate * xr)

    def doubling_scan(av, bv, axis):
        n = av.shape[axis]
        idx = lax.broadcasted_iota(jnp.int32, av.shape, axis)
        d = 1
        while d < n:
            keep = idx >= d
            a_sh = jnp.where(keep, pltpu.roll(av, d, axis), 1.0)
            b_sh = jnp.where(keep, pltpu.roll(bv, d, axis), 0.0)
            bv = av * b_sh + bv
            av = av * a_sh
            d *= 2
        return av, bv

    groups = ts // SUBLANES
    c = a.shape[1]
    a3, b3 = doubling_scan(a.reshape(groups, SUBLANES, c), bt.reshape(groups, SUBLANES, c), 1)
    a2, b2 = a3.reshape(ts, c), b3.reshape(ts, c)
    last = pl.ds(SUBLANES - 1, groups, stride=SUBLANES)
    slabs = c // LANES
    for s in range(slabs):
        ascan[s] = a2[:, s * LANES:(s + 1) * LANES]
        bscan[s] = b2[:, s * LANES:(s + 1) * LANES]
    ag = jnp.concatenate([ascan[s, last, :] for s in range(slabs)], axis=1)
    bg = jnp.concatenate([bscan[s, last, :] for s in range(slabs)], axis=1)
    ag, bg = doubling_scan(ag, bg, 0)
    h_end = ag * hcarry[0:1, :] + bg
    grow = lax.broadcasted_iota(jnp.int32, h_end.shape, 0)
    h_in = jnp.where(grow == 0, hcarry[0:1, :], pltpu.roll(h_end, 1, 0))
    hcarry[0:1, :] = h_end[groups - 1:groups, :]
    h = (a3 * h_in[:, None, :] + b3).reshape(ts, c)

    y = jnp.dot((h * gate_ref[0].astype(F32)).astype(BF16), wout_ref[...], preferred_element_type=F32)
    o_ref[0] = _residual_ln(h_ref[0], y, g_ref[...], b_ref[...])


def _lru_scan(xpre, gate, h, cw, cb, gwr, gwi, gb, lam, wout, g, b, ts=256):
    bsz, seq, c = xpre.shape
    d = h.shape[-1]
    ts = min(ts, seq)
    row = lambda i, j: (i, j, 0)
    fix = lambda i, j: (0, 0)
    return pl.pallas_call(
        _lru_scan_kernel,
        grid=(bsz, seq // ts),
        in_specs=[pl.BlockSpec((1, ts, c), row),
                  pl.BlockSpec((1, ts, c), row),
                  pl.BlockSpec((1, ts, d), row),
                  pl.BlockSpec(cw.shape, fix), pl.BlockSpec(cb.shape, fix),
                  pl.BlockSpec(gwr.shape, fix), pl.BlockSpec(gwi.shape, fix),
                  pl.BlockSpec(gb.shape, fix), pl.BlockSpec(lam.shape, fix),
                  pl.BlockSpec(wout.shape, fix),
                  pl.BlockSpec((1, d), fix), pl.BlockSpec((1, d), fix)],
        out_specs=pl.BlockSpec((1, ts, d), row),
        out_shape=jax.ShapeDtypeStruct((bsz, seq, d), F32),
        scratch_shapes=[pltpu.VMEM((ts + SUBLANES, c), F32), pltpu.VMEM((SUBLANES, c), F32),
                        pltpu.VMEM((c // LANES, ts, LANES), F32), pltpu.VMEM((c // LANES, ts, LANES), F32)],
        compiler_params=_cparams("parallel", "arbitrary"),
        name="lru_scan",
    )(xpre, gate, h, cw, cb, gwr, gwi, gb, lam, wout, g, b)


def _pad_last(x, n):
    return jnp.pad(x, [(0, 0)] * (x.ndim - 1) + [(0, n - x.shape[-1])])


def _rglru_sublayer(h, w_in, conv_w, conv_b, gate_w, gate_b, lam, w_out, g, b):
    bsz, seq, d = h.shape
    c = D_RNN_PAD
    w_cat = jnp.concatenate([_pad_last(w_in[:, :D_RNN], c), _pad_last(w_in[:, D_RNN:], c)], axis=1).astype(BF16)
    gate, xpre = _lru_in(h.reshape(bsz * seq, d), w_cat)
    eye = jnp.eye(LRU_BLOCKS, dtype=F32)
    dense = jnp.einsum('gncd,nm->gncmd', gate_w, eye).reshape(2, D_RNN, D_RNN)
    dense = jnp.pad(dense, ((0, 0), (0, c - D_RNN), (0, c - D_RNN))).astype(BF16)
    wout = jnp.pad(w_out, ((0, c - D_RNN), (0, 0))).astype(BF16)
    return _lru_scan(xpre.reshape(bsz, seq, c), gate.reshape(bsz, seq, c), h,
                     _pad_last(conv_w, c), _pad_last(conv_b[None, :], c), dense[0], dense[1],
                     _pad_last(gate_b, c), _pad_last(lam[None, :], c), wout, g, b)


def _rwkv_in_kernel(x_ref, xp_ref, mu_ref, wr_ref, wk_ref, wv_ref, w0_ref, w1_ref, w2_ref,
                    a0_ref, a1_ref, a2_ref, g1_ref, g2_ref,
                    r_ref, k_ref, v_ref, ld_ref, a_ref, g_ref):
    i = pl.program_id(1)
    x = x_ref[0]
    row = lax.broadcasted_iota(jnp.int32, x.shape, 0)
    prev = jnp.where(i > 0, xp_ref[0, SUBLANES - 1:SUBLANES, :], 0.0)
    xprev = jnp.where(row == 0, prev, pltpu.roll(x, 1, 0))
    xx = xprev - x
    mix = lambda c: (x + xx * mu_ref[c:c + 1, :]).astype(BF16)
    dot = lambda p, q: jnp.dot(p, q, preferred_element_type=F32)
    r_ref[0] = dot(mix(0), wr_ref[...])
    k_ref[0] = dot(mix(2), wk_ref[...])
    v_ref[0] = dot(mix(3), wv_ref[...])
    wl = w0_ref[...] + dot(jnp.tanh(dot(mix(1), w1_ref[...])).astype(BF16), w2_ref[...])
    w_log = -_softplus(-wl) - 0.5
    ld_ref[0] = -jnp.exp(w_log)
    a_ref[0] = _sigmoid(a0_ref[...] + dot(dot(mix(4), a1_ref[...]).astype(BF16), a2_ref[...]))
    g_ref[0] = dot(_sigmoid(dot(mix(5), g1_ref[...])).astype(BF16), g2_ref[...])


def _rwkv_in(h, mu, wr, wk, wv, w0, w1, w2, a0, a1, a2, g1, g2, tm=256):
    bsz, seq, d = h.shape
    tm = min(tm, seq)
    row = lambda b, i: (b, i, 0)
    fix = lambda b, i: (0, 0)
    prev = lambda b, i: (b, jnp.maximum(i * (tm // SUBLANES) - 1, 0), 0)
    ws = [mu, wr, wk, wv, w0, w1, w2, a0, a1, a2, g1, g2]
    return pl.pallas_call(
        _rwkv_in_kernel,
        grid=(bsz, seq // tm),
        in_specs=[pl.BlockSpec((1, tm, d), row), pl.BlockSpec((1, SUBLANES, d), prev)]
                 + [pl.BlockSpec(w.shape, fix) for w in ws],
        out_specs=[pl.BlockSpec((1, tm, d), row)] * 6,
        out_shape=[jax.ShapeDtypeStruct((bsz, seq, d), F32)] * 6,
        compiler_params=_cparams("parallel", "parallel"),
        name="rwkv_in",
    )(h, h, *ws)


def _bdot(a, b, dims):
    return jnp.einsum(dims, a, b, preferred_element_type=F32)


def _rwkv_rec_kernel(r_ref, k_ref, v_ref, ld_ref, a_ref, g_ref, kk_ref, ka_ref, rk_ref, lg_ref, lb_ref,
                     o_ref, state):
    L = r_ref.shape[1]
    P = RW_PAIRS
    N = RW_HEAD_SIZE
    j = pl.program_id(1)

    @pl.when(j == 0)
    def _():
        state[...] = jnp.zeros(state.shape, F32)

    pairs = lambda ref: jnp.stack([ref[0, :, LANES * p:LANES * (p + 1)] for p in range(P)])
    vec = lambda ref: jnp.stack([ref[:, LANES * p:LANES * (p + 1)] for p in range(P)])

    lane = lax.broadcasted_iota(jnp.int32, (1, 1, LANES), 2)
    head0 = lane < N
    ri = lax.broadcasted_iota(jnp.int32, (LANES, LANES), 0)
    ci = lax.broadcasted_iota(jnp.int32, (LANES, LANES), 1)
    seg = ((ri // N) == (ci // N)).astype(BF16)

    def segsum(x):
        flat = x.reshape(P * L, LANES)
        out = sum(jnp.dot(t, seg, preferred_element_type=F32) for t in _split3(flat))
        return out.reshape(P, L, LANES)

    r = pairs(r_ref)
    k = pairs(k_ref)
    v = pairs(v_ref)
    a = pairs(a_ref)

    kk = k * vec(kk_ref)
    kk = kk / jnp.maximum(jnp.sqrt(segsum(kk * kk)), 1e-12)
    k = k * (1.0 + (a - 1.0) * vec(ka_ref))
    aa = -kk
    bb = kk * a

    ld_full = ld_ref[0]
    tr = lax.broadcasted_iota(jnp.int32, (L, L), 0)
    tc = lax.broadcasted_iota(jnp.int32, (L, L), 1)
    tri = (tc <= tr).astype(BF16)
    cum_full = sum(jnp.dot(tri, t, preferred_element_type=F32) for t in _split3(ld_full))
    cum = jnp.stack([cum_full[:, LANES * p:LANES * (p + 1)] for p in range(P)])
    ld = pairs(ld_ref)
    gam = jnp.exp(cum)
    gam_prev = jnp.exp(cum - ld)
    cum_end = cum[:, L - 1:L, :]
    r_t = r * gam
    a_t = aa * gam_prev
    k_t = k * jnp.exp(-cum)
    b_t = bb * jnp.exp(-cum)
    k_e = k * jnp.exp(cum_end - cum)
    b_e = bb * jnp.exp(cum_end - cum)

    def stacked(x):
        return jnp.concatenate([jnp.where(head0, x, 0.0), jnp.where(head0, 0.0, x)], axis=1)

    ar = jnp.concatenate([stacked(a_t), stacked(r_t)], axis=1).astype(BF16)
    kb = jnp.concatenate([stacked(k_t), stacked(b_t)], axis=1).astype(BF16)
    v_s = stacked(v)
    s_old = state[...]

    s4 = _bdot(ar, kb, 'pik,pjk->pij')
    ah = _bdot(ar, s_old.astype(BF16), 'pik,plk->pil')
    M = 2 * L
    si = lax.broadcasted_iota(jnp.int32, (1, M, M), 1)
    sj = lax.broadcasted_iota(jnp.int32, (1, M, M), 2)
    same = (si // L) == (sj // L)
    strict = same & ((sj % L) < (si % L))
    incl = same & ((sj % L) <= (si % L))
    a_k = jnp.where(strict, s4[:, :M, :M], 0.0)
    a_b = jnp.where(strict, s4[:, :M, M:], 0.0)
    r_k = jnp.where(incl, s4[:, M:, :M], 0.0)
    r_b = jnp.where(incl, s4[:, M:, M:], 0.0)

    eye = (si == sj).astype(F32)
    lower1 = (si // 2 == sj // 2) & (si % 2 == 1) & (sj % 2 == 0)
    T = eye + jnp.where(lower1, a_b, 0.0)
    s = 2
    while s < L:
        cm = (si // (2 * s) == sj // (2 * s)) & (si % (2 * s) >= s) & (sj % (2 * s) < s)
        c_blk = jnp.where(cm, a_b, 0.0).astype(BF16)
        tb = T.astype(BF16)
        T = T + _bdot(_bdot(tb, c_blk, 'pij,pjk->pik').astype(BF16), tb, 'pij,pjk->pik')
        s *= 2

    x_s = ah[:, :M, :] + _bdot(a_k.astype(BF16), v_s.astype(BF16), 'pij,pjl->pil')
    u_s = _bdot(T.astype(BF16), x_s.astype(BF16), 'pij,pjl->pil')
    vu = jnp.concatenate([v_s, u_s], axis=1).astype(BF16)
    rkb = jnp.concatenate([r_k, r_b], axis=2).astype(BF16)
    y_s = ah[:, M:, :] + _bdot(rkb, vu, 'pij,pjl->pil')
    y = y_s[:, :L, :] + y_s[:, L:, :]

    kb_e = jnp.concatenate([stacked(k_e), stacked(b_e)], axis=1).astype(BF16)
    state[...] = s_old * jnp.exp(cum_end) + _bdot(vu, kb_e, 'pti,ptj->pij')

    mean = segsum(y) * (1.0 / N)
    yc = y - mean
    var = segsum(yc * yc) * (1.0 / N)
    yn = yc * lax.rsqrt(var + RW_GN_EPS) * vec(lg_ref) + vec(lb_ref)
    bonus = segsum(r * k * vec(rk_ref)) * v
    out = (yn + bonus) * pairs(g_ref)
    for p in range(P):
        o_ref[0, :, LANES * p:LANES * (p + 1)] = out[p].astype(o_ref.dtype)


def _rwkv_rec(r, k, v, ld, a, g, k_k, k_a, r_k, lnx_g, lnx_b):
    bsz, seq, d = r.shape
    L = RW_CHUNK
    row = lambda b, j: (b, j, 0)
    fix = lambda b, j: (0, 0)
    return pl.pallas_call(
        _rwkv_rec_kernel,
        grid=(bsz, seq // L),
        in_specs=[pl.BlockSpec((1, L, d), row)] * 6 + [pl.BlockSpec((1, d), fix)] * 5,
        out_specs=pl.BlockSpec((1, L, d), row),
        out_shape=jax.ShapeDtypeStruct((bsz, seq, d), BF16),
        scratch_shapes=[pltpu.VMEM((RW_PAIRS, LANES, LANES), F32)],
        compiler_params=_cparams("parallel", "arbitrary"),
        name="rwkv_rec",
    )(r, k, v, ld, a, g, k_k, k_a, r_k, lnx_g, lnx_b)


def _rwkv_sublayer(h, mu, w_r, w_k, w_v, w0, w1, w2, a0, a1, a2, g1, g2, k_k, k_a, r_k,
                   lnx_g, lnx_b, w_o, g, b):
    bsz, seq, d = h.shape
    bf = lambda w: w.astype(BF16)
    row = lambda w: w.reshape(1, d)
    mu8 = jnp.pad(mu, ((0, SUBLANES - mu.shape[0]), (0, 0)))
    r, k, v, ld, a, gg = _rwkv_in(h, mu8, bf(w_r), bf(w_k), bf(w_v), row(w0), bf(w1), bf(w2),
                                  row(a0), bf(a1), bf(a2), bf(g1), bf(g2))
    y = _rwkv_rec(r, k, v, ld, a, gg, row(k_k), row(k_a), row(r_k), row(lnx_g), row(lnx_b))
    return _proj_ln(y.reshape(bsz * seq, d), bf(w_o), h.reshape(bsz * seq, d), g, b).reshape(bsz, seq, d)


def _chunk_attn_kernel(q_ref, k_ref, v_ref, tbl_ref, o_ref):
    seq = q_ref.shape[1]
    blocks_back = (ATT_WIN - ATT_TQ) // ATT_TQ
    lane = lax.broadcasted_iota(jnp.int32, (ATT_TQ, LANES), 1)
    for qi in range(seq // ATT_TQ):
        back = min(qi, blocks_back)
        ws = (qi - back) * ATT_TQ
        width = (back + 1) * ATT_TQ
        off = (blocks_back - back) * ATT_TQ
        rows = slice(qi * ATT_TQ, (qi + 1) * ATT_TQ)
        kw = k_ref[0, ws:ws + width, :]
        vw = v_ref[0, ws:ws + width, :]
        q = q_ref[0, rows, :]
        outs = []
        for hh in range(2):
            in_head = (lane // ATT_HEAD_DIM) == hh
            qh = jnp.where(in_head, q, jnp.zeros_like(q))
            s = lax.dot_general(qh, kw, (((1,), (1,)), ((), ())), preferred_element_type=F32)
            s = s + tbl_ref[0, hh, :, off:off + width]
            m = jnp.max(s, axis=-1, keepdims=True)
            e = jnp.exp(s - m)
            l = jnp.sum(e, axis=-1, keepdims=True)
            outs.append(jnp.dot(e.astype(BF16), vw, preferred_element_type=F32) / l)
        o_ref[0, rows, :] = jnp.where(lane < ATT_HEAD_DIM, outs[0], outs[1]).astype(o_ref.dtype)


def _chunk_attn(qkv, tbl):
    bsz, seq, _ = qkv.shape
    return pl.pallas_call(
        _chunk_attn_kernel,
        grid=(bsz, ATT_PAIRS),
        in_specs=[pl.BlockSpec((1, seq, LANES), lambda b, p: (b, 0, p)),
                  pl.BlockSpec((1, seq, LANES), lambda b, p: (b, 0, ATT_PAIRS + p)),
                  pl.BlockSpec((1, seq, LANES), lambda b, p: (b, 0, 2 * ATT_PAIRS + p)),
                  pl.BlockSpec((1, 2, ATT_TQ, ATT_TBL), lambda b, p: (p, 0, 0, 0))],
        out_specs=pl.BlockSpec((1, seq, LANES), lambda b, p: (b, 0, p)),
        out_shape=jax.ShapeDtypeStruct((bsz, seq, D_MODEL), BF16),
        compiler_params=_cparams("parallel", "parallel"),
        name="chunk_attn",
    )(qkv, qkv, qkv, tbl)


def _attn_bias_table(rel_bias):
    left = ATT_WIN - ATT_TQ
    period = ATT_TBL + ATT_TQ
    j = jnp.arange(period)
    diff = jnp.where(j < ATT_TBL, j, j - period)
    vec = rel_bias[:, jnp.clip(left - diff, -MAX_REL, MAX_REL) + MAX_REL].astype(F32)
    flat = jnp.tile(vec, (1, ATT_TQ))[:, :ATT_TQ * (period - 1)]
    bias = flat.reshape(-1, ATT_TQ, period - 1)[:, :, :ATT_TBL]
    ic = jnp.arange(ATT_TQ)[:, None] // CHUNK
    mc = jnp.arange(ATT_TBL)[None, :] // CHUNK
    valid = (mc >= ic) & (mc <= ic + BAND_CHUNKS - 1)
    tbl = jnp.where(valid[None], bias, NEG_INF)
    return tbl.reshape(ATT_PAIRS, 2, ATT_TQ, ATT_TBL)


def _attn_sublayer(h, w_qkv, rel_bias, w_o, g, b):
    bsz, seq, d = h.shape
    assert seq % ATT_TQ == 0 and seq >= ATT_WIN
    scale = jnp.concatenate([jnp.full((d,), ATT_HEAD_DIM ** -0.5, F32), jnp.ones((2 * d,), F32)])
    qkv = _proj(h.reshape(bsz * seq, d), (w_qkv * scale).astype(BF16), BF16)
    o = _chunk_attn(qkv.reshape(bsz, seq, 3 * d), _attn_bias_table(rel_bias))
    return _proj_ln(o.reshape(bsz * seq, d), w_o.astype(BF16), h.reshape(bsz * seq, d), g, b).reshape(bsz, seq, d)


def _mem_attn_kernel(h_ref, kv_ref, wq_ref, wo_ref, g_ref, b_ref, o_ref):
    h = h_ref[0]
    q = jnp.dot(h.astype(BF16), wq_ref[...], preferred_element_type=F32).astype(BF16)
    outs = []
    for hh in range(MEM_HEADS):
        lo = hh * MEM_HEAD_DIM
        kh = kv_ref[0, :, lo:lo + MEM_HEAD_DIM]
        vh = kv_ref[0, :, D_MODEL + lo:D_MODEL + lo + MEM_HEAD_DIM]
        s = lax.dot_general(q[:, lo:lo + MEM_HEAD_DIM], kh, (((1,), (1,)), ((), ())),
                            preferred_element_type=F32)
        m = jnp.max(s, axis=-1, keepdims=True)
        e = jnp.exp(s - m)
        l = jnp.sum(e, axis=-1, keepdims=True)
        outs.append((jnp.dot(e.astype(BF16), vh, preferred_element_type=F32) / l).astype(BF16))
    o = jnp.concatenate(outs, axis=1)
    y = jnp.dot(o, wo_ref[...], preferred_element_type=F32)
    o_ref[0] = _residual_ln(h, y, g_ref[...], b_ref[...])


def _mem_sublayer(h, mem, w_q, w_kv, w_o, g, b, tm=512):
    bsz, seq, d = h.shape
    nm = mem.shape[1]
    tm = min(tm, seq)
    kv = _proj(mem.reshape(bsz * nm, d), w_kv.astype(BF16), BF16).reshape(bsz, nm, 2 * d)
    wq = (w_q * (MEM_HEAD_DIM ** -0.5)).astype(BF16)
    row = lambda b_, i: (b_, i, 0)
    fix = lambda b_, i: (0, 0)
    return pl.pallas_call(
        _mem_attn_kernel,
        grid=(bsz, seq // tm),
        in_specs=[pl.BlockSpec((1, tm, d), row),
                  pl.BlockSpec((1, nm, 2 * d), lambda b_, i: (b_, 0, 0)),
                  pl.BlockSpec((d, d), fix), pl.BlockSpec((d, d), fix),
                  pl.BlockSpec((1, d), fix), pl.BlockSpec((1, d), fix)],
        out_specs=pl.BlockSpec((1, tm, d), row),
        out_shape=jax.ShapeDtypeStruct((bsz, seq, d), F32),
        compiler_params=_cparams("parallel", "parallel"),
        name="mem_attn",
    )(h, kv, wq, w_o.astype(BF16), g, b)


def _mlp_kernel(h_ref, w1_ref, w2_ref, g_ref, b_ref, o_ref, acc):
    f = pl.program_id(1)
    u = jnp.dot(h_ref[...].astype(BF16), w1_ref[...], preferred_element_type=F32)
    u = jnp.maximum(u, 0.0)
    part = jnp.dot((u * u).astype(BF16), w2_ref[...], preferred_element_type=F32)

    @pl.when(f == 0)
    def _():
        acc[...] = part

    @pl.when(f > 0)
    def _():
        acc[...] += part

    @pl.when(f == pl.num_programs(1) - 1)
    def _():
        o_ref[...] = _residual_ln(h_ref[...], acc[...], g_ref[...], b_ref[...])


def _mlp_sublayer(h, w1, w2, g, b, tm=1024, tf=1024):
    bsz, seq, d = h.shape
    t = bsz * seq
    tm = min(tm, t)
    ff = w1.shape[1]
    out = pl.pallas_call(
        _mlp_kernel,
        grid=(t // tm, ff // tf),
        in_specs=[pl.BlockSpec((tm, d), lambda i, f: (i, 0)),
                  pl.BlockSpec((d, tf), lambda i, f: (0, f)),
                  pl.BlockSpec((tf, d), lambda i, f: (f, 0)),
                  pl.BlockSpec((1, d), lambda i, f: (0, 0)),
                  pl.BlockSpec((1, d), lambda i, f: (0, 0))],
        out_specs=pl.BlockSpec((tm, d), lambda i, f: (i, 0)),
        out_shape=jax.ShapeDtypeStruct((t, d), F32),
        scratch_shapes=[pltpu.VMEM((tm, d), F32)],
        compiler_params=_cparams("parallel", "arbitrary"),
        name="mlp",
    )(h.reshape(t, d), w1.astype(BF16), w2.astype(BF16), g, b)
    return out.reshape(bsz, seq, d)


def kernel(x, mem, ln_g, ln_b, lru_w_in, lru_conv_w, lru_conv_b, lru_gate_w, lru_gate_b, lru_lambda, lru_w_out, rw_mu, rw_w_r, rw_w_k, rw_w_v, rw_w0, rw_w1, rw_w2, rw_a0, rw_a1, rw_a2, rw_g1, rw_g2, rw_k_k, rw_k_a, rw_r_k, rw_lnx_g, rw_lnx_b, rw_w_o, ca_w_qkv, ca_rel_bias, ca_w_o, mx_w_q, mx_w_kv, mx_w_o, mlp_w1, mlp_w2):
    h = x
    for i in range(DEPTH):
        kind, j = i % N_MIXERS, i // N_MIXERS
        g = lambda s: ln_g[i, s][None, :]
        b = lambda s: ln_b[i, s][None, :]
        if kind == 0:
            h = _rglru_sublayer(h, lru_w_in[j], lru_conv_w[j], lru_conv_b[j], lru_gate_w[j], lru_gate_b[j],
                                lru_lambda[j], lru_w_out[j], g(0), b(0))
        elif kind == 1:
            h = _rwkv_sublayer(h, rw_mu[j], rw_w_r[j], rw_w_k[j], rw_w_v[j], rw_w0[j], rw_w1[j], rw_w2[j],
                               rw_a0[j], rw_a1[j], rw_a2[j], rw_g1[j], rw_g2[j], rw_k_k[j], rw_k_a[j],
                               rw_r_k[j], rw_lnx_g[j], rw_lnx_b[j], rw_w_o[j], g(0), b(0))
        else:
            h = _attn_sublayer(h, ca_w_qkv[j], ca_rel_bias[j], ca_w_o[j], g(0), b(0))
        h = _mem_sublayer(h, mem, mx_w_q[i], mx_w_kv[i], mx_w_o[i], g(1), b(1))
        h = _mlp_sublayer(h, mlp_w1[i], mlp_w2[i], g(2), b(2))
    return h
```

```python
import functools

import jax
import jax.numpy as jnp
from jax import lax
from jax.experimental import pallas as pl
from jax.experimental.pallas import tpu as pltpu

F32 = jnp.float32
BF16 = jnp.bfloat16

D_MODEL = 1024
DEPTH = 4
CHUNK = 64
N_MIXERS = 3
DEEPNORM_ALPHA = (2 * DEPTH) ** 0.25
LN_EPS = 1e-5

D_RNN = 1344
LRU_BLOCKS = 16
LRU_BLOCK_SIZE = D_RNN // LRU_BLOCKS
CONV_WIDTH = 4
RG_LRU_C = 8.0
LANES = 128
SUBLANES = 8
D_RNN_PAD = -(-D_RNN // LANES) * LANES

RW_HEAD_SIZE = 64
RW_HEADS = D_MODEL // RW_HEAD_SIZE
RW_PAIRS = D_MODEL // LANES
RW_GN_EPS = 64e-5
RW_CHUNK = 64

ATT_HEADS = 16
ATT_HEAD_DIM = D_MODEL // ATT_HEADS
ATT_PAIRS = D_MODEL // LANES
BAND_CHUNKS = 9
MAX_REL = 2 * CHUNK
NEG_INF = -1e30
ATT_TQ = 256
ATT_WIN = ATT_TQ + (BAND_CHUNKS - 1) * CHUNK
ATT_TBL = ATT_WIN + (BAND_CHUNKS - 1) * CHUNK

MEM_HEADS = 4
MEM_HEAD_DIM = D_MODEL // MEM_HEADS
D_FF = 4 * D_MODEL

VMEM_LIMIT = 56 * 1024 * 1024


def _cparams(*sem):
    return pltpu.CompilerParams(dimension_semantics=sem, vmem_limit_bytes=VMEM_LIMIT)


def _residual_ln(hres, y, g, b):
    z = DEEPNORM_ALPHA * hres + y
    mu = jnp.mean(z, axis=-1, keepdims=True)
    zc = z - mu
    var = jnp.mean(zc * zc, axis=-1, keepdims=True)
    return zc * lax.rsqrt(var + LN_EPS) * g + b


def _gelu_tanh(x):
    return 0.5 * x * (1.0 + jnp.tanh(0.7978845608028654 * (x + 0.044715 * (x * x * x))))


def _softplus(x):
    return jnp.maximum(x, 0.0) + jnp.log1p(jnp.exp(-jnp.abs(x)))


def _sigmoid(x):
    return 0.5 * jnp.tanh(0.5 * x) + 0.5


def _split3(x):
    hi = x.astype(BF16)
    r1 = x - hi.astype(F32)
    mid = r1.astype(BF16)
    lo = (r1 - mid.astype(F32)).astype(BF16)
    return hi, mid, lo


def _proj_kernel(x_ref, w_ref, o_ref, *, chunk):
    x = x_ref[...].astype(BF16)
    n = w_ref.shape[1]
    for n0 in range(0, n, chunk):
        y = jnp.dot(x, w_ref[:, n0:n0 + chunk], preferred_element_type=F32)
        o_ref[:, n0:n0 + chunk] = y.astype(o_ref.dtype)


def _proj(x2d, w, out_dtype, tm=512, chunk=512):
    t, k = x2d.shape
    n = w.shape[1]
    tm = min(tm, t)
    return pl.pallas_call(
        functools.partial(_proj_kernel, chunk=chunk),
        grid=(t // tm,),
        in_specs=[pl.BlockSpec((tm, k), lambda i: (i, 0)),
                  pl.BlockSpec((k, n), lambda i: (0, 0))],
        out_specs=pl.BlockSpec((tm, n), lambda i: (i, 0)),
        out_shape=jax.ShapeDtypeStruct((t, n), out_dtype),
        compiler_params=_cparams("parallel"),
        name="proj",
    )(x2d, w)


def _proj_ln_kernel(y_ref, w_ref, h_ref, g_ref, b_ref, o_ref, *, sub):
    for r0 in range(0, y_ref.shape[0], sub):
        rows = slice(r0, r0 + sub)
        y = jnp.dot(y_ref[rows, :].astype(BF16), w_ref[...], preferred_element_type=F32)
        o_ref[rows, :] = _residual_ln(h_ref[rows, :], y, g_ref[...], b_ref[...])


def _proj_ln(y2d, w, h2d, g, b, tm=1024, sub=256):
    t, k = y2d.shape
    d = w.shape[1]
    tm = min(tm, t)
    return pl.pallas_call(
        functools.partial(_proj_ln_kernel, sub=min(sub, tm)),
        grid=(t // tm,),
        in_specs=[pl.BlockSpec((tm, k), lambda i: (i, 0)),
                  pl.BlockSpec((k, d), lambda i: (0, 0)),
                  pl.BlockSpec((tm, d), lambda i: (i, 0)),
                  pl.BlockSpec((1, d), lambda i: (0, 0)),
                  pl.BlockSpec((1, d), lambda i: (0, 0))],
        out_specs=pl.BlockSpec((tm, d), lambda i: (i, 0)),
        out_shape=jax.ShapeDtypeStruct((t, d), F32),
        compiler_params=_cparams("parallel"),
        name="proj_ln",
    )(y2d, w, h2d, g, b)


def _lru_in_kernel(x_ref, w_ref, gate_ref, xpre_ref):
    x = x_ref[...].astype(BF16)
    c = gate_ref.shape[1]
    gate_ref[...] = _gelu_tanh(jnp.dot(x, w_ref[:, :c], preferred_element_type=F32)).astype(gate_ref.dtype)
    xpre_ref[...] = jnp.dot(x, w_ref[:, c:], preferred_element_type=F32)


def _lru_in(x2d, w, tm=512):
    t, k = x2d.shape
    c = w.shape[1] // 2
    tm = min(tm, t)
    return pl.pallas_call(
        _lru_in_kernel,
        grid=(t // tm,),
        in_specs=[pl.BlockSpec((tm, k), lambda i: (i, 0)),
                  pl.BlockSpec((k, 2 * c), lambda i: (0, 0))],
        out_specs=[pl.BlockSpec((tm, c), lambda i: (i, 0)),
                   pl.BlockSpec((tm, c), lambda i: (i, 0))],
        out_shape=[jax.ShapeDtypeStruct((t, c), BF16), jax.ShapeDtypeStruct((t, c), F32)],
        compiler_params=_cparams("parallel"),
        name="lru_in",
    )(x2d, w)


def _lru_scan_kernel(xpre_ref, gate_ref, h_ref, cw_ref, cb_ref, gwr_ref, gwi_ref, gb_ref, lam_ref,
                     wout_ref, g_ref, b_ref, o_ref, xbuf, hcarry, ascan, bscan):
    ts = xpre_ref.shape[1]
    j = pl.program_id(1)

    @pl.when(j == 0)
    def _():
        xbuf[0:SUBLANES, :] = jnp.zeros((SUBLANES, xbuf.shape[1]), F32)
        hcarry[...] = jnp.zeros(hcarry.shape, F32)

    x = xpre_ref[0]
    xbuf[SUBLANES:SUBLANES + ts, :] = x
    xr = cb_ref[...] + cw_ref[3:4, :] * x
    for k in range(CONV_WIDTH - 1):
        off = SUBLANES - (CONV_WIDTH - 1) + k
        xr = xr + cw_ref[k:k + 1, :] * xbuf[off:off + ts, :]
    xbuf[0:SUBLANES, :] = xbuf[ts:ts + SUBLANES, :]

    xb = xr.astype(BF16)
    r_gate = _sigmoid(jnp.dot(xb, gwr_ref[...], preferred_element_type=F32) + gb_ref[0:1, :])
    i_gate = _sigmoid(jnp.dot(xb, gwi_ref[...], preferred_element_type=F32) + gb_ref[1:2, :])
    log_a = (-RG_LRU_C) * r_gate * _softplus(-lam_ref[...])
    a = jnp.exp(log_a)
    bt = jnp.sqrt(-jnp.tanh(log_a) * (a * a + 1.0)) * (i_gate * xr)

    def doubling_scan(av, bv, axis):
        n = av.shape[axis]
        idx = lax.broadcasted_iota(jnp.int32, av.shape, axis)
        d = 1
        while d < n:
            keep = idx >= d
            a_sh = jnp.where(keep, pltpu.roll(av, d, axis), 1.0)
            b_sh = jnp.where(keep, pltpu.roll(bv, d, axis), 0.0)
            bv = av * b_sh + bv
            av = av * a_sh
            d *= 2
        return av, bv

    groups = ts // SUBLANES
    c = a.shape[1]
    a3, b3 = doubling_scan(a.reshape(groups, SUBLANES, c), bt.reshape(groups, SUBLANES, c), 1)
    a2, b2 = a3.reshape(ts, c), b3.reshape(ts, c)
    last = pl.ds(SUBLANES - 1, groups, stride=SUBLANES)
    slabs = c // LANES
    for s in range(slabs):
        ascan[s] = a2[:, s * LANES:(s + 1) * LANES]
        bscan[s] = b2[:, s * LANES:(s + 1) * LANES]
    ag = jnp.concatenate([ascan[s, last, :] for s in range(slabs)], axis=1)
    bg = jnp.concatenate([bscan[s, last, :] for s in range(slabs)], axis=1)
    ag, bg = doubling_scan(ag, bg, 0)
    h_end = ag * hcarry[0:1, :] + bg
    grow = lax.broadcasted_iota(jnp.int32, h_end.shape, 0)
    h_in = jnp.where(grow == 0, hcarry[0:1, :], pltpu.roll(h_end, 1, 0))
    hcarry[0:1, :] = h_end[groups - 1:groups, :]
    h = (a3 * h_in[:, None, :] + b3).reshape(ts, c)

    y = jnp.dot((h * gate_ref[0].astype(F32)).astype(BF16), wout_ref[...], preferred_element_type=F32)
    o_ref[0] = _residual_ln(h_ref[0], y, g_ref[...], b_ref[...])


def _lru_scan(xpre, gate, h, cw, cb, gwr, gwi, gb, lam, wout, g, b, ts=256):
    bsz, seq, c = xpre.shape
    d = h.shape[-1]
    ts = min(ts, seq)
    row = lambda i, j: (i, j, 0)
    fix = lambda i, j: (0, 0)
    return pl.pallas_call(
        _lru_scan_kernel,
        grid=(bsz, seq // ts),
        in_specs=[pl.BlockSpec((1, ts, c), row),
                  pl.BlockSpec((1, ts, c), row),
                  pl.BlockSpec((1, ts, d), row),
                  pl.BlockSpec(cw.shape, fix), pl.BlockSpec(cb.shape, fix),
                  pl.BlockSpec(gwr.shape, fix), pl.BlockSpec(gwi.shape, fix),
                  pl.BlockSpec(gb.shape, fix), pl.BlockSpec(lam.shape, fix),
                  pl.BlockSpec(wout.shape, fix),
                  pl.BlockSpec((1, d), fix), pl.BlockSpec((1, d), fix)],
        out_specs=pl.BlockSpec((1, ts, d), row),
        out_shape=jax.ShapeDtypeStruct((bsz, seq, d), F32),
        scratch_shapes=[pltpu.VMEM((ts + SUBLANES, c), F32), pltpu.VMEM((SUBLANES, c), F32),
                        pltpu.VMEM((c // LANES, ts, LANES), F32), pltpu.VMEM((c // LANES, ts, LANES), F32)],
        compiler_params=_cparams("parallel", "arbitrary"),
        name="lru_scan",
    )(xpre, gate, h, cw, cb, gwr, gwi, gb, lam, wout, g, b)


def _pad_last(x, n):
    return jnp.pad(x, [(0, 0)] * (x.ndim - 1) + [(0, n - x.shape[-1])])


def _rglru_sublayer(h, w_in, conv_w, conv_b, gate_w, gate_b, lam, w_out, g, b):
    bsz, seq, d = h.shape
    c = D_RNN_PAD
    w_cat = jnp.concatenate([_pad_last(w_in[:, :D_RNN], c), _pad_last(w_in[:, D_RNN:], c)], axis=1).astype(BF16)
    gate, xpre = _lru_in(h.reshape(bsz * seq, d), w_cat)
    eye = jnp.eye(LRU_BLOCKS, dtype=F32)
    dense = jnp.einsum('gncd,nm->gncmd', gate_w, eye).reshape(2, D_RNN, D_RNN)
    dense = jnp.pad(dense, ((0, 0), (0, c - D_RNN), (0, c - D_RNN))).astype(BF16)
    wout = jnp.pad(w_out, ((0, c - D_RNN), (0, 0))).astype(BF16)
    return _lru_scan(xpre.reshape(bsz, seq, c), gate.reshape(bsz, seq, c), h,
                     _pad_last(conv_w, c), _pad_last(conv_b[None, :], c), dense[0], dense[1],
                     _pad_last(gate_b, c), _pad_last(lam[None, :], c), wout, g, b)


def _rwkv_in_kernel(x_ref, xp_ref, mu_ref, wr_ref, wk_ref, wv_ref, w0_ref, w1_ref, w2_ref,
                    a0_ref, a1_ref, a2_ref, g1_ref, g2_ref,
                    r_ref, k_ref, v_ref, ld_ref, a_ref, g_ref):
    i = pl.program_id(1)
    x = x_ref[0]
    row = lax.broadcasted_iota(jnp.int32, x.shape, 0)
    prev = jnp.where(i > 0, xp_ref[0, SUBLANES - 1:SUBLANES, :], 0.0)
    xprev = jnp.where(row == 0, prev, pltpu.roll(x, 1, 0))
    xx = xprev - x
    mix = lambda c: (x + xx * mu_ref[c:c + 1, :]).astype(BF16)
    dot = lambda p, q: jnp.dot(p, q, preferred_element_type=F32)
    r_ref[0] = dot(mix(0), wr_ref[...])
    k_ref[0] = dot(mix(2), wk_ref[...])
    v_ref[0] = dot(mix(3), wv_ref[...])
    wl = w0_ref[...] + dot(jnp.tanh(dot(mix(1), w1_ref[...])).astype(BF16), w2_ref[...])
    w_log = -_softplus(-wl) - 0.5
    ld_ref[0] = -jnp.exp(w_log)
    a_ref[0] = _sigmoid(a0_ref[...] + dot(dot(mix(4), a1_ref[...]).astype(BF16), a2_ref[...]))
    g_ref[0] = dot(_sigmoid(dot(mix(5), g1_ref[...])).astype(BF16), g2_ref[...])


def _rwkv_in(h, mu, wr, wk, wv, w0, w1, w2, a0, a1, a2, g1, g2, tm=256):
    bsz, seq, d = h.shape
    tm = min(tm, seq)
    row = lambda b, i: (b, i, 0)
    fix = lambda b, i: (0, 0)
    prev = lambda b, i: (b, jnp.maximum(i * (tm // SUBLANES) - 1, 0), 0)
    ws = [mu, wr, wk, wv, w0, w1, w2, a0, a1, a2, g1, g2]
    return pl.pallas_call(
        _rwkv_in_kernel,
        grid=(bsz, seq // tm),
        in_specs=[pl.BlockSpec((1, tm, d), row), pl.BlockSpec((1, SUBLANES, d), prev)]
                 + [pl.BlockSpec(w.shape, fix) for w in ws],
        out_specs=[pl.BlockSpec((1, tm, d), row)] * 6,
        out_shape=[jax.ShapeDtypeStruct((bsz, seq, d), F32)] * 6,
        compiler_params=_cparams("parallel", "parallel"),
        name="rwkv_in",
    )(h, h, *ws)


def _bdot(a, b, dims):
    return jnp.einsum(dims, a, b, preferred_element_type=F32)


def _rwkv_rec_kernel(r_ref, k_ref, v_ref, ld_ref, a_ref, g_ref, kk_ref, ka_ref, rk_ref, lg_ref, lb_ref,
                     o_ref, state):
    L = r_ref.shape[1]
    P = RW_PAIRS
    N = RW_HEAD_SIZE
    j = pl.program_id(1)

    @pl.when(j == 0)
    def _():
        state[...] = jnp.zeros(state.shape, F32)

    pairs = lambda ref: jnp.stack([ref[0, :, LANES * p:LANES * (p + 1)] for p in range(P)])
    vec = lambda ref: jnp.stack([ref[:, LANES * p:LANES * (p + 1)] for p in range(P)])

    lane = lax.broadcasted_iota(jnp.int32, (1, 1, LANES), 2)
    head0 = lane < N
    ri = lax.broadcasted_iota(jnp.int32, (LANES, LANES), 0)
    ci = lax.broadcasted_iota(jnp.int32, (LANES, LANES), 1)
    seg = ((ri // N) == (ci // N)).astype(BF16)

    def segsum(x):
        flat = x.reshape(P * L, LANES)
        hi = flat.astype(BF16)
        lo = (flat - hi.astype(F32)).astype(BF16)
        out = jnp.dot(hi, seg, preferred_element_type=F32) + jnp.dot(lo, seg, preferred_element_type=F32)
        return out.reshape(P, L, LANES)

    r = pairs(r_ref)
    k = pairs(k_ref)
    v = pairs(v_ref)
    a = pairs(a_ref)

    kk = k * vec(kk_ref)
    kk = kk / jnp.maximum(jnp.sqrt(segsum(kk * kk)), 1e-12)
    k = k * (1.0 + (a - 1.0) * vec(ka_ref))
    aa = -kk
    bb = kk * a

    ld_full = ld_ref[0]
    tr = lax.broadcasted_iota(jnp.int32, (L, L), 0)
    tc = lax.broadcasted_iota(jnp.int32, (L, L), 1)
    tri = (tc <= tr).astype(BF16)
    cum_full = sum(jnp.dot(tri, t, preferred_element_type=F32) for t in _split3(ld_full))
    cum = jnp.stack([cum_full[:, LANES * p:LANES * (p + 1)] for p in range(P)])
    ld = pairs(ld_ref)
    gam = jnp.exp(cum)
    gam_prev = jnp.exp(cum - ld)
    cum_end = cum[:, L - 1:L, :]
    r_t = r * gam
    a_t = aa * gam_prev
    k_t = k * jnp.exp(-cum)
    b_t = bb * jnp.exp(-cum)
    k_e = k * jnp.exp(cum_end - cum)
    b_e = bb * jnp.exp(cum_end - cum)

    def stacked(x):
        return jnp.concatenate([jnp.where(head0, x, 0.0), jnp.where(head0, 0.0, x)], axis=1)

    ar = jnp.concatenate([stacked(a_t), stacked(r_t)], axis=1).astype(BF16)
    kb = jnp.concatenate([stacked(k_t), stacked(b_t)], axis=1).astype(BF16)
    v_s = stacked(v)
    s_old = state[...]

    s4 = _bdot(ar, kb, 'pik,pjk->pij')
    ah = _bdot(ar, s_old.astype(BF16), 'pik,plk->pil')
    M = 2 * L
    si = lax.broadcasted_iota(jnp.int32, (1, M, M), 1)
    sj = lax.broadcasted_iota(jnp.int32, (1, M, M), 2)
    same = (si // L) == (sj // L)
    strict = same & ((sj % L) < (si % L))
    incl = same & ((sj % L) <= (si % L))
    a_k = jnp.where(strict, s4[:, :M, :M], 0.0)
    a_b = jnp.where(strict, s4[:, :M, M:], 0.0)
    r_k = jnp.where(incl, s4[:, M:, :M], 0.0)
    r_b = jnp.where(incl, s4[:, M:, M:], 0.0)

    eye = (si == sj).astype(F32)
    lower1 = (si // 2 == sj // 2) & (si % 2 == 1) & (sj % 2 == 0)
    T = eye + jnp.where(lower1, a_b, 0.0)
    ab16 = a_b.astype(BF16)
    s = 2
    while s < L:
        cm = (si // (2 * s) == sj // (2 * s)) & (si % (2 * s) >= s) & (sj % (2 * s) < s)
        tb = T.astype(BF16)
        tat = _bdot(_bdot(tb, ab16, 'pij,pjk->pik').astype(BF16), tb, 'pij,pjk->pik')
        T = T + jnp.where(cm, tat, 0.0)
        s *= 2

    x_s = ah[:, :M, :] + _bdot(a_k.astype(BF16), v_s.astype(BF16), 'pij,pjl->pil')
    u_s = _bdot(T.astype(BF16), x_s.astype(BF16), 'pij,pjl->pil')
    vu = jnp.concatenate([v_s, u_s], axis=1).astype(BF16)
    rkb = jnp.concatenate([r_k, r_b], axis=2).astype(BF16)
    y_s = ah[:, M:, :] + _bdot(rkb, vu, 'pij,pjl->pil')
    y = y_s[:, :L, :] + y_s[:, L:, :]

    kb_e = jnp.concatenate([stacked(k_e), stacked(b_e)], axis=1).astype(BF16)
    state[...] = s_old * jnp.exp(cum_end) + _bdot(vu, kb_e, 'pti,ptj->pij')

    mean = segsum(y) * (1.0 / N)
    yc = y - mean
    var = segsum(yc * yc) * (1.0 / N)
    yn = yc * lax.rsqrt(var + RW_GN_EPS) * vec(lg_ref) + vec(lb_ref)
    bonus = segsum(r * k * vec(rk_ref)) * v
    out = (yn + bonus) * pairs(g_ref)
    for p in range(P):
        o_ref[0, :, LANES * p:LANES * (p + 1)] = out[p].astype(o_ref.dtype)


def _rwkv_rec(r, k, v, ld, a, g, k_k, k_a, r_k, lnx_g, lnx_b):
    bsz, seq, d = r.shape
    L = RW_CHUNK
    row = lambda b, j: (b, j, 0)
    fix = lambda b, j: (0, 0)
    return pl.pallas_call(
        _rwkv_rec_kernel,
        grid=(bsz, seq // L),
        in_specs=[pl.BlockSpec((1, L, d), row)] * 6 + [pl.BlockSpec((1, d), fix)] * 5,
        out_specs=pl.BlockSpec((1, L, d), row),
        out_shape=jax.ShapeDtypeStruct((bsz, seq, d), BF16),
        scratch_shapes=[pltpu.VMEM((RW_PAIRS, LANES, LANES), F32)],
        compiler_params=_cparams("parallel", "arbitrary"),
        name="rwkv_rec",
    )(r, k, v, ld, a, g, k_k, k_a, r_k, lnx_g, lnx_b)


def _rwkv_sublayer(h, mu, w_r, w_k, w_v, w0, w1, w2, a0, a1, a2, g1, g2, k_k, k_a, r_k,
                   lnx_g, lnx_b, w_o, g, b):
    bsz, seq, d = h.shape
    bf = lambda w: w.astype(BF16)
    row = lambda w: w.reshape(1, d)
    mu8 = jnp.pad(mu, ((0, SUBLANES - mu.shape[0]), (0, 0)))
    r, k, v, ld, a, gg = _rwkv_in(h, mu8, bf(w_r), bf(w_k), bf(w_v), row(w0), bf(w1), bf(w2),
                                  row(a0), bf(a1), bf(a2), bf(g1), bf(g2))
    y = _rwkv_rec(r, k, v, ld, a, gg, row(k_k), row(k_a), row(r_k), row(lnx_g), row(lnx_b))
    return _proj_ln(y.reshape(bsz * seq, d), bf(w_o), h.reshape(bsz * seq, d), g, b).reshape(bsz, seq, d)


def _chunk_attn_kernel(q_ref, k_ref, v_ref, tbl_ref, o_ref):
    seq = q_ref.shape[1]
    blocks_back = (ATT_WIN - ATT_TQ) // ATT_TQ
    lane = lax.broadcasted_iota(jnp.int32, (ATT_TQ, LANES), 1)
    for qi in range(seq // ATT_TQ):
        back = min(qi, blocks_back)
        ws = (qi - back) * ATT_TQ
        width = (back + 1) * ATT_TQ
        off = (blocks_back - back) * ATT_TQ
        rows = slice(qi * ATT_TQ, (qi + 1) * ATT_TQ)
        kw = k_ref[0, ws:ws + width, :]
        vw = v_ref[0, ws:ws + width, :]
        q = q_ref[0, rows, :]
        outs = []
        for hh in range(2):
            in_head = (lane // ATT_HEAD_DIM) == hh
            qh = jnp.where(in_head, q, jnp.zeros_like(q))
            s = lax.dot_general(qh, kw, (((1,), (1,)), ((), ())), preferred_element_type=F32)
            s = s + tbl_ref[0, hh, :, off:off + width]
            m = jnp.max(s, axis=-1, keepdims=True)
            e = jnp.exp(s - m)
            l = jnp.sum(e, axis=-1, keepdims=True)
            outs.append(jnp.dot(e.astype(BF16), vw, preferred_element_type=F32) / l)
        o_ref[0, rows, :] = jnp.where(lane < ATT_HEAD_DIM, outs[0], outs[1]).astype(o_ref.dtype)


def _chunk_attn(qkv, tbl):
    bsz, seq, _ = qkv.shape
    return pl.pallas_call(
        _chunk_attn_kernel,
        grid=(bsz, ATT_PAIRS),
        in_specs=[pl.BlockSpec((1, seq, LANES), lambda b, p: (b, 0, p)),
                  pl.BlockSpec((1, seq, LANES), lambda b, p: (b, 0, ATT_PAIRS + p)),
                  pl.BlockSpec((1, seq, LANES), lambda b, p: (b, 0, 2 * ATT_PAIRS + p)),
                  pl.BlockSpec((1, 2, ATT_TQ, ATT_TBL), lambda b, p: (p, 0, 0, 0))],
        out_specs=pl.BlockSpec((1, seq, LANES), lambda b, p: (b, 0, p)),
        out_shape=jax.ShapeDtypeStruct((bsz, seq, D_MODEL), BF16),
        compiler_params=_cparams("parallel", "parallel"),
        name="chunk_attn",
    )(qkv, qkv, qkv, tbl)


def _attn_bias_table(rel_bias):
    left = ATT_WIN - ATT_TQ
    period = ATT_TBL + ATT_TQ
    j = jnp.arange(period)
    diff = jnp.where(j < ATT_TBL, j, j - period)
    vec = rel_bias[:, jnp.clip(left - diff, -MAX_REL, MAX_REL) + MAX_REL].astype(F32)
    flat = jnp.tile(vec, (1, ATT_TQ))[:, :ATT_TQ * (period - 1)]
    bias = flat.reshape(-1, ATT_TQ, period - 1)[:, :, :ATT_TBL]
    ic = jnp.arange(ATT_TQ)[:, None] // CHUNK
    mc = jnp.arange(ATT_TBL)[None, :] // CHUNK
    valid = (mc >= ic) & (mc <= ic + BAND_CHUNKS - 1)
    tbl = jnp.where(valid[None], bias, NEG_INF)
    return tbl.reshape(ATT_PAIRS, 2, ATT_TQ, ATT_TBL)


def _attn_sublayer(h, w_qkv, rel_bias, w_o, g, b):
    bsz, seq, d = h.shape
    assert seq % ATT_TQ == 0 and seq >= ATT_WIN
    scale = jnp.concatenate([jnp.full((d,), ATT_HEAD_DIM ** -0.5, F32), jnp.ones((2 * d,), F32)])
    qkv = _proj(h.reshape(bsz * seq, d), (w_qkv * scale).astype(BF16), BF16)
    o = _chunk_attn(qkv.reshape(bsz, seq, 3 * d), _attn_bias_table(rel_bias))
    return _proj_ln(o.reshape(bsz * seq, d), w_o.astype(BF16), h.reshape(bsz * seq, d), g, b).reshape(bsz, seq, d)


def _mem_attn_kernel(h_ref, kv_ref, wq_ref, wo_ref, g_ref, b_ref, o_ref, *, sub):
    for r0 in range(0, h_ref.shape[1], sub):
        rows = slice(r0, r0 + sub)
        h = h_ref[0, rows, :]
        q = jnp.dot(h.astype(BF16), wq_ref[...], preferred_element_type=F32).astype(BF16)
        outs = []
        for hh in range(MEM_HEADS):
            lo = hh * MEM_HEAD_DIM
            kh = kv_ref[0, :, lo:lo + MEM_HEAD_DIM]
            vh = kv_ref[0, :, D_MODEL + lo:D_MODEL + lo + MEM_HEAD_DIM]
            s = lax.dot_general(q[:, lo:lo + MEM_HEAD_DIM], kh, (((1,), (1,)), ((), ())),
                                preferred_element_type=F32)
            m = jnp.max(s, axis=-1, keepdims=True)
            e = jnp.exp(s - m)
            l = jnp.sum(e, axis=-1, keepdims=True)
            outs.append((jnp.dot(e.astype(BF16), vh, preferred_element_type=F32) / l).astype(BF16))
        o = jnp.concatenate(outs, axis=1)
        y = jnp.dot(o, wo_ref[...], preferred_element_type=F32)
        o_ref[0, rows, :] = _residual_ln(h, y, g_ref[...], b_ref[...])


def _mem_sublayer(h, mem, w_q, w_kv, w_o, g, b, tm=1024, sub=256):
    bsz, seq, d = h.shape
    nm = mem.shape[1]
    tm = min(tm, seq)
    kv = _proj(mem.reshape(bsz * nm, d), w_kv.astype(BF16), BF16).reshape(bsz, nm, 2 * d)
    wq = (w_q * (MEM_HEAD_DIM ** -0.5)).astype(BF16)
    row = lambda b_, i: (b_, i, 0)
    fix = lambda b_, i: (0, 0)
    return pl.pallas_call(
        functools.partial(_mem_attn_kernel, sub=min(sub, tm)),
        grid=(bsz, seq // tm),
        in_specs=[pl.BlockSpec((1, tm, d), row),
                  pl.BlockSpec((1, nm, 2 * d), lambda b_, i: (b_, 0, 0)),
                  pl.BlockSpec((d, d), fix), pl.BlockSpec((d, d), fix),
                  pl.BlockSpec((1, d), fix), pl.BlockSpec((1, d), fix)],
        out_specs=pl.BlockSpec((1, tm, d), row),
        out_shape=jax.ShapeDtypeStruct((bsz, seq, d), F32),
        compiler_params=_cparams("parallel", "parallel"),
        name="mem_attn",
    )(h, kv, wq, w_o.astype(BF16), g, b)


def _mlp_kernel(h_ref, w1_ref, w2_ref, g_ref, b_ref, o_ref, acc):
    f = pl.program_id(1)

    @pl.when(f == 0)
    def _():
        acc[...] = jnp.zeros(acc.shape, F32)

    u = jnp.dot(h_ref[...].astype(BF16), w1_ref[...], preferred_element_type=F32)
    u = jnp.maximum(u, 0.0)
    acc[...] += jnp.dot((u * u).astype(BF16), w2_ref[...], preferred_element_type=F32)

    @pl.when(f == pl.num_programs(1) - 1)
    def _():
        o_ref[...] = _residual_ln(h_ref[...], acc[...], g_ref[...], b_ref[...])


def _mlp_sublayer(h, w1, w2, g, b, tm=1024, tf=1024):
    bsz, seq, d = h.shape
    t = bsz * seq
    tm = min(tm, t)
    ff = w1.shape[1]
    out = pl.pallas_call(
        _mlp_kernel,
        grid=(t // tm, ff // tf),
        in_specs=[pl.BlockSpec((tm, d), lambda i, f: (i, 0)),
                  pl.BlockSpec((d, tf), lambda i, f: (0, f)),
                  pl.BlockSpec((tf, d), lambda i, f: (f, 0)),
                  pl.BlockSpec((1, d), lambda i, f: (0, 0)),
                  pl.BlockSpec((1, d), lambda i, f: (0, 0))],
        out_specs=pl.BlockSpec((tm, d), lambda i, f: (i, 0)),
        out_shape=jax.ShapeDtypeStruct((t, d), F32),
        scratch_shapes=[pltpu.VMEM((tm, d), F32)],
        compiler_params=_cparams("parallel", "arbitrary"),
        name="mlp",
    )(h.reshape(t, d), w1.astype(BF16), w2.astype(BF16), g, b)
    return out.reshape(bsz, seq, d)


def kernel(x, mem, ln_g, ln_b, lru_w_in, lru_conv_w, lru_conv_b, lru_gate_w, lru_gate_b, lru_lambda, lru_w_out, rw_mu, rw_w_r, rw_w_k, rw_w_v, rw_w0, rw_w1, rw_w2, rw_a0, rw_a1, rw_a2, rw_g1, rw_g2, rw_k_k, rw_k_a, rw_r_k, rw_lnx_g, rw_lnx_b, rw_w_o, ca_w_qkv, ca_rel_bias, ca_w_o, mx_w_q, mx_w_kv, mx_w_o, mlp_w1, mlp_w2):
    h = x
    for i in range(DEPTH):
        kind, j = i % N_MIXERS, i // N_MIXERS
        g = lambda s: ln_g[i, s][None, :]
        b = lambda s: ln_b[i, s][None, :]
        if kind == 0:
            h = _rglru_sublayer(h, lru_w_in[j], lru_conv_w[j], lru_conv_b[j], lru_gate_w[j], lru_gate_b[j],
                                lru_lambda[j], lru_w_out[j], g(0), b(0))
        elif kind == 1:
            h = _rwkv_sublayer(h, rw_mu[j], rw_w_r[j], rw_w_k[j], rw_w_v[j], rw_w0[j], rw_w1[j], rw_w2[j],
                               rw_a0[j], rw_a1[j], rw_a2[j], rw_g1[j], rw_g2[j], rw_k_k[j], rw_k_a[j],
                               rw_r_k[j], rw_lnx_g[j], rw_lnx_b[j], rw_w_o[j], g(0), b(0))
        else:
            h = _attn_sublayer(h, ca_w_qkv[j], ca_rel_bias[j], ca_w_o[j], g(0), b(0))
        h = _mem_sublayer(h, mem, mx_w_q[i], mx_w_kv[i], mx_w_o[i], g(1), b(1))
        h = _mlp_sublayer(h, mlp_w1[i], mlp_w2[i], g(2), b(2))
    return h
```

```python
import functools

import jax
import jax.numpy as jnp
from jax import lax
from jax.experimental import pallas as pl
from jax.experimental.pallas import tpu as pltpu

F32 = jnp.float32
BF16 = jnp.bfloat16

D_MODEL = 1024
DEPTH = 4
CHUNK = 64
N_MIXERS = 3
DEEPNORM_ALPHA = (2 * DEPTH) ** 0.25
LN_EPS = 1e-5

D_RNN = 1344
LRU_BLOCKS = 16
LRU_BLOCK_SIZE = D_RNN // LRU_BLOCKS
CONV_WIDTH = 4
RG_LRU_C = 8.0
LANES = 128
SUBLANES = 8
D_RNN_PAD = -(-D_RNN // LANES) * LANES

RW_HEAD_SIZE = 64
RW_HEADS = D_MODEL // RW_HEAD_SIZE
RW_PAIRS = D_MODEL // LANES
RW_GN_EPS = 64e-5
RW_CHUNK = 64

ATT_HEADS = 16
ATT_HEAD_DIM = D_MODEL // ATT_HEADS
ATT_PAIRS = D_MODEL // LANES
BAND_CHUNKS = 9
MAX_REL = 2 * CHUNK
NEG_INF = -1e30
ATT_TQ = 256
ATT_WIN = ATT_TQ + (BAND_CHUNKS - 1) * CHUNK
ATT_TBL = ATT_WIN + (BAND_CHUNKS - 1) * CHUNK

MEM_HEADS = 4
MEM_HEAD_DIM = D_MODEL // MEM_HEADS
D_FF = 4 * D_MODEL

VMEM_LIMIT = 56 * 1024 * 1024


def _cparams(*sem):
    return pltpu.CompilerParams(dimension_semantics=sem, vmem_limit_bytes=VMEM_LIMIT)


def _residual_ln(hres, y, g, b):
    z = DEEPNORM_ALPHA * hres + y
    mu = jnp.mean(z, axis=-1, keepdims=True)
    zc = z - mu
    var = jnp.mean(zc * zc, axis=-1, keepdims=True)
    return zc * lax.rsqrt(var + LN_EPS) * g + b


def _gelu_tanh(x):
    return 0.5 * x * (1.0 + jnp.tanh(0.7978845608028654 * (x + 0.044715 * (x * x * x))))


def _softplus(x):
    return jnp.maximum(x, 0.0) + jnp.log1p(jnp.exp(-jnp.abs(x)))


def _sigmoid(x):
    return 0.5 * jnp.tanh(0.5 * x) + 0.5


def _split3(x):
    hi = x.astype(BF16)
    r1 = x - hi.astype(F32)
    mid = r1.astype(BF16)
    lo = (r1 - mid.astype(F32)).astype(BF16)
    return hi, mid, lo


def _proj_kernel(x_ref, w_ref, o_ref, *, chunk):
    x = x_ref[...].astype(BF16)
    n = w_ref.shape[1]
    for n0 in range(0, n, chunk):
        y = jnp.dot(x, w_ref[:, n0:n0 + chunk], preferred_element_type=F32)
        o_ref[:, n0:n0 + chunk] = y.astype(o_ref.dtype)


def _proj(x2d, w, out_dtype, tm=512, chunk=512):
    t, k = x2d.shape
    n = w.shape[1]
    tm = min(tm, t)
    return pl.pallas_call(
        functools.partial(_proj_kernel, chunk=chunk),
        grid=(t // tm,),
        in_specs=[pl.BlockSpec((tm, k), lambda i: (i, 0)),
                  pl.BlockSpec((k, n), lambda i: (0, 0))],
        out_specs=pl.BlockSpec((tm, n), lambda i: (i, 0)),
        out_shape=jax.ShapeDtypeStruct((t, n), out_dtype),
        compiler_params=_cparams("parallel"),
        name="proj",
    )(x2d, w)


def _proj_ln_kernel(y_ref, w_ref, h_ref, g_ref, b_ref, o_ref, *, sub):
    for r0 in range(0, y_ref.shape[0], sub):
        rows = slice(r0, r0 + sub)
        y = jnp.dot(y_ref[rows, :].astype(BF16), w_ref[...], preferred_element_type=F32)
        o_ref[rows, :] = _residual_ln(h_ref[rows, :], y, g_ref[...], b_ref[...])


def _proj_ln(y2d, w, h2d, g, b, tm=1024, sub=256):
    t, k = y2d.shape
    d = w.shape[1]
    tm = min(tm, t)
    return pl.pallas_call(
        functools.partial(_proj_ln_kernel, sub=min(sub, tm)),
        grid=(t // tm,),
        in_specs=[pl.BlockSpec((tm, k), lambda i: (i, 0)),
                  pl.BlockSpec((k, d), lambda i: (0, 0)),
                  pl.BlockSpec((tm, d), lambda i: (i, 0)),
                  pl.BlockSpec((1, d), lambda i: (0, 0)),
                  pl.BlockSpec((1, d), lambda i: (0, 0))],
        out_specs=pl.BlockSpec((tm, d), lambda i: (i, 0)),
        out_shape=jax.ShapeDtypeStruct((t, d), F32),
        compiler_params=_cparams("parallel"),
        name="proj_ln",
    )(y2d, w, h2d, g, b)


LRU_GATE_TILE = 256
LRU_GATE_WIN = 512


def _lru_gate_window_starts():
    starts = []
    for n0 in range(0, D_RNN_PAD, LRU_GATE_TILE):
        first_block = min(n0, D_RNN - 1) // LRU_BLOCK_SIZE
        last_block = min(n0 + LRU_GATE_TILE - 1, D_RNN - 1) // LRU_BLOCK_SIZE
        lo = (first_block * LRU_BLOCK_SIZE) // LANES * LANES
        lo = min(lo, D_RNN_PAD - LRU_GATE_WIN)
        assert (last_block + 1) * LRU_BLOCK_SIZE <= lo + LRU_GATE_WIN
        starts.append(lo)
    return starts


def _lru_kernel(h_ref, win_ref, cw_ref, cb_ref, gw_ref, gb_ref, lam_ref, wout_ref, g_ref, b_ref, o_ref,
                xbuf, hcarry, ascan, bscan):
    @pl.when(pl.program_id(1) == 0)
    def _():
        xbuf[:, 0:SUBLANES, :] = jnp.zeros((xbuf.shape[0], SUBLANES, xbuf.shape[2]), F32)
        hcarry[...] = jnp.zeros(hcarry.shape, F32)

    for bb in range(h_ref.shape[0]):
        _lru_one_row(bb, h_ref, win_ref, cw_ref, cb_ref, gw_ref, gb_ref, lam_ref, wout_ref, g_ref, b_ref,
                     o_ref, xbuf, hcarry, ascan, bscan)


def _lru_one_row(bb, h_ref, win_ref, cw_ref, cb_ref, gw_ref, gb_ref, lam_ref, wout_ref, g_ref, b_ref, o_ref,
                 xbuf, hcarry, ascan, bscan):
    ts = h_ref.shape[1]
    c = cw_ref.shape[1]
    hres = h_ref[bb]
    u = jnp.dot(hres.astype(BF16), win_ref[...], preferred_element_type=F32)
    gate_branch = _gelu_tanh(u[:, :c])
    x = u[:, c:]
    xbuf[bb, SUBLANES:SUBLANES + ts, :] = x
    xr = cb_ref[...] + cw_ref[3:4, :] * x
    for k in range(CONV_WIDTH - 1):
        off = SUBLANES - (CONV_WIDTH - 1) + k
        xr = xr + cw_ref[k:k + 1, :] * xbuf[bb, off:off + ts, :]
    xbuf[bb, 0:SUBLANES, :] = xbuf[bb, ts:ts + SUBLANES, :]

    xb = xr.astype(BF16)
    tiles = [jnp.dot(xb[:, lo:lo + LRU_GATE_WIN], gw_ref[n], preferred_element_type=F32)
             for n, lo in enumerate(_lru_gate_window_starts())]
    r_pre = jnp.concatenate([t[:, :LRU_GATE_TILE] for t in tiles], axis=1)[:, :c]
    i_pre = jnp.concatenate([t[:, LRU_GATE_TILE:] for t in tiles], axis=1)[:, :c]
    r_gate = _sigmoid(r_pre + gb_ref[0:1, :])
    i_gate = _sigmoid(i_pre + gb_ref[1:2, :])
    log_a = (-RG_LRU_C) * r_gate * _softplus(-lam_ref[...])
    a = jnp.exp(log_a)
    bt = jnp.sqrt(-jnp.tanh(log_a) * (a * a + 1.0)) * (i_gate * xr)

    def doubling_scan(av, bv, axis):
        n = av.shape[axis]
        idx = lax.broadcasted_iota(jnp.int32, av.shape, axis)
        d = 1
        while d < n:
            keep = idx >= d
            a_sh = jnp.where(keep, pltpu.roll(av, d, axis), 1.0)
            b_sh = jnp.where(keep, pltpu.roll(bv, d, axis), 0.0)
            bv = av * b_sh + bv
            av = av * a_sh
            d *= 2
        return av, bv

    groups = ts // SUBLANES
    a3, b3 = doubling_scan(a.reshape(groups, SUBLANES, c), bt.reshape(groups, SUBLANES, c), 1)
    a2, b2 = a3.reshape(ts, c), b3.reshape(ts, c)
    last = pl.ds(SUBLANES - 1, groups, stride=SUBLANES)
    slabs = c // LANES
    for s in range(slabs):
        ascan[bb, s] = a2[:, s * LANES:(s + 1) * LANES]
        bscan[bb, s] = b2[:, s * LANES:(s + 1) * LANES]
    ag = jnp.concatenate([ascan[bb, s, last, :] for s in range(slabs)], axis=1)
    bg = jnp.concatenate([bscan[bb, s, last, :] for s in range(slabs)], axis=1)
    ag, bg = doubling_scan(ag, bg, 0)
    carry = hcarry[bb, 0:1, :]
    h_end = ag * carry + bg
    grow = lax.broadcasted_iota(jnp.int32, h_end.shape, 0)
    h_in = jnp.where(grow == 0, carry, pltpu.roll(h_end, 1, 0))
    hcarry[bb, 0:1, :] = h_end[groups - 1:groups, :]
    h = (a3 * h_in[:, None, :] + b3).reshape(ts, c)

    y = jnp.dot((h * gate_branch).astype(BF16), wout_ref[...], preferred_element_type=F32)
    o_ref[bb] = _residual_ln(hres, y, g_ref[...], b_ref[...])


def _lru(h, win, cw, cb, gw, gb, lam, wout, g, b, ts=128, nb=2):
    bsz, seq, d = h.shape
    c = cw.shape[1]
    ts = min(ts, seq)
    nb = min(nb, bsz)
    row = lambda i, j: (i, j, 0)
    resident = lambda w: pl.BlockSpec(w.shape, lambda i, j: (0,) * w.ndim, pipeline_mode=pl.Buffered(1))
    return pl.pallas_call(
        _lru_kernel,
        grid=(bsz // nb, seq // ts),
        in_specs=[pl.BlockSpec((nb, ts, d), row)] + [resident(w) for w in (win, cw, cb, gw, gb, lam, wout, g, b)],
        out_specs=pl.BlockSpec((nb, ts, d), row),
        out_shape=jax.ShapeDtypeStruct((bsz, seq, d), F32),
        scratch_shapes=[pltpu.VMEM((nb, ts + SUBLANES, c), F32), pltpu.VMEM((nb, SUBLANES, c), F32),
                        pltpu.VMEM((nb, c // LANES, ts, LANES), F32),
                        pltpu.VMEM((nb, c // LANES, ts, LANES), F32)],
        compiler_params=_cparams("parallel", "arbitrary"),
        name="lru",
    )(h, win, cw, cb, gw, gb, lam, wout, g, b)


def _pad_last(x, n):
    return jnp.pad(x, [(0, 0)] * (x.ndim - 1) + [(0, n - x.shape[-1])])


def _rglru_sublayer(h, w_in, conv_w, conv_b, gate_w, gate_b, lam, w_out, g, b):
    bsz, seq, d = h.shape
    c = D_RNN_PAD
    w_cat = jnp.concatenate([_pad_last(w_in[:, :D_RNN], c), _pad_last(w_in[:, D_RNN:], c)], axis=1).astype(BF16)
    eye = jnp.eye(LRU_BLOCKS, dtype=F32)
    dense = jnp.einsum('gncd,nm->gncmd', gate_w, eye).reshape(2, D_RNN, D_RNN)
    starts = _lru_gate_window_starts()
    dense = jnp.pad(dense, ((0, 0), (0, c - D_RNN), (0, len(starts) * LRU_GATE_TILE - D_RNN)))
    gw = jnp.stack([jnp.concatenate([dense[gi, lo:lo + LRU_GATE_WIN, n * LRU_GATE_TILE:(n + 1) * LRU_GATE_TILE]
                                     for gi in range(2)], axis=1) for n, lo in enumerate(starts)]).astype(BF16)
    wout = jnp.pad(w_out, ((0, c - D_RNN), (0, 0))).astype(BF16)
    return _lru(h, w_cat, _pad_last(conv_w, c), _pad_last(conv_b[None, :], c), gw,
                _pad_last(gate_b, c), _pad_last(lam[None, :], c), wout, g, b)


def _rwkv_in_kernel(x_ref, xp_ref, mu_ref, wr_ref, wk_ref, wv_ref, w0_ref, w1_ref, w2_ref,
                    a0_ref, a1_ref, a2_ref, g1_ref, g2_ref,
                    r_ref, k_ref, v_ref, ld_ref, a_ref, g_ref):
    i = pl.program_id(1)
    x = x_ref[0]
    row = lax.broadcasted_iota(jnp.int32, x.shape, 0)
    prev = jnp.where(i > 0, xp_ref[0, SUBLANES - 1:SUBLANES, :], 0.0)
    xprev = jnp.where(row == 0, prev, pltpu.roll(x, 1, 0))
    xx = xprev - x
    mix = lambda c: (x + xx * mu_ref[c:c + 1, :]).astype(BF16)
    dot = lambda p, q: jnp.dot(p, q, preferred_element_type=F32)
    r_ref[0] = dot(mix(0), wr_ref[...]).astype(r_ref.dtype)
    k_ref[0] = dot(mix(2), wk_ref[...]).astype(k_ref.dtype)
    v_ref[0] = dot(mix(3), wv_ref[...]).astype(v_ref.dtype)
    wl = w0_ref[...] + dot(jnp.tanh(dot(mix(1), w1_ref[...])).astype(BF16), w2_ref[...])
    w_log = -_softplus(-wl) - 0.5
    ld_ref[0] = -jnp.exp(w_log)
    a_ref[0] = _sigmoid(a0_ref[...] + dot(dot(mix(4), a1_ref[...]).astype(BF16), a2_ref[...]))
    g_ref[0] = dot(_sigmoid(dot(mix(5), g1_ref[...])).astype(BF16), g2_ref[...]).astype(g_ref.dtype)


def _rwkv_in(h, mu, wr, wk, wv, w0, w1, w2, a0, a1, a2, g1, g2, tm=256):
    bsz, seq, d = h.shape
    tm = min(tm, seq)
    row = lambda b, i: (b, i, 0)
    fix = lambda b, i: (0, 0)
    prev = lambda b, i: (b, jnp.maximum(i * (tm // SUBLANES) - 1, 0), 0)
    ws = [mu, wr, wk, wv, w0, w1, w2, a0, a1, a2, g1, g2]
    return pl.pallas_call(
        _rwkv_in_kernel,
        grid=(bsz, seq // tm),
        in_specs=[pl.BlockSpec((1, tm, d), row), pl.BlockSpec((1, SUBLANES, d), prev)]
                 + [pl.BlockSpec(w.shape, fix) for w in ws],
        out_specs=[pl.BlockSpec((1, tm, d), row)] * 6,
        out_shape=[jax.ShapeDtypeStruct((bsz, seq, d), dt) for dt in (BF16, BF16, BF16, F32, F32, BF16)],
        compiler_params=_cparams("parallel", "parallel"),
        name="rwkv_in",
    )(h, h, *ws)


def _bdot(a, b, dims):
    return jnp.einsum(dims, a, b, preferred_element_type=F32)


def _rwkv_rec_kernel(r_ref, k_ref, v_ref, ld_ref, a_ref, g_ref, kk_ref, ka_ref, rk_ref, lg_ref, lb_ref,
                     o_ref, state):
    L = r_ref.shape[1]
    P = RW_PAIRS
    N = RW_HEAD_SIZE
    j = pl.program_id(1)

    @pl.when(j == 0)
    def _():
        state[...] = jnp.zeros(state.shape, F32)

    pairs = lambda ref: jnp.stack([ref[0, :, LANES * p:LANES * (p + 1)] for p in range(P)]).astype(F32)
    vec = lambda ref: jnp.stack([ref[:, LANES * p:LANES * (p + 1)] for p in range(P)])

    lane = lax.broadcasted_iota(jnp.int32, (1, 1, LANES), 2)
    head0 = lane < N
    ri = lax.broadcasted_iota(jnp.int32, (LANES, LANES), 0)
    ci = lax.broadcasted_iota(jnp.int32, (LANES, LANES), 1)
    seg = ((ri // N) == (ci // N)).astype(BF16)

    def segsum(x):
        flat = x.reshape(P * L, LANES)
        hi = flat.astype(BF16)
        lo = (flat - hi.astype(F32)).astype(BF16)
        out = jnp.dot(hi, seg, preferred_element_type=F32) + jnp.dot(lo, seg, preferred_element_type=F32)
        return out.reshape(P, L, LANES)

    r = pairs(r_ref)
    k = pairs(k_ref)
    v = pairs(v_ref)
    a = pairs(a_ref)

    kk = k * vec(kk_ref)
    kk = kk / jnp.maximum(jnp.sqrt(segsum(kk * kk)), 1e-12)
    k = k * (1.0 + (a - 1.0) * vec(ka_ref))
    aa = -kk
    bb = kk * a

    ld_full = ld_ref[0]
    tr = lax.broadcasted_iota(jnp.int32, (L, L), 0)
    tc = lax.broadcasted_iota(jnp.int32, (L, L), 1)
    tri = (tc <= tr).astype(BF16)
    cum_full = sum(jnp.dot(tri, t, preferred_element_type=F32) for t in _split3(ld_full))
    cum = jnp.stack([cum_full[:, LANES * p:LANES * (p + 1)] for p in range(P)])
    ld = pairs(ld_ref)
    gam = jnp.exp(cum)
    gam_prev = jnp.exp(cum - ld)
    cum_end = cum[:, L - 1:L, :]
    r_t = r * gam
    a_t = aa * gam_prev
    k_t = k * jnp.exp(-cum)
    b_t = bb * jnp.exp(-cum)
    k_e = k * jnp.exp(cum_end - cum)
    b_e = bb * jnp.exp(cum_end - cum)

    def stacked(x):
        return jnp.concatenate([jnp.where(head0, x, 0.0), jnp.where(head0, 0.0, x)], axis=1)

    ar = jnp.concatenate([stacked(a_t), stacked(r_t)], axis=1).astype(BF16)
    kb = jnp.concatenate([stacked(k_t), stacked(b_t)], axis=1).astype(BF16)
    v_s = stacked(v)
    s_old = state[...]

    s4 = _bdot(ar, kb, 'pik,pjk->pij')
    ah = _bdot(ar, s_old.astype(BF16), 'pik,plk->pil')
    M = 2 * L
    si = lax.broadcasted_iota(jnp.int32, (1, M, M), 1)
    sj = lax.broadcasted_iota(jnp.int32, (1, M, M), 2)
    same = (si // L) == (sj // L)
    strict = same & ((sj % L) < (si % L))
    incl = same & ((sj % L) <= (si % L))
    a_k = jnp.where(strict, s4[:, :M, :M], 0.0)
    a_b = jnp.where(strict, s4[:, :M, M:], 0.0)
    r_k = jnp.where(incl, s4[:, M:, :M], 0.0)
    r_b = jnp.where(incl, s4[:, M:, M:], 0.0)

    eye = (si == sj).astype(F32)
    lower1 = (si // 2 == sj // 2) & (si % 2 == 1) & (sj % 2 == 0)
    T = eye + jnp.where(lower1, a_b, 0.0)
    ab16 = a_b.astype(BF16)
    s = 2
    while s < L:
        cm = (si // (2 * s) == sj // (2 * s)) & (si % (2 * s) >= s) & (sj % (2 * s) < s)
        tb = T.astype(BF16)
        tat = _bdot(_bdot(tb, ab16, 'pij,pjk->pik').astype(BF16), tb, 'pij,pjk->pik')
        T = T + jnp.where(cm, tat, 0.0)
        s *= 2

    x_s = ah[:, :M, :] + _bdot(a_k.astype(BF16), v_s.astype(BF16), 'pij,pjl->pil')
    u_s = _bdot(T.astype(BF16), x_s.astype(BF16), 'pij,pjl->pil')
    vu = jnp.concatenate([v_s, u_s], axis=1).astype(BF16)
    rkb = jnp.concatenate([r_k, r_b], axis=2).astype(BF16)
    y_s = ah[:, M:, :] + _bdot(rkb, vu, 'pij,pjl->pil')
    y = y_s[:, :L, :] + y_s[:, L:, :]

    kb_e = jnp.concatenate([stacked(k_e), stacked(b_e)], axis=1).astype(BF16)
    state[...] = s_old * jnp.exp(cum_end) + _bdot(vu, kb_e, 'pti,ptj->pij')

    mean = segsum(y) * (1.0 / N)
    yc = y - mean
    var = segsum(yc * yc) * (1.0 / N)
    yn = yc * lax.rsqrt(var + RW_GN_EPS) * vec(lg_ref) + vec(lb_ref)
    bonus = segsum(r * k * vec(rk_ref)) * v
    out = (yn + bonus) * pairs(g_ref)
    for p in range(P):
        o_ref[0, :, LANES * p:LANES * (p + 1)] = out[p].astype(o_ref.dtype)


def _rwkv_rec(r, k, v, ld, a, g, k_k, k_a, r_k, lnx_g, lnx_b):
    bsz, seq, d = r.shape
    L = RW_CHUNK
    row = lambda b, j: (b, j, 0)
    fix = lambda b, j: (0, 0)
    return pl.pallas_call(
        _rwkv_rec_kernel,
        grid=(bsz, seq // L),
        in_specs=[pl.BlockSpec((1, L, d), row)] * 6 + [pl.BlockSpec((1, d), fix)] * 5,
        out_specs=pl.BlockSpec((1, L, d), row),
        out_shape=jax.ShapeDtypeStruct((bsz, seq, d), BF16),
        scratch_shapes=[pltpu.VMEM((RW_PAIRS, LANES, LANES), F32)],
        compiler_params=_cparams("parallel", "arbitrary"),
        name="rwkv_rec",
    )(r, k, v, ld, a, g, k_k, k_a, r_k, lnx_g, lnx_b)


def _rwkv_sublayer(h, mu, w_r, w_k, w_v, w0, w1, w2, a0, a1, a2, g1, g2, k_k, k_a, r_k,
                   lnx_g, lnx_b, w_o, g, b):
    bsz, seq, d = h.shape
    bf = lambda w: w.astype(BF16)
    row = lambda w: w.reshape(1, d)
    mu8 = jnp.pad(mu, ((0, SUBLANES - mu.shape[0]), (0, 0)))
    r, k, v, ld, a, gg = _rwkv_in(h, mu8, bf(w_r), bf(w_k), bf(w_v), row(w0), bf(w1), bf(w2),
                                  row(a0), bf(a1), bf(a2), bf(g1), bf(g2))
    y = _rwkv_rec(r, k, v, ld, a, gg, row(k_k), row(k_a), row(r_k), row(lnx_g), row(lnx_b))
    return _proj_ln(y.reshape(bsz * seq, d), bf(w_o), h.reshape(bsz * seq, d), g, b).reshape(bsz, seq, d)


def _chunk_attn_kernel(q_ref, k_ref, v_ref, tbl_ref, o_ref):
    seq = q_ref.shape[1]
    blocks_back = (ATT_WIN - ATT_TQ) // ATT_TQ
    lane = lax.broadcasted_iota(jnp.int32, (ATT_TQ, LANES), 1)
    for qi in range(seq // ATT_TQ):
        back = min(qi, blocks_back)
        ws = (qi - back) * ATT_TQ
        width = (back + 1) * ATT_TQ
        off = (blocks_back - back) * ATT_TQ
        rows = slice(qi * ATT_TQ, (qi + 1) * ATT_TQ)
        kw = k_ref[0, ws:ws + width, :]
        vw = v_ref[0, ws:ws + width, :]
        q = q_ref[0, rows, :]
        outs = []
        for hh in range(2):
            in_head = (lane // ATT_HEAD_DIM) == hh
            qh = jnp.where(in_head, q, jnp.zeros_like(q))
            s = lax.dot_general(qh, kw, (((1,), (1,)), ((), ())), preferred_element_type=F32)
            s = s + tbl_ref[0, hh, :, off:off + width]
            m = jnp.max(s, axis=-1, keepdims=True)
            e = jnp.exp(s - m)
            l = jnp.sum(e, axis=-1, keepdims=True)
            outs.append(jnp.dot(e.astype(BF16), vw, preferred_element_type=F32) / l)
        o_ref[0, rows, :] = jnp.where(lane < ATT_HEAD_DIM, outs[0], outs[1]).astype(o_ref.dtype)


def _chunk_attn(qkv, tbl):
    bsz, seq, _ = qkv.shape
    return pl.pallas_call(
        _chunk_attn_kernel,
        grid=(bsz, ATT_PAIRS),
        in_specs=[pl.BlockSpec((1, seq, LANES), lambda b, p: (b, 0, p)),
                  pl.BlockSpec((1, seq, LANES), lambda b, p: (b, 0, ATT_PAIRS + p)),
                  pl.BlockSpec((1, seq, LANES), lambda b, p: (b, 0, 2 * ATT_PAIRS + p)),
                  pl.BlockSpec((1, 2, ATT_TQ, ATT_TBL), lambda b, p: (p, 0, 0, 0))],
        out_specs=pl.BlockSpec((1, seq, LANES), lambda b, p: (b, 0, p)),
        out_shape=jax.ShapeDtypeStruct((bsz, seq, D_MODEL), BF16),
        compiler_params=_cparams("parallel", "parallel"),
        name="chunk_attn",
    )(qkv, qkv, qkv, tbl)


def _attn_bias_table(rel_bias):
    left = ATT_WIN - ATT_TQ
    period = ATT_TBL + ATT_TQ
    j = jnp.arange(period)
    diff = jnp.where(j < ATT_TBL, j, j - period)
    vec = rel_bias[:, jnp.clip(left - diff, -MAX_REL, MAX_REL) + MAX_REL].astype(F32)
    flat = jnp.tile(vec, (1, ATT_TQ))[:, :ATT_TQ * (period - 1)]
    bias = flat.reshape(-1, ATT_TQ, period - 1)[:, :, :ATT_TBL]
    ic = jnp.arange(ATT_TQ)[:, None] // CHUNK
    mc = jnp.arange(ATT_TBL)[None, :] // CHUNK
    valid = (mc >= ic) & (mc <= ic + BAND_CHUNKS - 1)
    tbl = jnp.where(valid[None], bias, NEG_INF)
    return tbl.reshape(ATT_PAIRS, 2, ATT_TQ, ATT_TBL)


def _attn_sublayer(h, w_qkv, rel_bias, w_o, g, b):
    bsz, seq, d = h.shape
    assert seq % ATT_TQ == 0 and seq >= ATT_WIN
    scale = jnp.concatenate([jnp.full((d,), ATT_HEAD_DIM ** -0.5, F32), jnp.ones((2 * d,), F32)])
    qkv = _proj(h.reshape(bsz * seq, d), (w_qkv * scale).astype(BF16), BF16)
    o = _chunk_attn(qkv.reshape(bsz, seq, 3 * d), _attn_bias_table(rel_bias))
    return _proj_ln(o.reshape(bsz * seq, d), w_o.astype(BF16), h.reshape(bsz * seq, d), g, b).reshape(bsz, seq, d)


def _mem_attn_kernel(h_ref, kv_ref, wq_ref, wo_ref, g_ref, b_ref, o_ref, *, sub):
    for r0 in range(0, h_ref.shape[1], sub):
        rows = slice(r0, r0 + sub)
        h = h_ref[0, rows, :]
        q = jnp.dot(h.astype(BF16), wq_ref[...], preferred_element_type=F32).astype(BF16)
        outs = []
        for hh in range(MEM_HEADS):
            lo = hh * MEM_HEAD_DIM
            kh = kv_ref[0, :, lo:lo + MEM_HEAD_DIM]
            vh = kv_ref[0, :, D_MODEL + lo:D_MODEL + lo + MEM_HEAD_DIM]
            s = lax.dot_general(q[:, lo:lo + MEM_HEAD_DIM], kh, (((1,), (1,)), ((), ())),
                                preferred_element_type=F32)
            m = jnp.max(s, axis=-1, keepdims=True)
            e = jnp.exp(s - m)
            l = jnp.sum(e, axis=-1, keepdims=True)
            outs.append((jnp.dot(e.astype(BF16), vh, preferred_element_type=F32) / l).astype(BF16))
        o = jnp.concatenate(outs, axis=1)
        y = jnp.dot(o, wo_ref[...], preferred_element_type=F32)
        o_ref[0, rows, :] = _residual_ln(h, y, g_ref[...], b_ref[...])


def _mem_sublayer(h, mem, w_q, w_kv, w_o, g, b, tm=1024, sub=256):
    bsz, seq, d = h.shape
    nm = mem.shape[1]
    tm = min(tm, seq)
    kv = _proj(mem.reshape(bsz * nm, d), w_kv.astype(BF16), BF16).reshape(bsz, nm, 2 * d)
    wq = (w_q * (MEM_HEAD_DIM ** -0.5)).astype(BF16)
    row = lambda b_, i: (b_, i, 0)
    fix = lambda b_, i: (0, 0)
    return pl.pallas_call(
        functools.partial(_mem_attn_kernel, sub=min(sub, tm)),
        grid=(bsz, seq // tm),
        in_specs=[pl.BlockSpec((1, tm, d), row),
                  pl.BlockSpec((1, nm, 2 * d), lambda b_, i: (b_, 0, 0)),
                  pl.BlockSpec((d, d), fix), pl.BlockSpec((d, d), fix),
                  pl.BlockSpec((1, d), fix), pl.BlockSpec((1, d), fix)],
        out_specs=pl.BlockSpec((1, tm, d), row),
        out_shape=jax.ShapeDtypeStruct((bsz, seq, d), F32),
        compiler_params=_cparams("parallel", "parallel"),
        name="mem_attn",
    )(h, kv, wq, w_o.astype(BF16), g, b)


def _mlp_kernel(h_ref, w1_ref, w2_ref, g_ref, b_ref, o_ref, *, sub):
    for r0 in range(0, h_ref.shape[0], sub):
        rows = slice(r0, r0 + sub)
        h = h_ref[rows, :]
        u = jnp.dot(h.astype(BF16), w1_ref[...], preferred_element_type=F32)
        u = jnp.maximum(u, 0.0)
        y = jnp.dot((u * u).astype(BF16), w2_ref[...], preferred_element_type=F32)
        o_ref[rows, :] = _residual_ln(h, y, g_ref[...], b_ref[...])


def _mlp_sublayer(h, w1, w2, g, b, tm=1024, sub=256):
    bsz, seq, d = h.shape
    t = bsz * seq
    tm = min(tm, t)
    ff = w1.shape[1]
    resident = pl.Buffered(1)
    out = pl.pallas_call(
        functools.partial(_mlp_kernel, sub=min(sub, tm)),
        grid=(t // tm,),
        in_specs=[pl.BlockSpec((tm, d), lambda i: (i, 0)),
                  pl.BlockSpec((d, ff), lambda i: (0, 0), pipeline_mode=resident),
                  pl.BlockSpec((ff, d), lambda i: (0, 0), pipeline_mode=resident),
                  pl.BlockSpec((1, d), lambda i: (0, 0)),
                  pl.BlockSpec((1, d), lambda i: (0, 0))],
        out_specs=pl.BlockSpec((tm, d), lambda i: (i, 0)),
        out_shape=jax.ShapeDtypeStruct((t, d), F32),
        compiler_params=_cparams("parallel"),
        name="mlp",
    )(h.reshape(t, d), w1.astype(BF16), w2.astype(BF16), g, b)
    return out.reshape(bsz, seq, d)


def kernel(x, mem, ln_g, ln_b, lru_w_in, lru_conv_w, lru_conv_b, lru_gate_w, lru_gate_b, lru_lambda, lru_w_out, rw_mu, rw_w_r, rw_w_k, rw_w_v, rw_w0, rw_w1, rw_w2, rw_a0, rw_a1, rw_a2, rw_g1, rw_g2, rw_k_k, rw_k_a, rw_r_k, rw_lnx_g, rw_lnx_b, rw_w_o, ca_w_qkv, ca_rel_bias, ca_w_o, mx_w_q, mx_w_kv, mx_w_o, mlp_w1, mlp_w2):
    h = x
    for i in range(DEPTH):
        kind, j = i % N_MIXERS, i // N_MIXERS
        g = lambda s: ln_g[i, s][None, :]
        b = lambda s: ln_b[i, s][None, :]
        if kind == 0:
            h = _rglru_sublayer(h, lru_w_in[j], lru_conv_w[j], lru_conv_b[j], lru_gate_w[j], lru_gate_b[j],
                                lru_lambda[j], lru_w_out[j], g(0), b(0))
        elif kind == 1:
            h = _rwkv_sublayer(h, rw_mu[j], rw_w_r[j], rw_w_k[j], rw_w_v[j], rw_w0[j], rw_w1[j], rw_w2[j],
                               rw_a0[j], rw_a1[j], rw_a2[j], rw_g1[j], rw_g2[j], rw_k_k[j], rw_k_a[j],
                               rw_r_k[j], rw_lnx_g[j], rw_lnx_b[j], rw_w_o[j], g(0), b(0))
        else:
            h = _attn_sublayer(h, ca_w_qkv[j], ca_rel_bias[j], ca_w_o[j], g(0), b(0))
        h = _mem_sublayer(h, mem, mx_w_q[i], mx_w_kv[i], mx_w_o[i], g(1), b(1))
        h = _mlp_sublayer(h, mlp_w1[i], mlp_w2[i], g(2), b(2))
    return h
```

```python
import functools

import jax
import jax.numpy as jnp
from jax import lax
from jax.experimental import pallas as pl
from jax.experimental.pallas import tpu as pltpu

F32 = jnp.float32
BF16 = jnp.bfloat16

D_MODEL = 1024
DEPTH = 4
CHUNK = 64
N_MIXERS = 3
DEEPNORM_ALPHA = (2 * DEPTH) ** 0.25
LN_EPS = 1e-5

D_RNN = 1344
LRU_BLOCKS = 16
LRU_BLOCK_SIZE = D_RNN // LRU_BLOCKS
CONV_WIDTH = 4
RG_LRU_C = 8.0
LANES = 128
SUBLANES = 8
D_RNN_PAD = -(-D_RNN // LANES) * LANES

RW_HEAD_SIZE = 64
RW_HEADS = D_MODEL // RW_HEAD_SIZE
RW_PAIRS = D_MODEL // LANES
RW_GN_EPS = 64e-5
RW_CHUNK = 64

ATT_HEADS = 16
ATT_HEAD_DIM = D_MODEL // ATT_HEADS
ATT_PAIRS = D_MODEL // LANES
BAND_CHUNKS = 9
MAX_REL = 2 * CHUNK
NEG_INF = -1e30
ATT_TQ = 256
ATT_WIN = ATT_TQ + (BAND_CHUNKS - 1) * CHUNK
ATT_TBL = ATT_WIN + (BAND_CHUNKS - 1) * CHUNK

MEM_HEADS = 4
MEM_HEAD_DIM = D_MODEL // MEM_HEADS
D_FF = 4 * D_MODEL

VMEM_LIMIT = 56 * 1024 * 1024


def _cparams(*sem):
    return pltpu.CompilerParams(dimension_semantics=sem, vmem_limit_bytes=VMEM_LIMIT)


def _residual_ln(hres, y, g, b):
    z = DEEPNORM_ALPHA * hres + y
    mu = jnp.mean(z, axis=-1, keepdims=True)
    zc = z - mu
    var = jnp.mean(zc * zc, axis=-1, keepdims=True)
    return zc * lax.rsqrt(var + LN_EPS) * g + b


def _gelu_tanh(x):
    return 0.5 * x * (1.0 + jnp.tanh(0.7978845608028654 * (x + 0.044715 * (x * x * x))))


def _softplus(x):
    return jnp.maximum(x, 0.0) + jnp.log1p(jnp.exp(-jnp.abs(x)))


def _sigmoid(x):
    return 0.5 * jnp.tanh(0.5 * x) + 0.5


def _split3(x):
    hi = x.astype(BF16)
    r1 = x - hi.astype(F32)
    mid = r1.astype(BF16)
    lo = (r1 - mid.astype(F32)).astype(BF16)
    return hi, mid, lo


def _proj_kernel(x_ref, w_ref, o_ref, *, chunk):
    x = x_ref[...].astype(BF16)
    n = w_ref.shape[1]
    for n0 in range(0, n, chunk):
        y = jnp.dot(x, w_ref[:, n0:n0 + chunk], preferred_element_type=F32)
        o_ref[:, n0:n0 + chunk] = y.astype(o_ref.dtype)


def _proj(x2d, w, out_dtype, tm=512, chunk=512):
    t, k = x2d.shape
    n = w.shape[1]
    tm = min(tm, t)
    return pl.pallas_call(
        functools.partial(_proj_kernel, chunk=chunk),
        grid=(t // tm,),
        in_specs=[pl.BlockSpec((tm, k), lambda i: (i, 0)),
                  pl.BlockSpec((k, n), lambda i: (0, 0))],
        out_specs=pl.BlockSpec((tm, n), lambda i: (i, 0)),
        out_shape=jax.ShapeDtypeStruct((t, n), out_dtype),
        compiler_params=_cparams("parallel"),
        name="proj",
    )(x2d, w)


def _proj_ln_kernel(y_ref, w_ref, h_ref, g_ref, b_ref, o_ref, *, sub):
    for r0 in range(0, y_ref.shape[0], sub):
        rows = slice(r0, r0 + sub)
        y = jnp.dot(y_ref[rows, :].astype(BF16), w_ref[...], preferred_element_type=F32)
        o_ref[rows, :] = _residual_ln(h_ref[rows, :], y, g_ref[...], b_ref[...])


def _proj_ln(y2d, w, h2d, g, b, tm=1024, sub=256):
    t, k = y2d.shape
    d = w.shape[1]
    tm = min(tm, t)
    return pl.pallas_call(
        functools.partial(_proj_ln_kernel, sub=min(sub, tm)),
        grid=(t // tm,),
        in_specs=[pl.BlockSpec((tm, k), lambda i: (i, 0)),
                  pl.BlockSpec((k, d), lambda i: (0, 0)),
                  pl.BlockSpec((tm, d), lambda i: (i, 0)),
                  pl.BlockSpec((1, d), lambda i: (0, 0)),
                  pl.BlockSpec((1, d), lambda i: (0, 0))],
        out_specs=pl.BlockSpec((tm, d), lambda i: (i, 0)),
        out_shape=jax.ShapeDtypeStruct((t, d), F32),
        compiler_params=_cparams("parallel"),
        name="proj_ln",
    )(y2d, w, h2d, g, b)


LRU_GATE_TILE = 256
LRU_GATE_WIN = 512


def _lru_gate_window_starts():
    starts = []
    for n0 in range(0, D_RNN_PAD, LRU_GATE_TILE):
        first_block = min(n0, D_RNN - 1) // LRU_BLOCK_SIZE
        last_block = min(n0 + LRU_GATE_TILE - 1, D_RNN - 1) // LRU_BLOCK_SIZE
        lo = (first_block * LRU_BLOCK_SIZE) // LANES * LANES
        lo = min(lo, D_RNN_PAD - LRU_GATE_WIN)
        assert (last_block + 1) * LRU_BLOCK_SIZE <= lo + LRU_GATE_WIN
        starts.append(lo)
    return starts


def _lru_kernel(h_ref, win_ref, cw_ref, cb_ref, gw_ref, gb_ref, lam_ref, wout_ref, g_ref, b_ref, o_ref,
                xbuf, hcarry, ascan, bscan):
    @pl.when(pl.program_id(1) == 0)
    def _():
        xbuf[:, 0:SUBLANES, :] = jnp.zeros((xbuf.shape[0], SUBLANES, xbuf.shape[2]), F32)
        hcarry[...] = jnp.zeros(hcarry.shape, F32)

    nb = h_ref.shape[0]
    fronts = [_lru_front(bb, h_ref, win_ref, cw_ref, cb_ref, gw_ref, xbuf) for bb in range(nb)]
    for bb in range(nb):
        _lru_back(bb, *fronts[bb], gb_ref, lam_ref, wout_ref, g_ref, b_ref, o_ref, hcarry, ascan, bscan)


def _lru_front(bb, h_ref, win_ref, cw_ref, cb_ref, gw_ref, xbuf):
    ts = h_ref.shape[1]
    c = cw_ref.shape[1]
    hres = h_ref[bb]
    u = jnp.dot(hres.astype(BF16), win_ref[...], preferred_element_type=F32)
    gate_branch = _gelu_tanh(u[:, :c])
    x = u[:, c:]
    xbuf[bb, SUBLANES:SUBLANES + ts, :] = x
    xr = cb_ref[...] + cw_ref[3:4, :] * x
    for k in range(CONV_WIDTH - 1):
        off = SUBLANES - (CONV_WIDTH - 1) + k
        xr = xr + cw_ref[k:k + 1, :] * xbuf[bb, off:off + ts, :]
    xbuf[bb, 0:SUBLANES, :] = xbuf[bb, ts:ts + SUBLANES, :]

    xb = xr.astype(BF16)
    tiles = [jnp.dot(xb[:, lo:lo + LRU_GATE_WIN], gw_ref[n], preferred_element_type=F32)
             for n, lo in enumerate(_lru_gate_window_starts())]
    r_pre = jnp.concatenate([t[:, :LRU_GATE_TILE] for t in tiles], axis=1)[:, :c]
    i_pre = jnp.concatenate([t[:, LRU_GATE_TILE:] for t in tiles], axis=1)[:, :c]
    return hres, gate_branch, xr, r_pre, i_pre


def _lru_back(bb, hres, gate_branch, xr, r_pre, i_pre, gb_ref, lam_ref, wout_ref, g_ref, b_ref, o_ref,
              hcarry, ascan, bscan):
    ts, c = xr.shape
    r_gate = _sigmoid(r_pre + gb_ref[0:1, :])
    i_gate = _sigmoid(i_pre + gb_ref[1:2, :])
    log_a = (-RG_LRU_C) * r_gate * _softplus(-lam_ref[...])
    a = jnp.exp(log_a)
    bt = jnp.sqrt(-jnp.tanh(log_a) * (a * a + 1.0)) * (i_gate * xr)

    def doubling_scan(av, bv, axis):
        n = av.shape[axis]
        idx = lax.broadcasted_iota(jnp.int32, av.shape, axis)
        d = 1
        while d < n:
            keep = idx >= d
            a_sh = jnp.where(keep, pltpu.roll(av, d, axis), 1.0)
            b_sh = jnp.where(keep, pltpu.roll(bv, d, axis), 0.0)
            bv = av * b_sh + bv
            av = av * a_sh
            d *= 2
        return av, bv

    groups = ts // SUBLANES
    a3, b3 = doubling_scan(a.reshape(groups, SUBLANES, c), bt.reshape(groups, SUBLANES, c), 1)
    a2, b2 = a3.reshape(ts, c), b3.reshape(ts, c)
    last = pl.ds(SUBLANES - 1, groups, stride=SUBLANES)
    slabs = c // LANES
    for s in range(slabs):
        ascan[bb, s] = a2[:, s * LANES:(s + 1) * LANES]
        bscan[bb, s] = b2[:, s * LANES:(s + 1) * LANES]
    ag = jnp.concatenate([ascan[bb, s, last, :] for s in range(slabs)], axis=1)
    bg = jnp.concatenate([bscan[bb, s, last, :] for s in range(slabs)], axis=1)
    ag, bg = doubling_scan(ag, bg, 0)
    carry = hcarry[bb, 0:1, :]
    h_end = ag * carry + bg
    grow = lax.broadcasted_iota(jnp.int32, h_end.shape, 0)
    h_in = jnp.where(grow == 0, carry, pltpu.roll(h_end, 1, 0))
    hcarry[bb, 0:1, :] = h_end[groups - 1:groups, :]
    h = (a3 * h_in[:, None, :] + b3).reshape(ts, c)

    y = jnp.dot((h * gate_branch).astype(BF16), wout_ref[...], preferred_element_type=F32)
    o_ref[bb] = _residual_ln(hres, y, g_ref[...], b_ref[...])


def _lru(h, win, cw, cb, gw, gb, lam, wout, g, b, ts=128, nb=2):
    bsz, seq, d = h.shape
    c = cw.shape[1]
    ts = min(ts, seq)
    nb = min(nb, bsz)
    row = lambda i, j: (i, j, 0)
    resident = lambda w: pl.BlockSpec(w.shape, lambda i, j: (0,) * w.ndim, pipeline_mode=pl.Buffered(1))
    return pl.pallas_call(
        _lru_kernel,
        grid=(bsz // nb, seq // ts),
        in_specs=[pl.BlockSpec((nb, ts, d), row)] + [resident(w) for w in (win, cw, cb, gw, gb, lam, wout, g, b)],
        out_specs=pl.BlockSpec((nb, ts, d), row),
        out_shape=jax.ShapeDtypeStruct((bsz, seq, d), F32),
        scratch_shapes=[pltpu.VMEM((nb, ts + SUBLANES, c), F32), pltpu.VMEM((nb, SUBLANES, c), F32),
                        pltpu.VMEM((nb, c // LANES, ts, LANES), F32),
                        pltpu.VMEM((nb, c // LANES, ts, LANES), F32)],
        compiler_params=_cparams("parallel", "arbitrary"),
        name="lru",
    )(h, win, cw, cb, gw, gb, lam, wout, g, b)


def _pad_last(x, n):
    return jnp.pad(x, [(0, 0)] * (x.ndim - 1) + [(0, n - x.shape[-1])])


def _rglru_sublayer(h, w_in, conv_w, conv_b, gate_w, gate_b, lam, w_out, g, b):
    bsz, seq, d = h.shape
    c = D_RNN_PAD
    w_cat = jnp.concatenate([_pad_last(w_in[:, :D_RNN], c), _pad_last(w_in[:, D_RNN:], c)], axis=1).astype(BF16)
    eye = jnp.eye(LRU_BLOCKS, dtype=F32)
    dense = jnp.einsum('gncd,nm->gncmd', gate_w, eye).reshape(2, D_RNN, D_RNN)
    starts = _lru_gate_window_starts()
    dense = jnp.pad(dense, ((0, 0), (0, c - D_RNN), (0, len(starts) * LRU_GATE_TILE - D_RNN)))
    gw = jnp.stack([jnp.concatenate([dense[gi, lo:lo + LRU_GATE_WIN, n * LRU_GATE_TILE:(n + 1) * LRU_GATE_TILE]
                                     for gi in range(2)], axis=1) for n, lo in enumerate(starts)]).astype(BF16)
    wout = jnp.pad(w_out, ((0, c - D_RNN), (0, 0))).astype(BF16)
    return _lru(h, w_cat, _pad_last(conv_w, c), _pad_last(conv_b[None, :], c), gw,
                _pad_last(gate_b, c), _pad_last(lam[None, :], c), wout, g, b)


def _rwkv_in_kernel(x_ref, xp_ref, mu_ref, wr_ref, wk_ref, wv_ref, w0_ref, w1_ref, w2_ref,
                    a0_ref, a1_ref, a2_ref, g1_ref, g2_ref,
                    r_ref, k_ref, v_ref, ld_ref, a_ref, g_ref, *, sub):
    i = pl.program_id(1)
    x_all = x_ref[0]
    row = lax.broadcasted_iota(jnp.int32, x_all.shape, 0)
    prev = jnp.where(i > 0, xp_ref[0, SUBLANES - 1:SUBLANES, :], 0.0)
    xx_all = jnp.where(row == 0, prev, pltpu.roll(x_all, 1, 0)) - x_all
    dot = lambda p, q: jnp.dot(p, q, preferred_element_type=F32)
    for r0 in range(0, x_all.shape[0], sub):
        rows = slice(r0, r0 + sub)
        x, xx = x_all[rows, :], xx_all[rows, :]
        mix = lambda c: (x + xx * mu_ref[c:c + 1, :]).astype(BF16)
        r_ref[0, rows, :] = dot(mix(0), wr_ref[...]).astype(r_ref.dtype)
        k_ref[0, rows, :] = dot(mix(2), wk_ref[...]).astype(k_ref.dtype)
        v_ref[0, rows, :] = dot(mix(3), wv_ref[...]).astype(v_ref.dtype)
        wl = w0_ref[...] + dot(jnp.tanh(dot(mix(1), w1_ref[...])).astype(BF16), w2_ref[...])
        w_log = -_softplus(-wl) - 0.5
        ld_ref[0, rows, :] = -jnp.exp(w_log)
        a_ref[0, rows, :] = _sigmoid(a0_ref[...] + dot(dot(mix(4), a1_ref[...]).astype(BF16), a2_ref[...]))
        g_ref[0, rows, :] = dot(_sigmoid(dot(mix(5), g1_ref[...])).astype(BF16), g2_ref[...]).astype(g_ref.dtype)


def _rwkv_in(h, mu, wr, wk, wv, w0, w1, w2, a0, a1, a2, g1, g2, tm=512, sub=256):
    bsz, seq, d = h.shape
    tm = min(tm, seq)
    row = lambda b, i: (b, i, 0)
    fix = lambda b, i: (0, 0)
    prev = lambda b, i: (b, jnp.maximum(i * (tm // SUBLANES) - 1, 0), 0)
    ws = [mu, wr, wk, wv, w0, w1, w2, a0, a1, a2, g1, g2]
    return pl.pallas_call(
        functools.partial(_rwkv_in_kernel, sub=min(sub, tm)),
        grid=(bsz, seq // tm),
        in_specs=[pl.BlockSpec((1, tm, d), row), pl.BlockSpec((1, SUBLANES, d), prev)]
                 + [pl.BlockSpec(w.shape, fix) for w in ws],
        out_specs=[pl.BlockSpec((1, tm, d), row)] * 6,
        out_shape=[jax.ShapeDtypeStruct((bsz, seq, d), dt) for dt in (BF16, BF16, BF16, F32, F32, BF16)],
        compiler_params=_cparams("parallel", "parallel"),
        name="rwkv_in",
    )(h, h, *ws)


def _bdot(a, b, dims):
    return jnp.einsum(dims, a, b, preferred_element_type=F32)


def _rwkv_rec_kernel(r_ref, k_ref, v_ref, ld_ref, a_ref, g_ref, kk_ref, ka_ref, rk_ref, lg_ref, lb_ref,
                     o_ref, state):
    @pl.when(pl.program_id(1) == 0)
    def _():
        state[...] = jnp.zeros(state.shape, F32)

    s_val = state[...]
    for c in range(r_ref.shape[1] // RW_CHUNK):
        s_val = _rwkv_chunk(slice(c * RW_CHUNK, (c + 1) * RW_CHUNK), s_val, r_ref, k_ref, v_ref, ld_ref, a_ref,
                            g_ref, kk_ref, ka_ref, rk_ref, lg_ref, lb_ref, o_ref)
    state[...] = s_val


def _rwkv_chunk(rows, s_old, r_ref, k_ref, v_ref, ld_ref, a_ref, g_ref, kk_ref, ka_ref, rk_ref, lg_ref, lb_ref,
                o_ref):
    L = RW_CHUNK
    P = RW_PAIRS
    N = RW_HEAD_SIZE
    pairs = lambda ref: jnp.stack([ref[0, rows, LANES * p:LANES * (p + 1)] for p in range(P)]).astype(F32)
    vec = lambda ref: jnp.stack([ref[:, LANES * p:LANES * (p + 1)] for p in range(P)])

    lane = lax.broadcasted_iota(jnp.int32, (1, 1, LANES), 2)
    head0 = lane < N
    ri = lax.broadcasted_iota(jnp.int32, (LANES, LANES), 0)
    ci = lax.broadcasted_iota(jnp.int32, (LANES, LANES), 1)
    seg = ((ri // N) == (ci // N)).astype(BF16)

    def segsum(x):
        flat = x.reshape(P * L, LANES)
        hi = flat.astype(BF16)
        lo = (flat - hi.astype(F32)).astype(BF16)
        out = jnp.dot(hi, seg, preferred_element_type=F32) + jnp.dot(lo, seg, preferred_element_type=F32)
        return out.reshape(P, L, LANES)

    r = pairs(r_ref)
    k = pairs(k_ref)
    v = pairs(v_ref)
    a = pairs(a_ref)

    kk = k * vec(kk_ref)
    kk = kk / jnp.maximum(jnp.sqrt(segsum(kk * kk)), 1e-12)
    k = k * (1.0 + (a - 1.0) * vec(ka_ref))
    aa = -kk
    bb = kk * a

    ld_full = ld_ref[0, rows, :]
    tr = lax.broadcasted_iota(jnp.int32, (L, L), 0)
    tc = lax.broadcasted_iota(jnp.int32, (L, L), 1)
    tri = (tc <= tr).astype(BF16)
    cum_full = sum(jnp.dot(tri, t, preferred_element_type=F32) for t in _split3(ld_full))
    cum = jnp.stack([cum_full[:, LANES * p:LANES * (p + 1)] for p in range(P)])
    ld = pairs(ld_ref)
    gam = jnp.exp(cum)
    gam_prev = jnp.exp(cum - ld)
    cum_end = cum[:, L - 1:L, :]
    r_t = r * gam
    a_t = aa * gam_prev
    k_t = k * jnp.exp(-cum)
    b_t = bb * jnp.exp(-cum)
    k_e = k * jnp.exp(cum_end - cum)
    b_e = bb * jnp.exp(cum_end - cum)

    def stacked(x):
        return jnp.concatenate([jnp.where(head0, x, 0.0), jnp.where(head0, 0.0, x)], axis=1)

    ar = jnp.concatenate([stacked(a_t), stacked(r_t)], axis=1).astype(BF16)
    kb = jnp.concatenate([stacked(k_t), stacked(b_t)], axis=1).astype(BF16)
    v_s = stacked(v)

    s4 = _bdot(ar, kb, 'pik,pjk->pij')
    ah = _bdot(ar, s_old.astype(BF16), 'pik,plk->pil')
    M = 2 * L
    si = lax.broadcasted_iota(jnp.int32, (1, M, M), 1)
    sj = lax.broadcasted_iota(jnp.int32, (1, M, M), 2)
    same = (si // L) == (sj // L)
    strict = same & ((sj % L) < (si % L))
    incl = same & ((sj % L) <= (si % L))
    a_k = jnp.where(strict, s4[:, :M, :M], 0.0)
    a_b = jnp.where(strict, s4[:, :M, M:], 0.0)
    r_k = jnp.where(incl, s4[:, M:, :M], 0.0)
    r_b = jnp.where(incl, s4[:, M:, M:], 0.0)

    eye = (si == sj).astype(F32)
    lower1 = (si // 2 == sj // 2) & (si % 2 == 1) & (sj % 2 == 0)
    T = eye + jnp.where(lower1, a_b, 0.0)
    ab16 = a_b.astype(BF16)
    s = 2
    hi_ = lax.broadcasted_iota(jnp.int32, (1, M // 2, M), 1)
    hj_ = lax.broadcasted_iota(jnp.int32, (1, M // 2, M), 2)
    while s < L:
        tb = T.astype(BF16)
        if s < SUBLANES:
            cm = (si // (2 * s) == sj // (2 * s)) & (si % (2 * s) >= s) & (sj % (2 * s) < s)
            tat = _bdot(_bdot(tb, ab16, 'pij,pjk->pik').astype(BF16), tb, 'pij,pjk->pik')
            T = T + jnp.where(cm, tat, 0.0)
        else:
            nblk = M // (2 * s)
            low = jnp.concatenate([T[:, (2 * k + 1) * s:(2 * k + 2) * s, :] for k in range(nblk)], axis=1)
            upd = _bdot(_bdot(low.astype(BF16), ab16, 'pij,pjk->pik').astype(BF16), tb, 'pij,pjk->pik')
            upd = jnp.where((hj_ // (2 * s) == hi_ // s) & (hj_ % (2 * s) < s), upd, 0.0)
            zero = jnp.zeros((P, s, M), F32)
            T = T + jnp.concatenate([blk for k in range(nblk) for blk in (zero, upd[:, k * s:(k + 1) * s, :])],
                                    axis=1)
        s *= 2

    x_s = ah[:, :M, :] + _bdot(a_k.astype(BF16), v_s.astype(BF16), 'pij,pjl->pil')
    u_s = _bdot(T.astype(BF16), x_s.astype(BF16), 'pij,pjl->pil')
    vu = jnp.concatenate([v_s, u_s], axis=1).astype(BF16)
    rkb = jnp.concatenate([r_k, r_b], axis=2).astype(BF16)
    y_s = ah[:, M:, :] + _bdot(rkb, vu, 'pij,pjl->pil')
    y = y_s[:, :L, :] + y_s[:, L:, :]

    kb_e = jnp.concatenate([stacked(k_e), stacked(b_e)], axis=1).astype(BF16)
    s_new = s_old * jnp.exp(cum_end) + _bdot(vu, kb_e, 'pti,ptj->pij')

    mean = segsum(y) * (1.0 / N)
    yc = y - mean
    var = segsum(yc * yc) * (1.0 / N)
    yn = yc * lax.rsqrt(var + RW_GN_EPS) * vec(lg_ref) + vec(lb_ref)
    bonus = segsum(r * k * vec(rk_ref)) * v
    out = (yn + bonus) * pairs(g_ref)
    for p in range(P):
        o_ref[0, rows, LANES * p:LANES * (p + 1)] = out[p].astype(o_ref.dtype)
    return s_new


def _rwkv_rec(r, k, v, ld, a, g, k_k, k_a, r_k, lnx_g, lnx_b, chunks_per_step=2):
    bsz, seq, d = r.shape
    L = RW_CHUNK * min(chunks_per_step, seq // RW_CHUNK)
    row = lambda b, j: (b, j, 0)
    fix = lambda b, j: (0, 0)
    return pl.pallas_call(
        _rwkv_rec_kernel,
        grid=(bsz, seq // L),
        in_specs=[pl.BlockSpec((1, L, d), row)] * 6 + [pl.BlockSpec((1, d), fix)] * 5,
        out_specs=pl.BlockSpec((1, L, d), row),
        out_shape=jax.ShapeDtypeStruct((bsz, seq, d), BF16),
        scratch_shapes=[pltpu.VMEM((RW_PAIRS, LANES, LANES), F32)],
        compiler_params=_cparams("parallel", "arbitrary"),
        name="rwkv_rec",
    )(r, k, v, ld, a, g, k_k, k_a, r_k, lnx_g, lnx_b)


def _rwkv_sublayer(h, mu, w_r, w_k, w_v, w0, w1, w2, a0, a1, a2, g1, g2, k_k, k_a, r_k,
                   lnx_g, lnx_b, w_o, g, b):
    bsz, seq, d = h.shape
    bf = lambda w: w.astype(BF16)
    row = lambda w: w.reshape(1, d)
    mu8 = jnp.pad(mu, ((0, SUBLANES - mu.shape[0]), (0, 0)))
    r, k, v, ld, a, gg = _rwkv_in(h, mu8, bf(w_r), bf(w_k), bf(w_v), row(w0), bf(w1), bf(w2),
                                  row(a0), bf(a1), bf(a2), bf(g1), bf(g2))
    y = _rwkv_rec(r, k, v, ld, a, gg, row(k_k), row(k_a), row(r_k), row(lnx_g), row(lnx_b))
    return _proj_ln(y.reshape(bsz * seq, d), bf(w_o), h.reshape(bsz * seq, d), g, b).reshape(bsz, seq, d)


def _chunk_attn_kernel(q_ref, k_ref, v_ref, tbl_ref, o_ref):
    seq = q_ref.shape[1]
    blocks_back = (ATT_WIN - ATT_TQ) // ATT_TQ
    lane = lax.broadcasted_iota(jnp.int32, (ATT_TQ, LANES), 1)
    nq = seq // ATT_TQ

    def window(qi):
        back = min(qi, blocks_back)
        ws = (qi - back) * ATT_TQ
        width = (back + 1) * ATT_TQ
        off = (blocks_back - back) * ATT_TQ
        return ws, width, off

    def scores(qi):
        ws, width, _ = window(qi)
        kw = k_ref[0, ws:ws + width, :]
        q = q_ref[0, qi * ATT_TQ:(qi + 1) * ATT_TQ, :]
        out = []
        for hh in range(2):
            qh = jnp.where((lane // ATT_HEAD_DIM) == hh, q, jnp.zeros_like(q))
            out.append(lax.dot_general(qh, kw, (((1,), (1,)), ((), ())), preferred_element_type=F32))
        return out

    s_next = scores(0)
    for qi in range(nq):
        s_cur = s_next
        if qi + 1 < nq:
            s_next = scores(qi + 1)
        ws, width, off = window(qi)
        vw = v_ref[0, ws:ws + width, :]
        outs = []
        for hh in range(2):
            s = s_cur[hh] + tbl_ref[0, hh, :, off:off + width]
            m = jnp.max(s, axis=-1, keepdims=True)
            e = jnp.exp(s - m)
            l = jnp.sum(e, axis=-1, keepdims=True)
            outs.append(jnp.dot(e.astype(BF16), vw, preferred_element_type=F32) / l)
        o_ref[0, qi * ATT_TQ:(qi + 1) * ATT_TQ, :] = (
            jnp.where(lane < ATT_HEAD_DIM, outs[0], outs[1]).astype(o_ref.dtype))


def _chunk_attn(qkv, tbl):
    bsz, seq, _ = qkv.shape
    return pl.pallas_call(
        _chunk_attn_kernel,
        grid=(bsz, ATT_PAIRS),
        in_specs=[pl.BlockSpec((1, seq, LANES), lambda b, p: (b, 0, p)),
                  pl.BlockSpec((1, seq, LANES), lambda b, p: (b, 0, ATT_PAIRS + p)),
                  pl.BlockSpec((1, seq, LANES), lambda b, p: (b, 0, 2 * ATT_PAIRS + p)),
                  pl.BlockSpec((1, 2, ATT_TQ, ATT_TBL), lambda b, p: (p, 0, 0, 0))],
        out_specs=pl.BlockSpec((1, seq, LANES), lambda b, p: (b, 0, p)),
        out_shape=jax.ShapeDtypeStruct((bsz, seq, D_MODEL), BF16),
        compiler_params=_cparams("parallel", "parallel"),
        name="chunk_attn",
    )(qkv, qkv, qkv, tbl)


def _attn_bias_table(rel_bias):
    left = ATT_WIN - ATT_TQ
    period = ATT_TBL + ATT_TQ
    j = jnp.arange(period)
    diff = jnp.where(j < ATT_TBL, j, j - period)
    vec = rel_bias[:, jnp.clip(left - diff, -MAX_REL, MAX_REL) + MAX_REL].astype(F32)
    flat = jnp.tile(vec, (1, ATT_TQ))[:, :ATT_TQ * (period - 1)]
    bias = flat.reshape(-1, ATT_TQ, period - 1)[:, :, :ATT_TBL]
    ic = jnp.arange(ATT_TQ)[:, None] // CHUNK
    mc = jnp.arange(ATT_TBL)[None, :] // CHUNK
    valid = (mc >= ic) & (mc <= ic + BAND_CHUNKS - 1)
    tbl = jnp.where(valid[None], bias, NEG_INF)
    return tbl.reshape(ATT_PAIRS, 2, ATT_TQ, ATT_TBL)


def _attn_sublayer(h, w_qkv, rel_bias, w_o, g, b):
    bsz, seq, d = h.shape
    assert seq % ATT_TQ == 0 and seq >= ATT_WIN
    scale = jnp.concatenate([jnp.full((d,), ATT_HEAD_DIM ** -0.5, F32), jnp.ones((2 * d,), F32)])
    qkv = _proj(h.reshape(bsz * seq, d), (w_qkv * scale).astype(BF16), BF16)
    o = _chunk_attn(qkv.reshape(bsz, seq, 3 * d), _attn_bias_table(rel_bias))
    return _proj_ln(o.reshape(bsz * seq, d), w_o.astype(BF16), h.reshape(bsz * seq, d), g, b).reshape(bsz, seq, d)


def _mem_attn_kernel(h_ref, kv_ref, wq_ref, wo_ref, g_ref, b_ref, o_ref, *, sub):
    tiles = [slice(r0, r0 + sub) for r0 in range(0, h_ref.shape[1], sub)]

    def front(rows):
        q = jnp.dot(h_ref[0, rows, :].astype(BF16), wq_ref[...], preferred_element_type=F32).astype(BF16)
        return [lax.dot_general(q[:, hh * MEM_HEAD_DIM:(hh + 1) * MEM_HEAD_DIM],
                                kv_ref[0, :, hh * MEM_HEAD_DIM:(hh + 1) * MEM_HEAD_DIM],
                                (((1,), (1,)), ((), ())), preferred_element_type=F32) for hh in range(MEM_HEADS)]

    def back(rows, scores):
        outs = []
        for hh, s in enumerate(scores):
            vh = kv_ref[0, :, D_MODEL + hh * MEM_HEAD_DIM:D_MODEL + (hh + 1) * MEM_HEAD_DIM]
            m = jnp.max(s, axis=-1, keepdims=True)
            e = jnp.exp(s - m)
            l = jnp.sum(e, axis=-1, keepdims=True)
            outs.append((jnp.dot(e.astype(BF16), vh, preferred_element_type=F32) / l).astype(BF16))
        y = jnp.dot(jnp.concatenate(outs, axis=1), wo_ref[...], preferred_element_type=F32)
        o_ref[0, rows, :] = _residual_ln(h_ref[0, rows, :], y, g_ref[...], b_ref[...])

    nxt = front(tiles[0])
    for i, rows in enumerate(tiles):
        cur = nxt
        if i + 1 < len(tiles):
            nxt = front(tiles[i + 1])
        back(rows, cur)


def _mem_sublayer(h, mem, w_q, w_kv, w_o, g, b, tm=1024, sub=256):
    bsz, seq, d = h.shape
    nm = mem.shape[1]
    tm = min(tm, seq)
    kv = _proj(mem.reshape(bsz * nm, d), w_kv.astype(BF16), BF16).reshape(bsz, nm, 2 * d)
    wq = (w_q * (MEM_HEAD_DIM ** -0.5)).astype(BF16)
    row = lambda b_, i: (b_, i, 0)
    fix = lambda b_, i: (0, 0)
    return pl.pallas_call(
        functools.partial(_mem_attn_kernel, sub=min(sub, tm)),
        grid=(bsz, seq // tm),
        in_specs=[pl.BlockSpec((1, tm, d), row),
                  pl.BlockSpec((1, nm, 2 * d), lambda b_, i: (b_, 0, 0)),
                  pl.BlockSpec((d, d), fix), pl.BlockSpec((d, d), fix),
                  pl.BlockSpec((1, d), fix), pl.BlockSpec((1, d), fix)],
        out_specs=pl.BlockSpec((1, tm, d), row),
        out_shape=jax.ShapeDtypeStruct((bsz, seq, d), F32),
        compiler_params=_cparams("parallel", "parallel"),
        name="mem_attn",
    )(h, kv, wq, w_o.astype(BF16), g, b)


def _mlp_kernel(h_ref, w1_ref, w2_ref, g_ref, b_ref, o_ref, *, sub):
    for r0 in range(0, h_ref.shape[0], sub):
        rows = slice(r0, r0 + sub)
        h = h_ref[rows, :]
        u = jnp.dot(h.astype(BF16), w1_ref[...], preferred_element_type=F32)
        u = jnp.maximum(u, 0.0)
        y = jnp.dot((u * u).astype(BF16), w2_ref[...], preferred_element_type=F32)
        o_ref[rows, :] = _residual_ln(h, y, g_ref[...], b_ref[...])


def _mlp_sublayer(h, w1, w2, g, b, tm=1024, sub=256):
    bsz, seq, d = h.shape
    t = bsz * seq
    tm = min(tm, t)
    ff = w1.shape[1]
    resident = pl.Buffered(1)
    out = pl.pallas_call(
        functools.partial(_mlp_kernel, sub=min(sub, tm)),
        grid=(t // tm,),
        in_specs=[pl.BlockSpec((tm, d), lambda i: (i, 0)),
                  pl.BlockSpec((d, ff), lambda i: (0, 0), pipeline_mode=resident),
                  pl.BlockSpec((ff, d), lambda i: (0, 0), pipeline_mode=resident),
                  pl.BlockSpec((1, d), lambda i: (0, 0)),
                  pl.BlockSpec((1, d), lambda i: (0, 0))],
        out_specs=pl.BlockSpec((tm, d), lambda i: (i, 0)),
        out_shape=jax.ShapeDtypeStruct((t, d), F32),
        compiler_params=_cparams("parallel"),
        name="mlp",
    )(h.reshape(t, d), w1.astype(BF16), w2.astype(BF16), g, b)
    return out.reshape(bsz, seq, d)


def kernel(x, mem, ln_g, ln_b, lru_w_in, lru_conv_w, lru_conv_b, lru_gate_w, lru_gate_b, lru_lambda, lru_w_out, rw_mu, rw_w_r, rw_w_k, rw_w_v, rw_w0, rw_w1, rw_w2, rw_a0, rw_a1, rw_a2, rw_g1, rw_g2, rw_k_k, rw_k_a, rw_r_k, rw_lnx_g, rw_lnx_b, rw_w_o, ca_w_qkv, ca_rel_bias, ca_w_o, mx_w_q, mx_w_kv, mx_w_o, mlp_w1, mlp_w2):
    h = x
    for i in range(DEPTH):
        kind, j = i % N_MIXERS, i // N_MIXERS
        g = lambda s: ln_g[i, s][None, :]
        b = lambda s: ln_b[i, s][None, :]
        if kind == 0:
            h = _rglru_sublayer(h, lru_w_in[j], lru_conv_w[j], lru_conv_b[j], lru_gate_w[j], lru_gate_b[j],
                                lru_lambda[j], lru_w_out[j], g(0), b(0))
        elif kind == 1:
            h = _rwkv_sublayer(h, rw_mu[j], rw_w_r[j], rw_w_k[j], rw_w_v[j], rw_w0[j], rw_w1[j], rw_w2[j],
                               rw_a0[j], rw_a1[j], rw_a2[j], rw_g1[j], rw_g2[j], rw_k_k[j], rw_k_a[j],
                               rw_r_k[j], rw_lnx_g[j], rw_lnx_b[j], rw_w_o[j], g(0), b(0))
        else:
            h = _attn_sublayer(h, ca_w_qkv[j], ca_rel_bias[j], ca_w_o[j], g(0), b(0))
        h = _mem_sublayer(h, mem, mx_w_q[i], mx_w_kv[i], mx_w_o[i], g(1), b(1))
        h = _mlp_sublayer(h, mlp_w1[i], mlp_w2[i], g(2), b(2))
    return h
```

```python
import functools

import jax
import jax.numpy as jnp
from jax import lax
from jax.experimental import pallas as pl
from jax.experimental.pallas import tpu as pltpu

F32 = jnp.float32
BF16 = jnp.bfloat16

D_MODEL = 1024
DEPTH = 4
CHUNK = 64
N_MIXERS = 3
DEEPNORM_ALPHA = (2 * DEPTH) ** 0.25
LN_EPS = 1e-5

D_RNN = 1344
LRU_BLOCKS = 16
LRU_BLOCK_SIZE = D_RNN // LRU_BLOCKS
CONV_WIDTH = 4
RG_LRU_C = 8.0
LANES = 128
SUBLANES = 8
D_RNN_PAD = -(-D_RNN // LANES) * LANES

RW_HEAD_SIZE = 64
RW_HEADS = D_MODEL // RW_HEAD_SIZE
RW_PAIRS = D_MODEL // LANES
RW_GN_EPS = 64e-5
RW_CHUNK = 64

ATT_HEADS = 16
ATT_HEAD_DIM = D_MODEL // ATT_HEADS
ATT_PAIRS = D_MODEL // LANES
BAND_CHUNKS = 9
MAX_REL = 2 * CHUNK
NEG_INF = -1e30
ATT_TQ = 256
ATT_WIN = ATT_TQ + (BAND_CHUNKS - 1) * CHUNK
ATT_TBL = ATT_WIN + (BAND_CHUNKS - 1) * CHUNK

MEM_HEADS = 4
MEM_HEAD_DIM = D_MODEL // MEM_HEADS
D_FF = 4 * D_MODEL

VMEM_LIMIT = 56 * 1024 * 1024


def _cparams(*sem):
    return pltpu.CompilerParams(dimension_semantics=sem, vmem_limit_bytes=VMEM_LIMIT)


def _residual_ln(hres, y, g, b):
    z = DEEPNORM_ALPHA * hres + y
    mu = jnp.mean(z, axis=-1, keepdims=True)
    zc = z - mu
    var = jnp.mean(zc * zc, axis=-1, keepdims=True)
    return zc * lax.rsqrt(var + LN_EPS) * g + b


def _gelu_tanh(x):
    return 0.5 * x * (1.0 + jnp.tanh(0.7978845608028654 * (x + 0.044715 * (x * x * x))))


def _softplus(x):
    return jnp.maximum(x, 0.0) + jnp.log1p(jnp.exp(-jnp.abs(x)))


def _sigmoid(x):
    return 0.5 * jnp.tanh(0.5 * x) + 0.5


def _split3(x):
    hi = x.astype(BF16)
    r1 = x - hi.astype(F32)
    mid = r1.astype(BF16)
    lo = (r1 - mid.astype(F32)).astype(BF16)
    return hi, mid, lo


def _proj_kernel(x_ref, w_ref, o_ref, *, chunk):
    x = x_ref[...].astype(BF16)
    n = w_ref.shape[1]
    for n0 in range(0, n, chunk):
        y = jnp.dot(x, w_ref[:, n0:n0 + chunk], preferred_element_type=F32)
        o_ref[:, n0:n0 + chunk] = y.astype(o_ref.dtype)


def _proj(x2d, w, out_dtype, tm=512, chunk=512):
    t, k = x2d.shape
    n = w.shape[1]
    tm = min(tm, t)
    return pl.pallas_call(
        functools.partial(_proj_kernel, chunk=chunk),
        grid=(t // tm,),
        in_specs=[pl.BlockSpec((tm, k), lambda i: (i, 0)),
                  pl.BlockSpec((k, n), lambda i: (0, 0))],
        out_specs=pl.BlockSpec((tm, n), lambda i: (i, 0)),
        out_shape=jax.ShapeDtypeStruct((t, n), out_dtype),
        compiler_params=_cparams("parallel"),
        name="proj",
    )(x2d, w)


def _proj_ln_kernel(y_ref, w_ref, h_ref, g_ref, b_ref, o_ref, *, sub):
    for r0 in range(0, y_ref.shape[0], sub):
        rows = slice(r0, r0 + sub)
        y = jnp.dot(y_ref[rows, :].astype(BF16), w_ref[...], preferred_element_type=F32)
        o_ref[rows, :] = _residual_ln(h_ref[rows, :], y, g_ref[...], b_ref[...])


def _proj_ln(y2d, w, h2d, g, b, tm=1024, sub=256):
    t, k = y2d.shape
    d = w.shape[1]
    tm = min(tm, t)
    return pl.pallas_call(
        functools.partial(_proj_ln_kernel, sub=min(sub, tm)),
        grid=(t // tm,),
        in_specs=[pl.BlockSpec((tm, k), lambda i: (i, 0)),
                  pl.BlockSpec((k, d), lambda i: (0, 0)),
                  pl.BlockSpec((tm, d), lambda i: (i, 0)),
                  pl.BlockSpec((1, d), lambda i: (0, 0)),
                  pl.BlockSpec((1, d), lambda i: (0, 0))],
        out_specs=pl.BlockSpec((tm, d), lambda i: (i, 0)),
        out_shape=jax.ShapeDtypeStruct((t, d), F32),
        compiler_params=_cparams("parallel"),
        name="proj_ln",
    )(y2d, w, h2d, g, b)


LRU_GATE_TILE = 256
LRU_GATE_WIN = 512


def _lru_gate_window_starts():
    starts = []
    for n0 in range(0, D_RNN_PAD, LRU_GATE_TILE):
        first_block = min(n0, D_RNN - 1) // LRU_BLOCK_SIZE
        last_block = min(n0 + LRU_GATE_TILE - 1, D_RNN - 1) // LRU_BLOCK_SIZE
        lo = (first_block * LRU_BLOCK_SIZE) // LANES * LANES
        lo = min(lo, D_RNN_PAD - LRU_GATE_WIN)
        assert (last_block + 1) * LRU_BLOCK_SIZE <= lo + LRU_GATE_WIN
        starts.append(lo)
    return starts


LRU_HIST = (CONV_WIDTH - 1) * SUBLANES


def _lru_kernel(h_ref, win_ref, cw_ref, cb_ref, gw_ref, gb_ref, lam_ref, wout_ref, g_ref, b_ref, o_ref,
                xbuf, hstate):
    @pl.when(pl.program_id(0) == 0)
    def _():
        xbuf[:, 0:LRU_HIST, :] = jnp.zeros((xbuf.shape[0], LRU_HIST, xbuf.shape[2]), F32)
        hstate[...] = jnp.zeros(hstate.shape, F32)

    ng = h_ref.shape[0] // SUBLANES
    fronts = [_lru_front(gi, h_ref, win_ref, cw_ref, cb_ref, gw_ref, xbuf) for gi in range(ng)]
    for gi in range(ng):
        _lru_back(gi, *fronts[gi], gb_ref, lam_ref, wout_ref, g_ref, b_ref, o_ref, hstate)


def _lru_front(gi, h_ref, win_ref, cw_ref, cb_ref, gw_ref, xbuf):
    tt, d = h_ref.shape[1], h_ref.shape[2]
    c = cw_ref.shape[1]
    ts = tt * SUBLANES
    hres = h_ref[gi * SUBLANES:(gi + 1) * SUBLANES]
    ht = pltpu.einshape("btd->tbd", hres).reshape(ts, d)
    u = jnp.dot(ht.astype(BF16), win_ref[...], preferred_element_type=F32)
    gate_branch = _gelu_tanh(u[:, :c])
    x = u[:, c:]
    xbuf[gi, LRU_HIST:LRU_HIST + ts, :] = x
    xr = cb_ref[...] + cw_ref[3:4, :] * x
    for k in range(CONV_WIDTH - 1):
        xr = xr + cw_ref[k:k + 1, :] * xbuf[gi, k * SUBLANES:k * SUBLANES + ts, :]
    xbuf[gi, 0:LRU_HIST, :] = xbuf[gi, ts:ts + LRU_HIST, :]

    xb = xr.astype(BF16)
    tiles = [jnp.dot(xb[:, lo:lo + LRU_GATE_WIN], gw_ref[n], preferred_element_type=F32)
             for n, lo in enumerate(_lru_gate_window_starts())]
    r_pre = jnp.concatenate([t[:, :LRU_GATE_TILE] for t in tiles], axis=1)[:, :c]
    i_pre = jnp.concatenate([t[:, LRU_GATE_TILE:] for t in tiles], axis=1)[:, :c]
    return hres, gate_branch, xr, r_pre, i_pre


def _lru_back(gi, hres, gate_branch, xr, r_pre, i_pre, gb_ref, lam_ref, wout_ref, g_ref, b_ref, o_ref, hstate):
    ts, c = xr.shape
    tt = ts // SUBLANES
    r_gate = _sigmoid(r_pre + gb_ref[0:1, :])
    i_gate = _sigmoid(i_pre + gb_ref[1:2, :])
    log_a = (-RG_LRU_C) * r_gate * _softplus(-lam_ref[...])
    a = jnp.exp(log_a)
    bt = jnp.sqrt(-jnp.tanh(log_a) * (a * a + 1.0)) * (i_gate * xr)

    h = hstate[gi]
    steps = []
    for t in range(tt):
        slab = slice(t * SUBLANES, (t + 1) * SUBLANES)
        h = a[slab, :] * h + bt[slab, :]
        steps.append(h)
    hstate[gi] = h
    hs = jnp.concatenate(steps, axis=0)

    y = jnp.dot((hs * gate_branch).astype(BF16), wout_ref[...], preferred_element_type=F32)
    yb = pltpu.einshape("tbd->btd", y.reshape(tt, SUBLANES, y.shape[1]))
    o_ref[gi * SUBLANES:(gi + 1) * SUBLANES] = _residual_ln(hres, yb, g_ref[...], b_ref[...])


def _lru(h, win, cw, cb, gw, gb, lam, wout, g, b, tt=32):
    bsz, seq, d = h.shape
    c = cw.shape[1]
    assert bsz % SUBLANES == 0
    tt = min(tt, seq)
    ng = bsz // SUBLANES
    resident = lambda w: pl.BlockSpec(w.shape, lambda j: (0,) * w.ndim, pipeline_mode=pl.Buffered(1))
    return pl.pallas_call(
        _lru_kernel,
        grid=(seq // tt,),
        in_specs=[pl.BlockSpec((bsz, tt, d), lambda j: (0, j, 0))]
                 + [resident(w) for w in (win, cw, cb, gw, gb, lam, wout, g, b)],
        out_specs=pl.BlockSpec((bsz, tt, d), lambda j: (0, j, 0)),
        out_shape=jax.ShapeDtypeStruct((bsz, seq, d), F32),
        scratch_shapes=[pltpu.VMEM((ng, LRU_HIST + tt * SUBLANES, c), F32), pltpu.VMEM((ng, SUBLANES, c), F32)],
        compiler_params=_cparams("arbitrary"),
        name="lru",
    )(h, win, cw, cb, gw, gb, lam, wout, g, b)


def _pad_last(x, n):
    return jnp.pad(x, [(0, 0)] * (x.ndim - 1) + [(0, n - x.shape[-1])])


def _rglru_sublayer(h, w_in, conv_w, conv_b, gate_w, gate_b, lam, w_out, g, b):
    bsz, seq, d = h.shape
    c = D_RNN_PAD
    w_cat = jnp.concatenate([_pad_last(w_in[:, :D_RNN], c), _pad_last(w_in[:, D_RNN:], c)], axis=1).astype(BF16)
    eye = jnp.eye(LRU_BLOCKS, dtype=F32)
    dense = jnp.einsum('gncd,nm->gncmd', gate_w, eye).reshape(2, D_RNN, D_RNN)
    starts = _lru_gate_window_starts()
    dense = jnp.pad(dense, ((0, 0), (0, c - D_RNN), (0, len(starts) * LRU_GATE_TILE - D_RNN)))
    gw = jnp.stack([jnp.concatenate([dense[gi, lo:lo + LRU_GATE_WIN, n * LRU_GATE_TILE:(n + 1) * LRU_GATE_TILE]
                                     for gi in range(2)], axis=1) for n, lo in enumerate(starts)]).astype(BF16)
    wout = jnp.pad(w_out, ((0, c - D_RNN), (0, 0))).astype(BF16)
    return _lru(h, w_cat, _pad_last(conv_w, c), _pad_last(conv_b[None, :], c), gw,
                _pad_last(gate_b, c), _pad_last(lam[None, :], c), wout, g, b)


def _rwkv_in_kernel(x_ref, xp_ref, mu_ref, wr_ref, wk_ref, wv_ref, w0_ref, w1_ref, w2_ref,
                    a0_ref, a1_ref, a2_ref, g1_ref, g2_ref,
                    r_ref, k_ref, v_ref, ld_ref, a_ref, g_ref, *, sub):
    i = pl.program_id(1)
    x_all = x_ref[0]
    row = lax.broadcasted_iota(jnp.int32, x_all.shape, 0)
    prev = jnp.where(i > 0, xp_ref[0, SUBLANES - 1:SUBLANES, :], 0.0)
    xx_all = jnp.where(row == 0, prev, pltpu.roll(x_all, 1, 0)) - x_all
    dot = lambda p, q: jnp.dot(p, q, preferred_element_type=F32)
    for r0 in range(0, x_all.shape[0], sub):
        rows = slice(r0, r0 + sub)
        x, xx = x_all[rows, :], xx_all[rows, :]
        mix = lambda c: (x + xx * mu_ref[c:c + 1, :]).astype(BF16)
        r_ref[0, rows, :] = dot(mix(0), wr_ref[...]).astype(r_ref.dtype)
        k_ref[0, rows, :] = dot(mix(2), wk_ref[...]).astype(k_ref.dtype)
        v_ref[0, rows, :] = dot(mix(3), wv_ref[...]).astype(v_ref.dtype)
        wl = w0_ref[...] + dot(jnp.tanh(dot(mix(1), w1_ref[...])).astype(BF16), w2_ref[...])
        w_log = -_softplus(-wl) - 0.5
        ld_ref[0, rows, :] = -jnp.exp(w_log)
        a_ref[0, rows, :] = _sigmoid(a0_ref[...] + dot(dot(mix(4), a1_ref[...]).astype(BF16), a2_ref[...]))
        g_ref[0, rows, :] = dot(_sigmoid(dot(mix(5), g1_ref[...])).astype(BF16), g2_ref[...]).astype(g_ref.dtype)


def _rwkv_in(h, mu, wr, wk, wv, w0, w1, w2, a0, a1, a2, g1, g2, tm=512, sub=256):
    bsz, seq, d = h.shape
    tm = min(tm, seq)
    row = lambda b, i: (b, i, 0)
    fix = lambda b, i: (0, 0)
    prev = lambda b, i: (b, jnp.maximum(i * (tm // SUBLANES) - 1, 0), 0)
    ws = [mu, wr, wk, wv, w0, w1, w2, a0, a1, a2, g1, g2]
    return pl.pallas_call(
        functools.partial(_rwkv_in_kernel, sub=min(sub, tm)),
        grid=(bsz, seq // tm),
        in_specs=[pl.BlockSpec((1, tm, d), row), pl.BlockSpec((1, SUBLANES, d), prev)]
                 + [pl.BlockSpec(w.shape, fix) for w in ws],
        out_specs=[pl.BlockSpec((1, tm, d), row)] * 6,
        out_shape=[jax.ShapeDtypeStruct((bsz, seq, d), dt) for dt in (BF16, BF16, BF16, F32, F32, BF16)],
        compiler_params=_cparams("parallel", "parallel"),
        name="rwkv_in",
    )(h, h, *ws)


def _bdot(a, b, dims):
    return jnp.einsum(dims, a, b, preferred_element_type=F32)


def _rwkv_rec_kernel(r_ref, k_ref, v_ref, ld_ref, a_ref, g_ref, kk_ref, ka_ref, rk_ref, lg_ref, lb_ref,
                     o_ref, state):
    @pl.when(pl.program_id(1) == 0)
    def _():
        state[...] = jnp.zeros(state.shape, F32)

    s_val = state[...]
    for c in range(r_ref.shape[1] // RW_CHUNK):
        s_val = _rwkv_chunk(slice(c * RW_CHUNK, (c + 1) * RW_CHUNK), s_val, r_ref, k_ref, v_ref, ld_ref, a_ref,
                            g_ref, kk_ref, ka_ref, rk_ref, lg_ref, lb_ref, o_ref)
    state[...] = s_val


def _rwkv_chunk(rows, s_old, r_ref, k_ref, v_ref, ld_ref, a_ref, g_ref, kk_ref, ka_ref, rk_ref, lg_ref, lb_ref,
                o_ref):
    L = RW_CHUNK
    P = RW_PAIRS
    N = RW_HEAD_SIZE
    pairs = lambda ref: jnp.stack([ref[0, rows, LANES * p:LANES * (p + 1)] for p in range(P)]).astype(F32)
    vec = lambda ref: jnp.stack([ref[:, LANES * p:LANES * (p + 1)] for p in range(P)])

    lane = lax.broadcasted_iota(jnp.int32, (1, 1, LANES), 2)
    head0 = lane < N
    ri = lax.broadcasted_iota(jnp.int32, (LANES, LANES), 0)
    ci = lax.broadcasted_iota(jnp.int32, (LANES, LANES), 1)
    seg = ((ri // N) == (ci // N)).astype(BF16)

    def segsum(x):
        flat = x.reshape(P * L, LANES)
        hi = flat.astype(BF16)
        lo = (flat - hi.astype(F32)).astype(BF16)
        out = jnp.dot(hi, seg, preferred_element_type=F32) + jnp.dot(lo, seg, preferred_element_type=F32)
        return out.reshape(P, L, LANES)

    r = pairs(r_ref)
    k = pairs(k_ref)
    v = pairs(v_ref)
    a = pairs(a_ref)

    kk = k * vec(kk_ref)
    kk = kk / jnp.maximum(jnp.sqrt(segsum(kk * kk)), 1e-12)
    k = k * (1.0 + (a - 1.0) * vec(ka_ref))
    aa = -kk
    bb = kk * a

    ld_full = ld_ref[0, rows, :]
    tr = lax.broadcasted_iota(jnp.int32, (L, L), 0)
    tc = lax.broadcasted_iota(jnp.int32, (L, L), 1)
    tri = (tc <= tr).astype(BF16)
    cum_full = sum(jnp.dot(tri, t, preferred_element_type=F32) for t in _split3(ld_full))
    cum = jnp.stack([cum_full[:, LANES * p:LANES * (p + 1)] for p in range(P)])
    ld = pairs(ld_ref)
    gam = jnp.exp(cum)
    gam_prev = jnp.exp(cum - ld)
    cum_end = cum[:, L - 1:L, :]
    r_t = r * gam
    a_t = aa * gam_prev
    k_t = k * jnp.exp(-cum)
    b_t = bb * jnp.exp(-cum)
    k_e = k * jnp.exp(cum_end - cum)
    b_e = bb * jnp.exp(cum_end - cum)

    def stacked(x):
        return jnp.concatenate([jnp.where(head0, x, 0.0), jnp.where(head0, 0.0, x)], axis=1)

    ar = jnp.concatenate([stacked(a_t), stacked(r_t)], axis=1).astype(BF16)
    kb = jnp.concatenate([stacked(k_t), stacked(b_t)], axis=1).astype(BF16)
    v_s = stacked(v)

    s4 = _bdot(ar, kb, 'pik,pjk->pij')
    ah = _bdot(ar, s_old.astype(BF16), 'pik,plk->pil')
    M = 2 * L
    si = lax.broadcasted_iota(jnp.int32, (1, M, M), 1)
    sj = lax.broadcasted_iota(jnp.int32, (1, M, M), 2)
    same = (si // L) == (sj // L)
    strict = same & ((sj % L) < (si % L))
    incl = same & ((sj % L) <= (si % L))
    a_k = jnp.where(strict, s4[:, :M, :M], 0.0)
    a_b = jnp.where(strict, s4[:, :M, M:], 0.0)
    r_k = jnp.where(incl, s4[:, M:, :M], 0.0)
    r_b = jnp.where(incl, s4[:, M:, M:], 0.0)

    eye = (si == sj).astype(F32)
    lower1 = (si // 2 == sj // 2) & (si % 2 == 1) & (sj % 2 == 0)
    T = eye + jnp.where(lower1, a_b, 0.0)
    ab16 = a_b.astype(BF16)
    s = 2
    hi_ = lax.broadcasted_iota(jnp.int32, (1, M // 2, M), 1)
    hj_ = lax.broadcasted_iota(jnp.int32, (1, M // 2, M), 2)
    while s < L:
        tb = T.astype(BF16)
        if s < SUBLANES:
            cm = (si // (2 * s) == sj // (2 * s)) & (si % (2 * s) >= s) & (sj % (2 * s) < s)
            tat = _bdot(_bdot(tb, ab16, 'pij,pjk->pik').astype(BF16), tb, 'pij,pjk->pik')
            T = T + jnp.where(cm, tat, 0.0)
        else:
            nblk = M // (2 * s)
            low = jnp.concatenate([T[:, (2 * k + 1) * s:(2 * k + 2) * s, :] for k in range(nblk)], axis=1)
            upd = _bdot(_bdot(low.astype(BF16), ab16, 'pij,pjk->pik').astype(BF16), tb, 'pij,pjk->pik')
            upd = jnp.where((hj_ // (2 * s) == hi_ // s) & (hj_ % (2 * s) < s), upd, 0.0)
            zero = jnp.zeros((P, s, M), F32)
            T = T + jnp.concatenate([blk for k in range(nblk) for blk in (zero, upd[:, k * s:(k + 1) * s, :])],
                                    axis=1)
        s *= 2

    x_s = ah[:, :M, :] + _bdot(a_k.astype(BF16), v_s.astype(BF16), 'pij,pjl->pil')
    u_s = _bdot(T.astype(BF16), x_s.astype(BF16), 'pij,pjl->pil')
    vu = jnp.concatenate([v_s, u_s], axis=1).astype(BF16)
    rkb = jnp.concatenate([r_k, r_b], axis=2).astype(BF16)
    y_s = ah[:, M:, :] + _bdot(rkb, vu, 'pij,pjl->pil')
    y = y_s[:, :L, :] + y_s[:, L:, :]

    kb_e = jnp.concatenate([stacked(k_e), stacked(b_e)], axis=1).astype(BF16)
    s_new = s_old * jnp.exp(cum_end) + _bdot(vu, kb_e, 'pti,ptj->pij')

    mean = segsum(y) * (1.0 / N)
    yc = y - mean
    var = segsum(yc * yc) * (1.0 / N)
    yn = yc * lax.rsqrt(var + RW_GN_EPS) * vec(lg_ref) + vec(lb_ref)
    bonus = segsum(r * k * vec(rk_ref)) * v
    out = (yn + bonus) * pairs(g_ref)
    for p in range(P):
        o_ref[0, rows, LANES * p:LANES * (p + 1)] = out[p].astype(o_ref.dtype)
    return s_new


def _rwkv_rec(r, k, v, ld, a, g, k_k, k_a, r_k, lnx_g, lnx_b, chunks_per_step=4):
    bsz, seq, d = r.shape
    L = RW_CHUNK * min(chunks_per_step, seq // RW_CHUNK)
    row = lambda b, j: (b, j, 0)
    fix = lambda b, j: (0, 0)
    return pl.pallas_call(
        _rwkv_rec_kernel,
        grid=(bsz, seq // L),
        in_specs=[pl.BlockSpec((1, L, d), row)] * 6 + [pl.BlockSpec((1, d), fix)] * 5,
        out_specs=pl.BlockSpec((1, L, d), row),
        out_shape=jax.ShapeDtypeStruct((bsz, seq, d), BF16),
        scratch_shapes=[pltpu.VMEM((RW_PAIRS, LANES, LANES), F32)],
        compiler_params=_cparams("parallel", "arbitrary"),
        name="rwkv_rec",
    )(r, k, v, ld, a, g, k_k, k_a, r_k, lnx_g, lnx_b)


def _rwkv_sublayer(h, mu, w_r, w_k, w_v, w0, w1, w2, a0, a1, a2, g1, g2, k_k, k_a, r_k,
                   lnx_g, lnx_b, w_o, g, b):
    bsz, seq, d = h.shape
    bf = lambda w: w.astype(BF16)
    row = lambda w: w.reshape(1, d)
    mu8 = jnp.pad(mu, ((0, SUBLANES - mu.shape[0]), (0, 0)))
    r, k, v, ld, a, gg = _rwkv_in(h, mu8, bf(w_r), bf(w_k), bf(w_v), row(w0), bf(w1), bf(w2),
                                  row(a0), bf(a1), bf(a2), bf(g1), bf(g2))
    y = _rwkv_rec(r, k, v, ld, a, gg, row(k_k), row(k_a), row(r_k), row(lnx_g), row(lnx_b))
    return _proj_ln(y.reshape(bsz * seq, d), bf(w_o), h.reshape(bsz * seq, d), g, b).reshape(bsz, seq, d)


def _chunk_attn_kernel(q_ref, k_ref, v_ref, tbl_ref, o_ref):
    seq = q_ref.shape[1]
    blocks_back = (ATT_WIN - ATT_TQ) // ATT_TQ
    lane = lax.broadcasted_iota(jnp.int32, (ATT_TQ, LANES), 1)
    nq = seq // ATT_TQ

    def window(qi):
        back = min(qi, blocks_back)
        ws = (qi - back) * ATT_TQ
        width = (back + 1) * ATT_TQ
        off = (blocks_back - back) * ATT_TQ
        return ws, width, off

    def scores(qi):
        ws, width, _ = window(qi)
        kw = k_ref[0, ws:ws + width, :]
        q = q_ref[0, qi * ATT_TQ:(qi + 1) * ATT_TQ, :]
        out = []
        for hh in range(2):
            qh = jnp.where((lane // ATT_HEAD_DIM) == hh, q, jnp.zeros_like(q))
            out.append(lax.dot_general(qh, kw, (((1,), (1,)), ((), ())), preferred_element_type=F32))
        return out

    s_next = scores(0)
    for qi in range(nq):
        s_cur = s_next
        if qi + 1 < nq:
            s_next = scores(qi + 1)
        ws, width, off = window(qi)
        vw = v_ref[0, ws:ws + width, :]
        vlane = lax.broadcasted_iota(jnp.int32, vw.shape, 1)
        pv = []
        for hh in range(2):
            s = s_cur[hh] + tbl_ref[0, hh, :, off:off + width]
            m = jnp.max(s, axis=-1, keepdims=True)
            e = jnp.exp((s - m).astype(BF16))
            vh = jnp.where((vlane // ATT_HEAD_DIM) == hh, vw, jnp.ones_like(vw))
            pv.append(jnp.dot(e, vh, preferred_element_type=F32))
        num = jnp.where(lane < ATT_HEAD_DIM, pv[0], pv[1])
        den = jnp.where(lane < ATT_HEAD_DIM, pltpu.roll(pv[0], ATT_HEAD_DIM, 1), pltpu.roll(pv[1], ATT_HEAD_DIM, 1))
        o_ref[0, qi * ATT_TQ:(qi + 1) * ATT_TQ, :] = (num / den).astype(o_ref.dtype)


def _chunk_attn(qkv, tbl):
    bsz, seq, _ = qkv.shape
    return pl.pallas_call(
        _chunk_attn_kernel,
        grid=(bsz, ATT_PAIRS),
        in_specs=[pl.BlockSpec((1, seq, LANES), lambda b, p: (b, 0, p)),
                  pl.BlockSpec((1, seq, LANES), lambda b, p: (b, 0, ATT_PAIRS + p)),
                  pl.BlockSpec((1, seq, LANES), lambda b, p: (b, 0, 2 * ATT_PAIRS + p)),
                  pl.BlockSpec((1, 2, ATT_TQ, ATT_TBL), lambda b, p: (p, 0, 0, 0))],
        out_specs=pl.BlockSpec((1, seq, LANES), lambda b, p: (b, 0, p)),
        out_shape=jax.ShapeDtypeStruct((bsz, seq, D_MODEL), BF16),
        compiler_params=_cparams("parallel", "parallel"),
        name="chunk_attn",
    )(qkv, qkv, qkv, tbl)


def _attn_bias_table(rel_bias):
    left = ATT_WIN - ATT_TQ
    period = ATT_TBL + ATT_TQ
    j = jnp.arange(period)
    diff = jnp.where(j < ATT_TBL, j, j - period)
    vec = rel_bias[:, jnp.clip(left - diff, -MAX_REL, MAX_REL) + MAX_REL].astype(F32)
    flat = jnp.tile(vec, (1, ATT_TQ))[:, :ATT_TQ * (period - 1)]
    bias = flat.reshape(-1, ATT_TQ, period - 1)[:, :, :ATT_TBL]
    ic = jnp.arange(ATT_TQ)[:, None] // CHUNK
    mc = jnp.arange(ATT_TBL)[None, :] // CHUNK
    valid = (mc >= ic) & (mc <= ic + BAND_CHUNKS - 1)
    tbl = jnp.where(valid[None], bias, NEG_INF)
    return tbl.reshape(ATT_PAIRS, 2, ATT_TQ, ATT_TBL)


def _attn_sublayer(h, w_qkv, rel_bias, w_o, g, b):
    bsz, seq, d = h.shape
    assert seq % ATT_TQ == 0 and seq >= ATT_WIN
    scale = jnp.concatenate([jnp.full((d,), ATT_HEAD_DIM ** -0.5, F32), jnp.ones((2 * d,), F32)])
    qkv = _proj(h.reshape(bsz * seq, d), (w_qkv * scale).astype(BF16), BF16)
    o = _chunk_attn(qkv.reshape(bsz, seq, 3 * d), _attn_bias_table(rel_bias))
    return _proj_ln(o.reshape(bsz * seq, d), w_o.astype(BF16), h.reshape(bsz * seq, d), g, b).reshape(bsz, seq, d)


def _mem_attn_kernel(h_ref, kv_ref, wq_ref, wo_ref, g_ref, b_ref, o_ref, *, sub):
    tiles = [slice(r0, r0 + sub) for r0 in range(0, h_ref.shape[1], sub)]

    def front(rows):
        q = jnp.dot(h_ref[0, rows, :].astype(BF16), wq_ref[...], preferred_element_type=F32).astype(BF16)
        return [lax.dot_general(q[:, hh * MEM_HEAD_DIM:(hh + 1) * MEM_HEAD_DIM],
                                kv_ref[0, :, hh * MEM_HEAD_DIM:(hh + 1) * MEM_HEAD_DIM],
                                (((1,), (1,)), ((), ())), preferred_element_type=F32) for hh in range(MEM_HEADS)]

    def back(rows, scores):
        outs = []
        for hh, s in enumerate(scores):
            vh = kv_ref[0, :, D_MODEL + hh * MEM_HEAD_DIM:D_MODEL + (hh + 1) * MEM_HEAD_DIM]
            m = jnp.max(s, axis=-1, keepdims=True)
            e = jnp.exp(s - m)
            l = jnp.sum(e, axis=-1, keepdims=True)
            outs.append((jnp.dot(e.astype(BF16), vh, preferred_element_type=F32) / l).astype(BF16))
        y = jnp.dot(jnp.concatenate(outs, axis=1), wo_ref[...], preferred_element_type=F32)
        o_ref[0, rows, :] = _residual_ln(h_ref[0, rows, :], y, g_ref[...], b_ref[...])

    nxt = front(tiles[0])
    for i, rows in enumerate(tiles):
        cur = nxt
        if i + 1 < len(tiles):
            nxt = front(tiles[i + 1])
        back(rows, cur)


def _mem_sublayer(h, mem, w_q, w_kv, w_o, g, b, tm=1024, sub=256):
    bsz, seq, d = h.shape
    nm = mem.shape[1]
    tm = min(tm, seq)
    kv = _proj(mem.reshape(bsz * nm, d), w_kv.astype(BF16), BF16).reshape(bsz, nm, 2 * d)
    wq = (w_q * (MEM_HEAD_DIM ** -0.5)).astype(BF16)
    row = lambda b_, i: (b_, i, 0)
    fix = lambda b_, i: (0, 0)
    return pl.pallas_call(
        functools.partial(_mem_attn_kernel, sub=min(sub, tm)),
        grid=(bsz, seq // tm),
        in_specs=[pl.BlockSpec((1, tm, d), row),
                  pl.BlockSpec((1, nm, 2 * d), lambda b_, i: (b_, 0, 0)),
                  pl.BlockSpec((d, d), fix), pl.BlockSpec((d, d), fix),
                  pl.BlockSpec((1, d), fix), pl.BlockSpec((1, d), fix)],
        out_specs=pl.BlockSpec((1, tm, d), row),
        out_shape=jax.ShapeDtypeStruct((bsz, seq, d), F32),
        compiler_params=_cparams("parallel", "parallel"),
        name="mem_attn",
    )(h, kv, wq, w_o.astype(BF16), g, b)


def _mlp_kernel(h_ref, w1_ref, w2_ref, g_ref, b_ref, o_ref, *, sub):
    for r0 in range(0, h_ref.shape[0], sub):
        rows = slice(r0, r0 + sub)
        h = h_ref[rows, :]
        u = jnp.dot(h.astype(BF16), w1_ref[...], preferred_element_type=F32)
        u = jnp.maximum(u, 0.0)
        y = jnp.dot((u * u).astype(BF16), w2_ref[...], preferred_element_type=F32)
        o_ref[rows, :] = _residual_ln(h, y, g_ref[...], b_ref[...])


def _mlp_sublayer(h, w1, w2, g, b, tm=1024, sub=256):
    bsz, seq, d = h.shape
    t = bsz * seq
    tm = min(tm, t)
    ff = w1.shape[1]
    resident = pl.Buffered(1)
    out = pl.pallas_call(
        functools.partial(_mlp_kernel, sub=min(sub, tm)),
        grid=(t // tm,),
        in_specs=[pl.BlockSpec((tm, d), lambda i: (i, 0)),
                  pl.BlockSpec((d, ff), lambda i: (0, 0), pipeline_mode=resident),
                  pl.BlockSpec((ff, d), lambda i: (0, 0), pipeline_mode=resident),
                  pl.BlockSpec((1, d), lambda i: (0, 0)),
                  pl.BlockSpec((1, d), lambda i: (0, 0))],
        out_specs=pl.BlockSpec((tm, d), lambda i: (i, 0)),
        out_shape=jax.ShapeDtypeStruct((t, d), F32),
        compiler_params=_cparams("parallel"),
        name="mlp",
    )(h.reshape(t, d), w1.astype(BF16), w2.astype(BF16), g, b)
    return out.reshape(bsz, seq, d)


def kernel(x, mem, ln_g, ln_b, lru_w_in, lru_conv_w, lru_conv_b, lru_gate_w, lru_gate_b, lru_lambda, lru_w_out, rw_mu, rw_w_r, rw_w_k, rw_w_v, rw_w0, rw_w1, rw_w2, rw_a0, rw_a1, rw_a2, rw_g1, rw_g2, rw_k_k, rw_k_a, rw_r_k, rw_lnx_g, rw_lnx_b, rw_w_o, ca_w_qkv, ca_rel_bias, ca_w_o, mx_w_q, mx_w_kv, mx_w_o, mlp_w1, mlp_w2):
    h = x
    for i in range(DEPTH):
        kind, j = i % N_MIXERS, i // N_MIXERS
        g = lambda s: ln_g[i, s][None, :]
        b = lambda s: ln_b[i, s][None, :]
        if kind == 0:
            h = _rglru_sublayer(h, lru_w_in[j], lru_conv_w[j], lru_conv_b[j], lru_gate_w[j], lru_gate_b[j],
                                lru_lambda[j], lru_w_out[j], g(0), b(0))
        elif kind == 1:
            h = _rwkv_sublayer(h, rw_mu[j], rw_w_r[j], rw_w_k[j], rw_w_v[j], rw_w0[j], rw_w1[j], rw_w2[j],
                               rw_a0[j], rw_a1[j], rw_a2[j], rw_g1[j], rw_g2[j], rw_k_k[j], rw_k_a[j],
                               rw_r_k[j], rw_lnx_g[j], rw_lnx_b[j], rw_w_o[j], g(0), b(0))
        else:
            h = _attn_sublayer(h, ca_w_qkv[j], ca_rel_bias[j], ca_w_o[j], g(0), b(0))
        h = _mem_sublayer(h, mem, mx_w_q[i], mx_w_kv[i], mx_w_o[i], g(1), b(1))
        h = _mlp_sublayer(h, mlp_w1[i], mlp_w2[i], g(2), b(2))
    return h
```

```python
import functools

import jax
import jax.numpy as jnp
from jax import lax
from jax.experimental import pallas as pl
from jax.experimental.pallas import tpu as pltpu

F32 = jnp.float32
BF16 = jnp.bfloat16

D_MODEL = 1024
DEPTH = 4
CHUNK = 64
N_MIXERS = 3
DEEPNORM_ALPHA = (2 * DEPTH) ** 0.25
LN_EPS = 1e-5

D_RNN = 1344
LRU_BLOCKS = 16
LRU_BLOCK_SIZE = D_RNN // LRU_BLOCKS
CONV_WIDTH = 4
RG_LRU_C = 8.0
LANES = 128
SUBLANES = 8
D_RNN_PAD = -(-D_RNN // LANES) * LANES

RW_HEAD_SIZE = 64
RW_HEADS = D_MODEL // RW_HEAD_SIZE
RW_PAIRS = D_MODEL // LANES
RW_GN_EPS = 64e-5
RW_CHUNK = 64

ATT_HEADS = 16
ATT_HEAD_DIM = D_MODEL // ATT_HEADS
ATT_PAIRS = D_MODEL // LANES
BAND_CHUNKS = 9
MAX_REL = 2 * CHUNK
NEG_INF = -1e30
ATT_TQ = 256
ATT_WIN = ATT_TQ + (BAND_CHUNKS - 1) * CHUNK
ATT_TBL = ATT_WIN + (BAND_CHUNKS - 1) * CHUNK

MEM_HEADS = 4
MEM_HEAD_DIM = D_MODEL // MEM_HEADS
D_FF = 4 * D_MODEL

VMEM_LIMIT = 56 * 1024 * 1024


def _cparams(*sem):
    return pltpu.CompilerParams(dimension_semantics=sem, vmem_limit_bytes=VMEM_LIMIT)


def _residual_ln(hres, y, g, b):
    return _ln(DEEPNORM_ALPHA * hres + y, g, b)


def _ln(z, g, b):
    mu = jnp.mean(z, axis=-1, keepdims=True)
    zc = z - mu
    var = jnp.mean(zc * zc, axis=-1, keepdims=True)
    return zc * lax.rsqrt(var + LN_EPS) * g + b


def _gelu_tanh(x):
    return 0.5 * x * (1.0 + jnp.tanh(0.7978845608028654 * (x + 0.044715 * (x * x * x))))


def _softplus(x):
    return jnp.maximum(x, 0.0) + jnp.log1p(jnp.exp(-jnp.abs(x)))


def _sigmoid(x):
    return 0.5 * jnp.tanh(0.5 * x) + 0.5


def _split3(x):
    hi = x.astype(BF16)
    r1 = x - hi.astype(F32)
    mid = r1.astype(BF16)
    lo = (r1 - mid.astype(F32)).astype(BF16)
    return hi, mid, lo


def _proj_kernel(x_ref, w_ref, o_ref, *, chunk):
    x = x_ref[...].astype(BF16)
    n = w_ref.shape[1]
    for n0 in range(0, n, chunk):
        y = jnp.dot(x, w_ref[:, n0:n0 + chunk], preferred_element_type=F32)
        o_ref[:, n0:n0 + chunk] = y.astype(o_ref.dtype)


def _proj(x2d, w, out_dtype, tm=512, chunk=512):
    t, k = x2d.shape
    n = w.shape[1]
    tm = min(tm, t)
    return pl.pallas_call(
        functools.partial(_proj_kernel, chunk=chunk),
        grid=(t // tm,),
        in_specs=[pl.BlockSpec((tm, k), lambda i: (i, 0)),
                  pl.BlockSpec((k, n), lambda i: (0, 0))],
        out_specs=pl.BlockSpec((tm, n), lambda i: (i, 0)),
        out_shape=jax.ShapeDtypeStruct((t, n), out_dtype),
        compiler_params=_cparams("parallel"),
        name="proj",
    )(x2d, w)


def _proj_ln_kernel(y_ref, w_ref, h_ref, g_ref, b_ref, o_ref, *, sub):
    for r0 in range(0, y_ref.shape[0], sub):
        rows = slice(r0, r0 + sub)
        y = jnp.dot(y_ref[rows, :].astype(BF16), w_ref[...], preferred_element_type=F32)
        o_ref[rows, :] = _residual_ln(h_ref[rows, :], y, g_ref[...], b_ref[...])


def _proj_ln(y2d, w, h2d, g, b, tm=2048, sub=256):
    t, k = y2d.shape
    d = w.shape[1]
    tm = min(tm, t)
    return pl.pallas_call(
        functools.partial(_proj_ln_kernel, sub=min(sub, tm)),
        grid=(t // tm,),
        in_specs=[pl.BlockSpec((tm, k), lambda i: (i, 0)),
                  pl.BlockSpec((k, d), lambda i: (0, 0)),
                  pl.BlockSpec((tm, d), lambda i: (i, 0)),
                  pl.BlockSpec((1, d), lambda i: (0, 0)),
                  pl.BlockSpec((1, d), lambda i: (0, 0))],
        out_specs=pl.BlockSpec((tm, d), lambda i: (i, 0)),
        out_shape=jax.ShapeDtypeStruct((t, d), F32),
        compiler_params=_cparams("parallel"),
        name="proj_ln",
    )(y2d, w, h2d, g, b)


LRU_GATE_TILE = 256
LRU_GATE_WIN = 512


def _lru_gate_window_starts():
    starts = []
    for n0 in range(0, D_RNN_PAD, LRU_GATE_TILE):
        first_block = min(n0, D_RNN - 1) // LRU_BLOCK_SIZE
        last_block = min(n0 + LRU_GATE_TILE - 1, D_RNN - 1) // LRU_BLOCK_SIZE
        lo = (first_block * LRU_BLOCK_SIZE) // LANES * LANES
        lo = min(lo, D_RNN_PAD - LRU_GATE_WIN)
        assert (last_block + 1) * LRU_BLOCK_SIZE <= lo + LRU_GATE_WIN
        starts.append(lo)
    return starts


LRU_HIST = (CONV_WIDTH - 1) * SUBLANES


def _lru_kernel(h_ref, win_ref, cw_ref, cb_ref, gw_ref, gb_ref, lam_ref, wout_ref, g_ref, b_ref, o_ref,
                xbuf, hstate):
    @pl.when(pl.program_id(0) == 0)
    def _():
        xbuf[:, 0:LRU_HIST, :] = jnp.zeros((xbuf.shape[0], LRU_HIST, xbuf.shape[2]), F32)
        hstate[...] = jnp.zeros(hstate.shape, F32)

    ng = h_ref.shape[0] // SUBLANES
    fronts = [_lru_front(gi, h_ref, win_ref, cw_ref, cb_ref, gw_ref, xbuf) for gi in range(ng)]
    for gi in range(ng):
        _lru_back(gi, *fronts[gi], gb_ref, lam_ref, wout_ref, g_ref, b_ref, o_ref, hstate)


def _lru_front(gi, h_ref, win_ref, cw_ref, cb_ref, gw_ref, xbuf):
    tt, d = h_ref.shape[1], h_ref.shape[2]
    c = cw_ref.shape[1]
    ts = tt * SUBLANES
    hres = h_ref[gi * SUBLANES:(gi + 1) * SUBLANES]
    ht = pltpu.einshape("btd->tbd", hres).reshape(ts, d)
    u = jnp.dot(ht.astype(BF16), win_ref[...], preferred_element_type=F32)
    gate_branch = _gelu_tanh(u[:, :c])
    x = u[:, c:]
    xbuf[gi, LRU_HIST:LRU_HIST + ts, :] = x
    xr = cb_ref[...] + cw_ref[3:4, :] * x
    for k in range(CONV_WIDTH - 1):
        xr = xr + cw_ref[k:k + 1, :] * xbuf[gi, k * SUBLANES:k * SUBLANES + ts, :]
    xbuf[gi, 0:LRU_HIST, :] = xbuf[gi, ts:ts + LRU_HIST, :]

    xb = xr.astype(BF16)
    tiles = [jnp.dot(xb[:, lo:lo + LRU_GATE_WIN], gw_ref[n], preferred_element_type=F32)
             for n, lo in enumerate(_lru_gate_window_starts())]
    r_pre = jnp.concatenate([t[:, :LRU_GATE_TILE] for t in tiles], axis=1)[:, :c]
    i_pre = jnp.concatenate([t[:, LRU_GATE_TILE:] for t in tiles], axis=1)[:, :c]
    return hres, gate_branch, xr, r_pre, i_pre


def _lru_back(gi, hres, gate_branch, xr, r_pre, i_pre, gb_ref, lam_ref, wout_ref, g_ref, b_ref, o_ref, hstate):
    ts, c = xr.shape
    tt = ts // SUBLANES
    r_gate = _sigmoid(r_pre + gb_ref[0:1, :])
    i_gate = _sigmoid(i_pre + gb_ref[1:2, :])
    log_a = (-RG_LRU_C) * r_gate * _softplus(-lam_ref[...])
    a = jnp.exp(log_a)
    bt = jnp.sqrt(-jnp.tanh(log_a) * (a * a + 1.0)) * (i_gate * xr)

    h = hstate[gi]
    steps = []
    for t in range(tt):
        slab = slice(t * SUBLANES, (t + 1) * SUBLANES)
        h = a[slab, :] * h + bt[slab, :]
        steps.append(h)
    hstate[gi] = h
    hs = jnp.concatenate(steps, axis=0)

    y = jnp.dot((hs * gate_branch).astype(BF16), wout_ref[...], preferred_element_type=F32)
    yb = pltpu.einshape("tbd->btd", y.reshape(tt, SUBLANES, y.shape[1]))
    o_ref[gi * SUBLANES:(gi + 1) * SUBLANES] = _residual_ln(hres, yb, g_ref[...], b_ref[...])


def _lru(h, win, cw, cb, gw, gb, lam, wout, g, b, tt=32):
    bsz, seq, d = h.shape
    c = cw.shape[1]
    assert bsz % SUBLANES == 0
    tt = min(tt, seq)
    ng = bsz // SUBLANES
    resident = lambda w: pl.BlockSpec(w.shape, lambda j: (0,) * w.ndim, pipeline_mode=pl.Buffered(1))
    return pl.pallas_call(
        _lru_kernel,
        grid=(seq // tt,),
        in_specs=[pl.BlockSpec((bsz, tt, d), lambda j: (0, j, 0))]
                 + [resident(w) for w in (win, cw, cb, gw, gb, lam, wout, g, b)],
        out_specs=pl.BlockSpec((bsz, tt, d), lambda j: (0, j, 0)),
        out_shape=jax.ShapeDtypeStruct((bsz, seq, d), F32),
        scratch_shapes=[pltpu.VMEM((ng, LRU_HIST + tt * SUBLANES, c), F32), pltpu.VMEM((ng, SUBLANES, c), F32)],
        compiler_params=_cparams("arbitrary"),
        name="lru",
    )(h, win, cw, cb, gw, gb, lam, wout, g, b)


def _pad_last(x, n):
    return jnp.pad(x, [(0, 0)] * (x.ndim - 1) + [(0, n - x.shape[-1])])


def _rglru_sublayer(h, w_in, conv_w, conv_b, gate_w, gate_b, lam, w_out, g, b):
    bsz, seq, d = h.shape
    c = D_RNN_PAD
    w_cat = jnp.concatenate([_pad_last(w_in[:, :D_RNN], c), _pad_last(w_in[:, D_RNN:], c)], axis=1).astype(BF16)
    eye = jnp.eye(LRU_BLOCKS, dtype=F32)
    dense = jnp.einsum('gncd,nm->gncmd', gate_w, eye).reshape(2, D_RNN, D_RNN)
    starts = _lru_gate_window_starts()
    dense = jnp.pad(dense, ((0, 0), (0, c - D_RNN), (0, len(starts) * LRU_GATE_TILE - D_RNN)))
    gw = jnp.stack([jnp.concatenate([dense[gi, lo:lo + LRU_GATE_WIN, n * LRU_GATE_TILE:(n + 1) * LRU_GATE_TILE]
                                     for gi in range(2)], axis=1) for n, lo in enumerate(starts)]).astype(BF16)
    wout = jnp.pad(w_out, ((0, c - D_RNN), (0, 0))).astype(BF16)
    return _lru(h, w_cat, _pad_last(conv_w, c), _pad_last(conv_b[None, :], c), gw,
                _pad_last(gate_b, c), _pad_last(lam[None, :], c), wout, g, b)


def _rwkv_in_kernel(x_ref, xp_ref, mu_ref, wr_ref, wk_ref, wv_ref, w0_ref, w1_ref, w2_ref,
                    a0_ref, a1_ref, a2_ref, g1_ref, g2_ref,
                    r_ref, k_ref, v_ref, ld_ref, a_ref, g_ref, *, sub):
    i = pl.program_id(1)
    x_all = x_ref[0]
    row = lax.broadcasted_iota(jnp.int32, x_all.shape, 0)
    prev = jnp.where(i > 0, xp_ref[0, SUBLANES - 1:SUBLANES, :], 0.0)
    xx_all = jnp.where(row == 0, prev, pltpu.roll(x_all, 1, 0)) - x_all
    dot = lambda p, q: jnp.dot(p, q, preferred_element_type=F32)
    for r0 in range(0, x_all.shape[0], sub):
        rows = slice(r0, r0 + sub)
        x, xx = x_all[rows, :], xx_all[rows, :]
        mix = lambda c: (x + xx * mu_ref[c:c + 1, :]).astype(BF16)
        r_ref[0, rows, :] = dot(mix(0), wr_ref[...]).astype(r_ref.dtype)
        k_ref[0, rows, :] = dot(mix(2), wk_ref[...]).astype(k_ref.dtype)
        v_ref[0, rows, :] = dot(mix(3), wv_ref[...]).astype(v_ref.dtype)
        wl = w0_ref[...] + dot(jnp.tanh(dot(mix(1), w1_ref[...])).astype(BF16), w2_ref[...])
        w_log = -_softplus(-wl) - 0.5
        ld_ref[0, rows, :] = -jnp.exp(w_log)
        a_ref[0, rows, :] = _sigmoid(a0_ref[...] + dot(dot(mix(4), a1_ref[...]).astype(BF16), a2_ref[...]))
        g_ref[0, rows, :] = dot(_sigmoid(dot(mix(5), g1_ref[...])).astype(BF16), g2_ref[...]).astype(g_ref.dtype)


def _rwkv_in(h, mu, wr, wk, wv, w0, w1, w2, a0, a1, a2, g1, g2, tm=512, sub=256):
    bsz, seq, d = h.shape
    tm = min(tm, seq)
    row = lambda b, i: (b, i, 0)
    fix = lambda b, i: (0, 0)
    prev = lambda b, i: (b, jnp.maximum(i * (tm // SUBLANES) - 1, 0), 0)
    ws = [mu, wr, wk, wv, w0, w1, w2, a0, a1, a2, g1, g2]
    return pl.pallas_call(
        functools.partial(_rwkv_in_kernel, sub=min(sub, tm)),
        grid=(bsz, seq // tm),
        in_specs=[pl.BlockSpec((1, tm, d), row), pl.BlockSpec((1, SUBLANES, d), prev)]
                 + [pl.BlockSpec(w.shape, fix) for w in ws],
        out_specs=[pl.BlockSpec((1, tm, d), row)] * 6,
        out_shape=[jax.ShapeDtypeStruct((bsz, seq, d), dt) for dt in (BF16, BF16, BF16, F32, F32, BF16)],
        compiler_params=_cparams("parallel", "parallel"),
        name="rwkv_in",
    )(h, h, *ws)


def _bdot(a, b, dims):
    return jnp.einsum(dims, a, b, preferred_element_type=F32)


def _rwkv_rec_kernel(r_ref, k_ref, v_ref, ld_ref, a_ref, g_ref, kk_ref, ka_ref, rk_ref, lg_ref, lb_ref,
                     o_ref, state):
    @pl.when(pl.program_id(1) == 0)
    def _():
        state[...] = jnp.zeros(state.shape, F32)

    s_val = state[...]
    for c in range(r_ref.shape[1] // RW_CHUNK):
        s_val = _rwkv_chunk(slice(c * RW_CHUNK, (c + 1) * RW_CHUNK), s_val, r_ref, k_ref, v_ref, ld_ref, a_ref,
                            g_ref, kk_ref, ka_ref, rk_ref, lg_ref, lb_ref, o_ref)
    state[...] = s_val


def _rwkv_chunk(rows, s_old, r_ref, k_ref, v_ref, ld_ref, a_ref, g_ref, kk_ref, ka_ref, rk_ref, lg_ref, lb_ref,
                o_ref):
    L = RW_CHUNK
    P = RW_PAIRS
    N = RW_HEAD_SIZE
    pairs = lambda ref: jnp.stack([ref[0, rows, LANES * p:LANES * (p + 1)] for p in range(P)]).astype(F32)
    vec = lambda ref: jnp.stack([ref[:, LANES * p:LANES * (p + 1)] for p in range(P)])

    lane = lax.broadcasted_iota(jnp.int32, (1, 1, LANES), 2)
    head0 = lane < N
    ri = lax.broadcasted_iota(jnp.int32, (LANES, LANES), 0)
    ci = lax.broadcasted_iota(jnp.int32, (LANES, LANES), 1)
    seg = ((ri // N) == (ci // N)).astype(BF16)

    def segsum(x):
        flat = x.reshape(P * L, LANES)
        hi = flat.astype(BF16)
        lo = (flat - hi.astype(F32)).astype(BF16)
        out = jnp.dot(hi, seg, preferred_element_type=F32) + jnp.dot(lo, seg, preferred_element_type=F32)
        return out.reshape(P, L, LANES)

    r = pairs(r_ref)
    k = pairs(k_ref)
    v = pairs(v_ref)
    a = pairs(a_ref)

    kk = k * vec(kk_ref)
    kk = kk / jnp.maximum(jnp.sqrt(segsum(kk * kk)), 1e-12)
    k = k * (1.0 + (a - 1.0) * vec(ka_ref))
    aa = -kk
    bb = kk * a

    ld_full = ld_ref[0, rows, :]
    tr = lax.broadcasted_iota(jnp.int32, (L, L), 0)
    tc = lax.broadcasted_iota(jnp.int32, (L, L), 1)
    tri = (tc <= tr).astype(BF16)
    cum_full = sum(jnp.dot(tri, t, preferred_element_type=F32) for t in _split3(ld_full))
    cum = jnp.stack([cum_full[:, LANES * p:LANES * (p + 1)] for p in range(P)])
    ld = pairs(ld_ref)
    gam = jnp.exp(cum)
    gam_prev = jnp.exp(cum - ld)
    cum_end = cum[:, L - 1:L, :]
    r_t = r * gam
    a_t = aa * gam_prev
    k_t = k * jnp.exp(-cum)
    b_t = bb * jnp.exp(-cum)
    k_e = k * jnp.exp(cum_end - cum)
    b_e = bb * jnp.exp(cum_end - cum)

    def stacked(x):
        return jnp.concatenate([jnp.where(head0, x, 0.0), jnp.where(head0, 0.0, x)], axis=1)

    ar = jnp.concatenate([stacked(a_t), stacked(r_t)], axis=1).astype(BF16)
    kb = jnp.concatenate([stacked(k_t), stacked(b_t)], axis=1).astype(BF16)
    v_s = stacked(v)

    s4 = _bdot(ar, kb, 'pik,pjk->pij')
    ah = _bdot(ar, s_old.astype(BF16), 'pik,plk->pil')
    M = 2 * L
    si = lax.broadcasted_iota(jnp.int32, (1, M, M), 1)
    sj = lax.broadcasted_iota(jnp.int32, (1, M, M), 2)
    same = (si // L) == (sj // L)
    strict = same & ((sj % L) < (si % L))
    incl = same & ((sj % L) <= (si % L))
    a_k = jnp.where(strict, s4[:, :M, :M], 0.0)
    a_b = jnp.where(strict, s4[:, :M, M:], 0.0)
    r_k = jnp.where(incl, s4[:, M:, :M], 0.0)
    r_b = jnp.where(incl, s4[:, M:, M:], 0.0)

    eye = (si == sj).astype(F32)
    lower1 = (si // 2 == sj // 2) & (si % 2 == 1) & (sj % 2 == 0)
    T = eye + jnp.where(lower1, a_b, 0.0)
    ab16 = a_b.astype(BF16)
    s = 2
    hi_ = lax.broadcasted_iota(jnp.int32, (1, M // 2, M), 1)
    hj_ = lax.broadcasted_iota(jnp.int32, (1, M // 2, M), 2)
    while s < L:
        tb = T.astype(BF16)
        if s < SUBLANES:
            cm = (si // (2 * s) == sj // (2 * s)) & (si % (2 * s) >= s) & (sj % (2 * s) < s)
            tat = _bdot(_bdot(tb, ab16, 'pij,pjk->pik').astype(BF16), tb, 'pij,pjk->pik')
            T = T + jnp.where(cm, tat, 0.0)
        else:
            nblk = M // (2 * s)
            low = jnp.concatenate([T[:, (2 * k + 1) * s:(2 * k + 2) * s, :] for k in range(nblk)], axis=1)
            upd = _bdot(_bdot(low.astype(BF16), ab16, 'pij,pjk->pik').astype(BF16), tb, 'pij,pjk->pik')
            upd = jnp.where((hj_ // (2 * s) == hi_ // s) & (hj_ % (2 * s) < s), upd, 0.0)
            zero = jnp.zeros((P, s, M), F32)
            T = T + jnp.concatenate([blk for k in range(nblk) for blk in (zero, upd[:, k * s:(k + 1) * s, :])],
                                    axis=1)
        s *= 2

    x_s = ah[:, :M, :] + _bdot(a_k.astype(BF16), v_s.astype(BF16), 'pij,pjl->pil')
    u_s = _bdot(T.astype(BF16), x_s.astype(BF16), 'pij,pjl->pil')
    vu = jnp.concatenate([v_s, u_s], axis=1).astype(BF16)
    rkb = jnp.concatenate([r_k, r_b], axis=2).astype(BF16)
    y_s = ah[:, M:, :] + _bdot(rkb, vu, 'pij,pjl->pil')
    y = y_s[:, :L, :] + y_s[:, L:, :]

    kb_e = jnp.concatenate([stacked(k_e), stacked(b_e)], axis=1).astype(BF16)
    s_new = s_old * jnp.exp(cum_end) + _bdot(vu, kb_e, 'pti,ptj->pij')

    mean = segsum(y) * (1.0 / N)
    yc = y - mean
    var = segsum(yc * yc) * (1.0 / N)
    yn = yc * lax.rsqrt(var + RW_GN_EPS) * vec(lg_ref) + vec(lb_ref)
    bonus = segsum(r * k * vec(rk_ref)) * v
    out = (yn + bonus) * pairs(g_ref)
    for p in range(P):
        o_ref[0, rows, LANES * p:LANES * (p + 1)] = out[p].astype(o_ref.dtype)
    return s_new


def _rwkv_rec(r, k, v, ld, a, g, k_k, k_a, r_k, lnx_g, lnx_b, chunks_per_step=4):
    bsz, seq, d = r.shape
    L = RW_CHUNK * min(chunks_per_step, seq // RW_CHUNK)
    row = lambda b, j: (b, j, 0)
    fix = lambda b, j: (0, 0)
    return pl.pallas_call(
        _rwkv_rec_kernel,
        grid=(bsz, seq // L),
        in_specs=[pl.BlockSpec((1, L, d), row)] * 6 + [pl.BlockSpec((1, d), fix)] * 5,
        out_specs=pl.BlockSpec((1, L, d), row),
        out_shape=jax.ShapeDtypeStruct((bsz, seq, d), BF16),
        scratch_shapes=[pltpu.VMEM((RW_PAIRS, LANES, LANES), F32)],
        compiler_params=_cparams("parallel", "arbitrary"),
        name="rwkv_rec",
    )(r, k, v, ld, a, g, k_k, k_a, r_k, lnx_g, lnx_b)


def _rwkv_sublayer(h, mu, w_r, w_k, w_v, w0, w1, w2, a0, a1, a2, g1, g2, k_k, k_a, r_k,
                   lnx_g, lnx_b, w_o, g, b):
    bsz, seq, d = h.shape
    bf = lambda w: w.astype(BF16)
    row = lambda w: w.reshape(1, d)
    mu8 = jnp.pad(mu, ((0, SUBLANES - mu.shape[0]), (0, 0)))
    r, k, v, ld, a, gg = _rwkv_in(h, mu8, bf(w_r), bf(w_k), bf(w_v), row(w0), bf(w1), bf(w2),
                                  row(a0), bf(a1), bf(a2), bf(g1), bf(g2))
    y = _rwkv_rec(r, k, v, ld, a, gg, row(k_k), row(k_a), row(r_k), row(lnx_g), row(lnx_b))
    return _proj_ln(y.reshape(bsz * seq, d), bf(w_o), h.reshape(bsz * seq, d), g, b).reshape(bsz, seq, d)


def _chunk_attn_kernel(q_ref, k_ref, v_ref, tbl_ref, o_ref):
    seq = q_ref.shape[1]
    blocks_back = (ATT_WIN - ATT_TQ) // ATT_TQ
    lane = lax.broadcasted_iota(jnp.int32, (ATT_TQ, LANES), 1)
    nq = seq // ATT_TQ

    def window(qi):
        back = min(qi, blocks_back)
        ws = (qi - back) * ATT_TQ
        width = (back + 1) * ATT_TQ
        off = (blocks_back - back) * ATT_TQ
        return ws, width, off

    def scores(qi):
        ws, width, _ = window(qi)
        kw = k_ref[0, ws:ws + width, :]
        q = q_ref[0, qi * ATT_TQ:(qi + 1) * ATT_TQ, :]
        out = []
        for hh in range(2):
            qh = jnp.where((lane // ATT_HEAD_DIM) == hh, q, jnp.zeros_like(q))
            out.append(lax.dot_general(qh, kw, (((1,), (1,)), ((), ())), preferred_element_type=F32))
        return out

    s_next = scores(0)
    for qi in range(nq):
        s_cur = s_next
        if qi + 1 < nq:
            s_next = scores(qi + 1)
        ws, width, off = window(qi)
        vw = v_ref[0, ws:ws + width, :]
        vlane = lax.broadcasted_iota(jnp.int32, vw.shape, 1)
        pv = []
        for hh in range(2):
            s = s_cur[hh] + tbl_ref[0, hh, :, off:off + width]
            m = jnp.max(s, axis=-1, keepdims=True)
            e = jnp.exp((s - m).astype(BF16))
            vh = jnp.where((vlane // ATT_HEAD_DIM) == hh, vw, jnp.ones_like(vw))
            pv.append(jnp.dot(e, vh, preferred_element_type=F32))
        num = jnp.where(lane < ATT_HEAD_DIM, pv[0], pv[1])
        den = jnp.where(lane < ATT_HEAD_DIM, pltpu.roll(pv[0], ATT_HEAD_DIM, 1), pltpu.roll(pv[1], ATT_HEAD_DIM, 1))
        o_ref[0, qi * ATT_TQ:(qi + 1) * ATT_TQ, :] = (num / den).astype(o_ref.dtype)


def _chunk_attn(qkv, tbl):
    bsz, seq, _ = qkv.shape
    return pl.pallas_call(
        _chunk_attn_kernel,
        grid=(bsz, ATT_PAIRS),
        in_specs=[pl.BlockSpec((1, seq, LANES), lambda b, p: (b, 0, p)),
                  pl.BlockSpec((1, seq, LANES), lambda b, p: (b, 0, ATT_PAIRS + p)),
                  pl.BlockSpec((1, seq, LANES), lambda b, p: (b, 0, 2 * ATT_PAIRS + p)),
                  pl.BlockSpec((1, 2, ATT_TQ, ATT_TBL), lambda b, p: (p, 0, 0, 0))],
        out_specs=pl.BlockSpec((1, seq, LANES), lambda b, p: (b, 0, p)),
        out_shape=jax.ShapeDtypeStruct((bsz, seq, D_MODEL), BF16),
        compiler_params=_cparams("parallel", "parallel"),
        name="chunk_attn",
    )(qkv, qkv, qkv, tbl)


ATT_PERIOD = ATT_TBL + ATT_TQ


def _attn_table_kernel(vec_ref, o_ref):
    p = pl.program_id(0)
    ic = lax.broadcasted_iota(jnp.int32, (ATT_TQ, ATT_TBL), 0) // CHUNK
    mc = lax.broadcasted_iota(jnp.int32, (ATT_TQ, ATT_TBL), 1) // CHUNK
    valid = (mc >= ic) & (mc <= ic + BAND_CHUNKS - 1)
    for hh in range(2):
        row = vec_ref[pl.ds(2 * p + hh, 1), :]
        skew = pltpu.roll(jnp.broadcast_to(row, (ATT_TQ, ATT_PERIOD)), 0, 1, stride=1, stride_axis=0)
        o_ref[0, hh] = jnp.where(valid, skew[:, :ATT_TBL], NEG_INF)


def _attn_bias_table(rel_bias):
    left = ATT_WIN - ATT_TQ
    j = jnp.arange(ATT_PERIOD)
    diff = jnp.where(j < ATT_TBL, j, j - ATT_PERIOD)
    vec = rel_bias[:, jnp.clip(left - diff, -MAX_REL, MAX_REL) + MAX_REL].astype(F32)
    return pl.pallas_call(
        _attn_table_kernel,
        grid=(ATT_PAIRS,),
        in_specs=[pl.BlockSpec(vec.shape, lambda p: (0, 0))],
        out_specs=pl.BlockSpec((1, 2, ATT_TQ, ATT_TBL), lambda p: (p, 0, 0, 0)),
        out_shape=jax.ShapeDtypeStruct((ATT_PAIRS, 2, ATT_TQ, ATT_TBL), F32),
        compiler_params=_cparams("parallel"),
        name="attn_table",
    )(vec)


def _attn_sublayer(h, w_qkv, rel_bias, w_o, g, b):
    bsz, seq, d = h.shape
    assert seq % ATT_TQ == 0 and seq >= ATT_WIN
    scale = jnp.concatenate([jnp.full((d,), ATT_HEAD_DIM ** -0.5, F32), jnp.ones((2 * d,), F32)])
    qkv = _proj(h.reshape(bsz * seq, d), (w_qkv * scale).astype(BF16), BF16)
    o = _chunk_attn(qkv.reshape(bsz, seq, 3 * d), _attn_bias_table(rel_bias))
    return _proj_ln(o.reshape(bsz * seq, d), w_o.astype(BF16), h.reshape(bsz * seq, d), g, b).reshape(bsz, seq, d)


def _mem_attn_kernel(h_ref, kv_ref, wq_ref, wo_ref, o_ref, *, sub):
    tiles = [slice(r0, r0 + sub) for r0 in range(0, h_ref.shape[1], sub)]

    def front(rows):
        q = jnp.dot(h_ref[0, rows, :].astype(BF16), wq_ref[...], preferred_element_type=F32).astype(BF16)
        return [lax.dot_general(q[:, hh * MEM_HEAD_DIM:(hh + 1) * MEM_HEAD_DIM],
                                kv_ref[0, :, hh * MEM_HEAD_DIM:(hh + 1) * MEM_HEAD_DIM],
                                (((1,), (1,)), ((), ())), preferred_element_type=F32) for hh in range(MEM_HEADS)]

    def back(rows, scores):
        outs = []
        for hh, s in enumerate(scores):
            vh = kv_ref[0, :, D_MODEL + hh * MEM_HEAD_DIM:D_MODEL + (hh + 1) * MEM_HEAD_DIM]
            m = jnp.max(s, axis=-1, keepdims=True)
            e = jnp.exp(s - m)
            l = jnp.sum(e, axis=-1, keepdims=True)
            outs.append((jnp.dot(e.astype(BF16), vh, preferred_element_type=F32) / l).astype(BF16))
        y = jnp.dot(jnp.concatenate(outs, axis=1), wo_ref[...], preferred_element_type=F32)
        o_ref[0, rows, :] = DEEPNORM_ALPHA * h_ref[0, rows, :] + y

    nxt = front(tiles[0])
    for i, rows in enumerate(tiles):
        cur = nxt
        if i + 1 < len(tiles):
            nxt = front(tiles[i + 1])
        back(rows, cur)


def _mem_sublayer(h, mem, w_q, w_kv, w_o, tm=1024, sub=256):
    bsz, seq, d = h.shape
    nm = mem.shape[1]
    tm = min(tm, seq)
    kv = _proj(mem.reshape(bsz * nm, d), w_kv.astype(BF16), BF16).reshape(bsz, nm, 2 * d)
    wq = (w_q * (MEM_HEAD_DIM ** -0.5)).astype(BF16)
    row = lambda b_, i: (b_, i, 0)
    fix = lambda b_, i: (0, 0)
    return pl.pallas_call(
        functools.partial(_mem_attn_kernel, sub=min(sub, tm)),
        grid=(bsz, seq // tm),
        in_specs=[pl.BlockSpec((1, tm, d), row),
                  pl.BlockSpec((1, nm, 2 * d), lambda b_, i: (b_, 0, 0)),
                  pl.BlockSpec((d, d), fix, pipeline_mode=pl.Buffered(1)),
                  pl.BlockSpec((d, d), fix, pipeline_mode=pl.Buffered(1))],
        out_specs=pl.BlockSpec((1, tm, d), row),
        out_shape=jax.ShapeDtypeStruct((bsz, seq, d), F32),
        compiler_params=_cparams("parallel", "parallel"),
        name="mem_attn",
    )(h, kv, wq, w_o.astype(BF16))


def _mlp_kernel(z_ref, gin_ref, bin_ref, w1_ref, w2_ref, g_ref, b_ref, o_ref, *, sub):
    for r0 in range(0, z_ref.shape[0], sub):
        rows = slice(r0, r0 + sub)
        h = _ln(z_ref[rows, :], gin_ref[...], bin_ref[...])
        u = jnp.dot(h.astype(BF16), w1_ref[...], preferred_element_type=F32)
        u = jnp.maximum(u, 0.0)
        y = jnp.dot((u * u).astype(BF16), w2_ref[...], preferred_element_type=F32)
        o_ref[rows, :] = _residual_ln(h, y, g_ref[...], b_ref[...])


def _mlp_sublayer(z, g_in, b_in, w1, w2, g, b, tm=1024, sub=256):
    bsz, seq, d = z.shape
    t = bsz * seq
    tm = min(tm, t)
    ff = w1.shape[1]
    resident = pl.Buffered(1)
    out = pl.pallas_call(
        functools.partial(_mlp_kernel, sub=min(sub, tm)),
        grid=(t // tm,),
        in_specs=[pl.BlockSpec((tm, d), lambda i: (i, 0)),
                  pl.BlockSpec((1, d), lambda i: (0, 0)),
                  pl.BlockSpec((1, d), lambda i: (0, 0)),
                  pl.BlockSpec((d, ff), lambda i: (0, 0), pipeline_mode=resident),
                  pl.BlockSpec((ff, d), lambda i: (0, 0), pipeline_mode=resident),
                  pl.BlockSpec((1, d), lambda i: (0, 0)),
                  pl.BlockSpec((1, d), lambda i: (0, 0))],
        out_specs=pl.BlockSpec((tm, d), lambda i: (i, 0)),
        out_shape=jax.ShapeDtypeStruct((t, d), F32),
        compiler_params=_cparams("parallel"),
        name="mlp",
    )(z.reshape(t, d), g_in, b_in, w1.astype(BF16), w2.astype(BF16), g, b)
    return out.reshape(bsz, seq, d)


def kernel(x, mem, ln_g, ln_b, lru_w_in, lru_conv_w, lru_conv_b, lru_gate_w, lru_gate_b, lru_lambda, lru_w_out, rw_mu, rw_w_r, rw_w_k, rw_w_v, rw_w0, rw_w1, rw_w2, rw_a0, rw_a1, rw_a2, rw_g1, rw_g2, rw_k_k, rw_k_a, rw_r_k, rw_lnx_g, rw_lnx_b, rw_w_o, ca_w_qkv, ca_rel_bias, ca_w_o, mx_w_q, mx_w_kv, mx_w_o, mlp_w1, mlp_w2):
    h = x
    for i in range(DEPTH):
        kind, j = i % N_MIXERS, i // N_MIXERS
        g = lambda s: ln_g[i, s][None, :]
        b = lambda s: ln_b[i, s][None, :]
        if kind == 0:
            h = _rglru_sublayer(h, lru_w_in[j], lru_conv_w[j], lru_conv_b[j], lru_gate_w[j], lru_gate_b[j],
                                lru_lambda[j], lru_w_out[j], g(0), b(0))
        elif kind == 1:
            h = _rwkv_sublayer(h, rw_mu[j], rw_w_r[j], rw_w_k[j], rw_w_v[j], rw_w0[j], rw_w1[j], rw_w2[j],
                               rw_a0[j], rw_a1[j], rw_a2[j], rw_g1[j], rw_g2[j], rw_k_k[j], rw_k_a[j],
                               rw_r_k[j], rw_lnx_g[j], rw_lnx_b[j], rw_w_o[j], g(0), b(0))
        else:
            h = _attn_sublayer(h, ca_w_qkv[j], ca_rel_bias[j], ca_w_o[j], g(0), b(0))
        z = _mem_sublayer(h, mem, mx_w_q[i], mx_w_kv[i], mx_w_o[i])
        h = _mlp_sublayer(z, g(1), b(1), mlp_w1[i], mlp_w2[i], g(2), b(2))
    return h
```

```python
import functools

import jax
import jax.numpy as jnp
from jax import lax
from jax.experimental import pallas as pl
from jax.experimental.pallas import tpu as pltpu

F32 = jnp.float32
BF16 = jnp.bfloat16

D_MODEL = 1024
DEPTH = 4
CHUNK = 64
N_MIXERS = 3
DEEPNORM_ALPHA = (2 * DEPTH) ** 0.25
LN_EPS = 1e-5

D_RNN = 1344
LRU_BLOCKS = 16
LRU_BLOCK_SIZE = D_RNN // LRU_BLOCKS
CONV_WIDTH = 4
RG_LRU_C = 8.0
LANES = 128
SUBLANES = 8
D_RNN_PAD = -(-D_RNN // LANES) * LANES

RW_HEAD_SIZE = 64
RW_HEADS = D_MODEL // RW_HEAD_SIZE
RW_PAIRS = D_MODEL // LANES
RW_GN_EPS = 64e-5
RW_CHUNK = 64

ATT_HEADS = 16
ATT_HEAD_DIM = D_MODEL // ATT_HEADS
ATT_PAIRS = D_MODEL // LANES
BAND_CHUNKS = 9
MAX_REL = 2 * CHUNK
NEG_INF = -1e30
ATT_TQ = 256
ATT_WIN = ATT_TQ + (BAND_CHUNKS - 1) * CHUNK
ATT_TBL = ATT_WIN + (BAND_CHUNKS - 1) * CHUNK

MEM_HEADS = 4
MEM_HEAD_DIM = D_MODEL // MEM_HEADS
D_FF = 4 * D_MODEL

VMEM_LIMIT = 56 * 1024 * 1024


def _cparams(*sem):
    return pltpu.CompilerParams(dimension_semantics=sem, vmem_limit_bytes=VMEM_LIMIT)


def _residual_ln(hres, y, g, b):
    return _ln(DEEPNORM_ALPHA * hres + y, g, b)


def _ln(z, g, b):
    mu = jnp.mean(z, axis=-1, keepdims=True)
    zc = z - mu
    var = jnp.mean(zc * zc, axis=-1, keepdims=True)
    return zc * lax.rsqrt(var + LN_EPS) * g + b


def _gelu_tanh(x):
    return 0.5 * x * (1.0 + jnp.tanh(0.7978845608028654 * (x + 0.044715 * (x * x * x))))


def _softplus(x):
    return jnp.maximum(x, 0.0) + jnp.log1p(jnp.exp(-jnp.abs(x)))


def _sigmoid(x):
    return 0.5 * jnp.tanh(0.5 * x) + 0.5


def _split2(x):
    hi = x.astype(BF16)
    lo = (x - hi.astype(F32)).astype(BF16)
    return hi, lo


def _proj_kernel(x_ref, w_ref, o_ref, *, chunk):
    x = x_ref[...].astype(BF16)
    n = w_ref.shape[1]
    for n0 in range(0, n, chunk):
        y = jnp.dot(x, w_ref[:, n0:n0 + chunk], preferred_element_type=F32)
        o_ref[:, n0:n0 + chunk] = y.astype(o_ref.dtype)


def _proj(x2d, w, out_dtype, tm=512, chunk=512):
    t, k = x2d.shape
    n = w.shape[1]
    tm = min(tm, t)
    return pl.pallas_call(
        functools.partial(_proj_kernel, chunk=chunk),
        grid=(t // tm,),
        in_specs=[pl.BlockSpec((tm, k), lambda i: (i, 0)),
                  pl.BlockSpec((k, n), lambda i: (0, 0))],
        out_specs=pl.BlockSpec((tm, n), lambda i: (i, 0)),
        out_shape=jax.ShapeDtypeStruct((t, n), out_dtype),
        compiler_params=_cparams("parallel"),
        name="proj",
    )(x2d, w)


def _proj_ln_kernel(y_ref, w_ref, h_ref, o_ref, *, sub):
    for r0 in range(0, y_ref.shape[0], sub):
        rows = slice(r0, r0 + sub)
        y = jnp.dot(y_ref[rows, :].astype(BF16), w_ref[...], preferred_element_type=F32)
        o_ref[rows, :] = DEEPNORM_ALPHA * h_ref[rows, :] + y


def _proj_ln(y2d, w, h2d, tm=2048, sub=256):
    t, k = y2d.shape
    d = w.shape[1]
    tm = min(tm, t)
    return pl.pallas_call(
        functools.partial(_proj_ln_kernel, sub=min(sub, tm)),
        grid=(t // tm,),
        in_specs=[pl.BlockSpec((tm, k), lambda i: (i, 0)),
                  pl.BlockSpec((k, d), lambda i: (0, 0)),
                  pl.BlockSpec((tm, d), lambda i: (i, 0))],
        out_specs=pl.BlockSpec((tm, d), lambda i: (i, 0)),
        out_shape=jax.ShapeDtypeStruct((t, d), F32),
        compiler_params=_cparams("parallel"),
        name="proj_ln",
    )(y2d, w, h2d)


LRU_GATE_TILE = 256
LRU_GATE_WIN = 512


def _lru_gate_window_starts():
    starts = []
    for n0 in range(0, D_RNN_PAD, LRU_GATE_TILE):
        first_block = min(n0, D_RNN - 1) // LRU_BLOCK_SIZE
        last_block = min(n0 + LRU_GATE_TILE - 1, D_RNN - 1) // LRU_BLOCK_SIZE
        lo = (first_block * LRU_BLOCK_SIZE) // LANES * LANES
        lo = min(lo, D_RNN_PAD - LRU_GATE_WIN)
        assert (last_block + 1) * LRU_BLOCK_SIZE <= lo + LRU_GATE_WIN
        starts.append(lo)
    return starts


LRU_HIST = (CONV_WIDTH - 1) * SUBLANES


def _lru_kernel(h_ref, win_ref, cw_ref, cb_ref, gw_ref, gb_ref, lam_ref, wout_ref, o_ref, xbuf, hstate):
    @pl.when(pl.program_id(0) == 0)
    def _():
        xbuf[:, 0:LRU_HIST, :] = jnp.zeros((xbuf.shape[0], LRU_HIST, xbuf.shape[2]), F32)
        hstate[...] = jnp.zeros(hstate.shape, F32)

    ng = h_ref.shape[0] // SUBLANES
    fronts = [_lru_front(gi, h_ref, win_ref, cw_ref, cb_ref, gw_ref, xbuf) for gi in range(ng)]
    for gi in range(ng):
        _lru_back(gi, *fronts[gi], gb_ref, lam_ref, wout_ref, o_ref, hstate)


def _lru_front(gi, h_ref, win_ref, cw_ref, cb_ref, gw_ref, xbuf):
    tt, d = h_ref.shape[1], h_ref.shape[2]
    c = cw_ref.shape[1]
    ts = tt * SUBLANES
    hres = h_ref[gi * SUBLANES:(gi + 1) * SUBLANES]
    ht = pltpu.einshape("btd->tbd", hres).reshape(ts, d)
    u = jnp.dot(ht.astype(BF16), win_ref[...], preferred_element_type=F32)
    gate_branch = _gelu_tanh(u[:, :c])
    x = u[:, c:]
    xbuf[gi, LRU_HIST:LRU_HIST + ts, :] = x
    xr = cb_ref[...] + cw_ref[3:4, :] * x
    for k in range(CONV_WIDTH - 1):
        xr = xr + cw_ref[k:k + 1, :] * xbuf[gi, k * SUBLANES:k * SUBLANES + ts, :]
    xbuf[gi, 0:LRU_HIST, :] = xbuf[gi, ts:ts + LRU_HIST, :]

    xb = xr.astype(BF16)
    tiles = [jnp.dot(xb[:, lo:lo + LRU_GATE_WIN], gw_ref[n], preferred_element_type=F32)
             for n, lo in enumerate(_lru_gate_window_starts())]
    r_pre = jnp.concatenate([t[:, :LRU_GATE_TILE] for t in tiles], axis=1)[:, :c]
    i_pre = jnp.concatenate([t[:, LRU_GATE_TILE:] for t in tiles], axis=1)[:, :c]
    return hres, gate_branch, xr, r_pre, i_pre


def _lru_back(gi, hres, gate_branch, xr, r_pre, i_pre, gb_ref, lam_ref, wout_ref, o_ref, hstate):
    ts, c = xr.shape
    tt = ts // SUBLANES
    r_gate = _sigmoid(r_pre + gb_ref[0:1, :])
    i_gate = _sigmoid(i_pre + gb_ref[1:2, :])
    log_a = (-RG_LRU_C) * r_gate * _softplus(-lam_ref[...])
    a = jnp.exp(log_a)
    bt = jnp.sqrt(-jnp.tanh(log_a) * (a * a + 1.0)) * (i_gate * xr)

    h = hstate[gi]
    steps = []
    for t in range(tt):
        slab = slice(t * SUBLANES, (t + 1) * SUBLANES)
        h = a[slab, :] * h + bt[slab, :]
        steps.append(h)
    hstate[gi] = h
    hs = jnp.concatenate(steps, axis=0)

    y = jnp.dot((hs * gate_branch).astype(BF16), wout_ref[...], preferred_element_type=F32)
    yb = pltpu.einshape("tbd->btd", y.reshape(tt, SUBLANES, y.shape[1]))
    o_ref[gi * SUBLANES:(gi + 1) * SUBLANES] = DEEPNORM_ALPHA * hres + yb


def _lru(h, win, cw, cb, gw, gb, lam, wout, tt=32):
    bsz, seq, d = h.shape
    c = cw.shape[1]
    assert bsz % SUBLANES == 0
    tt = min(tt, seq)
    ng = bsz // SUBLANES
    resident = lambda w: pl.BlockSpec(w.shape, lambda j: (0,) * w.ndim, pipeline_mode=pl.Buffered(1))
    return pl.pallas_call(
        _lru_kernel,
        grid=(seq // tt,),
        in_specs=[pl.BlockSpec((bsz, tt, d), lambda j: (0, j, 0))]
                 + [resident(w) for w in (win, cw, cb, gw, gb, lam, wout)],
        out_specs=pl.BlockSpec((bsz, tt, d), lambda j: (0, j, 0)),
        out_shape=jax.ShapeDtypeStruct((bsz, seq, d), F32),
        scratch_shapes=[pltpu.VMEM((ng, LRU_HIST + tt * SUBLANES, c), F32), pltpu.VMEM((ng, SUBLANES, c), F32)],
        compiler_params=_cparams("arbitrary"),
        name="lru",
    )(h, win, cw, cb, gw, gb, lam, wout)


def _pad_last(x, n):
    return jnp.pad(x, [(0, 0)] * (x.ndim - 1) + [(0, n - x.shape[-1])])


def _rglru_sublayer(h, w_in, conv_w, conv_b, gate_w, gate_b, lam, w_out):
    bsz, seq, d = h.shape
    c = D_RNN_PAD
    w_cat = jnp.concatenate([_pad_last(w_in[:, :D_RNN], c), _pad_last(w_in[:, D_RNN:], c)], axis=1).astype(BF16)
    eye = jnp.eye(LRU_BLOCKS, dtype=F32)
    dense = jnp.einsum('gncd,nm->gncmd', gate_w, eye).reshape(2, D_RNN, D_RNN)
    starts = _lru_gate_window_starts()
    dense = jnp.pad(dense, ((0, 0), (0, c - D_RNN), (0, len(starts) * LRU_GATE_TILE - D_RNN)))
    gw = jnp.stack([jnp.concatenate([dense[gi, lo:lo + LRU_GATE_WIN, n * LRU_GATE_TILE:(n + 1) * LRU_GATE_TILE]
                                     for gi in range(2)], axis=1) for n, lo in enumerate(starts)]).astype(BF16)
    wout = jnp.pad(w_out, ((0, c - D_RNN), (0, 0))).astype(BF16)
    return _lru(h, w_cat, _pad_last(conv_w, c), _pad_last(conv_b[None, :], c), gw,
                _pad_last(gate_b, c), _pad_last(lam[None, :], c), wout)


def _rwkv_in_kernel(x_ref, xp_ref, mu_ref, wr_ref, wk_ref, wv_ref, w0_ref, w1_ref, w2_ref,
                    a0_ref, a1_ref, a2_ref, g1_ref, g2_ref,
                    r_ref, k_ref, v_ref, ld_ref, a_ref, g_ref, *, sub):
    i = pl.program_id(1)
    x_all = x_ref[0]
    row = lax.broadcasted_iota(jnp.int32, x_all.shape, 0)
    prev = jnp.where(i > 0, xp_ref[0, SUBLANES - 1:SUBLANES, :], 0.0)
    xx_all = jnp.where(row == 0, prev, pltpu.roll(x_all, 1, 0)) - x_all
    dot = lambda p, q: jnp.dot(p, q, preferred_element_type=F32)
    for r0 in range(0, x_all.shape[0], sub):
        rows = slice(r0, r0 + sub)
        x, xx = x_all[rows, :], xx_all[rows, :]
        mix = lambda c: (x + xx * mu_ref[c:c + 1, :]).astype(BF16)
        r_ref[0, rows, :] = dot(mix(0), wr_ref[...]).astype(r_ref.dtype)
        k_ref[0, rows, :] = dot(mix(2), wk_ref[...]).astype(k_ref.dtype)
        v_ref[0, rows, :] = dot(mix(3), wv_ref[...]).astype(v_ref.dtype)
        wl = w0_ref[...] + dot(jnp.tanh(dot(mix(1), w1_ref[...])).astype(BF16), w2_ref[...])
        w_log = -_softplus(-wl) - 0.5
        ld_ref[0, rows, :] = -jnp.exp(w_log)
        a_ref[0, rows, :] = _sigmoid(a0_ref[...] + dot(dot(mix(4), a1_ref[...]).astype(BF16), a2_ref[...]))
        g_ref[0, rows, :] = dot(_sigmoid(dot(mix(5), g1_ref[...])).astype(BF16), g2_ref[...]).astype(g_ref.dtype)


def _rwkv_in(h, mu, wr, wk, wv, w0, w1, w2, a0, a1, a2, g1, g2, tm=512, sub=256):
    bsz, seq, d = h.shape
    tm = min(tm, seq)
    row = lambda b, i: (b, i, 0)
    fix = lambda b, i: (0, 0)
    prev = lambda b, i: (b, jnp.maximum(i * (tm // SUBLANES) - 1, 0), 0)
    ws = [mu, wr, wk, wv, w0, w1, w2, a0, a1, a2, g1, g2]
    return pl.pallas_call(
        functools.partial(_rwkv_in_kernel, sub=min(sub, tm)),
        grid=(bsz, seq // tm),
        in_specs=[pl.BlockSpec((1, tm, d), row), pl.BlockSpec((1, SUBLANES, d), prev)]
                 + [pl.BlockSpec(w.shape, fix) for w in ws],
        out_specs=[pl.BlockSpec((1, tm, d), row)] * 6,
        out_shape=[jax.ShapeDtypeStruct((bsz, seq, d), dt) for dt in (BF16, BF16, BF16, F32, F32, BF16)],
        compiler_params=_cparams("parallel", "parallel"),
        name="rwkv_in",
    )(h, h, *ws)


def _bdot(a, b, dims):
    return jnp.einsum(dims, a, b, preferred_element_type=F32)


def _rwkv_rec_kernel(r_ref, k_ref, v_ref, ld_ref, a_ref, g_ref, kk_ref, ka_ref, rk_ref, lg_ref, lb_ref,
                     o_ref, state):
    @pl.when(pl.program_id(1) == 0)
    def _():
        state[...] = jnp.zeros(state.shape, F32)

    s_val = state[...]
    for c in range(r_ref.shape[1] // RW_CHUNK):
        s_val = _rwkv_chunk(slice(c * RW_CHUNK, (c + 1) * RW_CHUNK), s_val, r_ref, k_ref, v_ref, ld_ref, a_ref,
                            g_ref, kk_ref, ka_ref, rk_ref, lg_ref, lb_ref, o_ref)
    state[...] = s_val


def _rwkv_chunk(rows, s_old, r_ref, k_ref, v_ref, ld_ref, a_ref, g_ref, kk_ref, ka_ref, rk_ref, lg_ref, lb_ref,
                o_ref):
    L = RW_CHUNK
    P = RW_PAIRS
    N = RW_HEAD_SIZE
    pairs = lambda ref: jnp.stack([ref[0, rows, LANES * p:LANES * (p + 1)] for p in range(P)]).astype(F32)
    vec = lambda ref: jnp.stack([ref[:, LANES * p:LANES * (p + 1)] for p in range(P)])

    lane = lax.broadcasted_iota(jnp.int32, (1, 1, LANES), 2)
    head0 = lane < N
    ri = lax.broadcasted_iota(jnp.int32, (LANES, LANES), 0)
    ci = lax.broadcasted_iota(jnp.int32, (LANES, LANES), 1)
    seg = ((ri // N) == (ci // N)).astype(BF16)

    def segsum(x, terms=2):
        flat = x.reshape(P * L, LANES)
        hi = flat.astype(BF16)
        out = jnp.dot(hi, seg, preferred_element_type=F32)
        if terms == 2:
            lo = (flat - hi.astype(F32)).astype(BF16)
            out = out + jnp.dot(lo, seg, preferred_element_type=F32)
        return out.reshape(P, L, LANES)

    r = pairs(r_ref)
    k = pairs(k_ref)
    v = pairs(v_ref)
    a = pairs(a_ref)

    kk = k * vec(kk_ref)
    kk = kk * lax.rsqrt(jnp.maximum(segsum(kk * kk, terms=1), 1e-24))
    k = k * (1.0 + (a - 1.0) * vec(ka_ref))
    aa = -kk
    bb = kk * a

    ld_full = ld_ref[0, rows, :]
    tr = lax.broadcasted_iota(jnp.int32, (L, L), 0)
    tc = lax.broadcasted_iota(jnp.int32, (L, L), 1)
    tri = (tc <= tr).astype(BF16)
    cum_full = sum(jnp.dot(tri, t, preferred_element_type=F32) for t in _split2(ld_full))
    cum = jnp.stack([cum_full[:, LANES * p:LANES * (p + 1)] for p in range(P)])
    ld = pairs(ld_ref)
    gam = jnp.exp(cum)
    gam_prev = jnp.exp(cum - ld)
    cum_end = cum[:, L - 1:L, :]
    r_t = r * gam
    a_t = aa * gam_prev
    k_t = k * jnp.exp(-cum)
    b_t = bb * jnp.exp(-cum)
    k_e = k * jnp.exp(cum_end - cum)
    b_e = bb * jnp.exp(cum_end - cum)

    def stacked(x):
        return jnp.concatenate([jnp.where(head0, x, 0.0), jnp.where(head0, 0.0, x)], axis=1)

    ar = jnp.concatenate([stacked(a_t), stacked(r_t)], axis=1).astype(BF16)
    kb = jnp.concatenate([stacked(k_t), stacked(b_t)], axis=1).astype(BF16)
    v_s = stacked(v)

    s4 = _bdot(ar, kb, 'pik,pjk->pij')
    ah = _bdot(ar, s_old.astype(BF16), 'pik,plk->pil')
    M = 2 * L
    si = lax.broadcasted_iota(jnp.int32, (1, M, M), 1)
    sj = lax.broadcasted_iota(jnp.int32, (1, M, M), 2)
    same = (si // L) == (sj // L)
    strict = same & ((sj % L) < (si % L))
    incl = same & ((sj % L) <= (si % L))
    a_k = jnp.where(strict, s4[:, :M, :M], 0.0)
    a_b = jnp.where(strict, s4[:, :M, M:], 0.0)
    r_k = jnp.where(incl, s4[:, M:, :M], 0.0)
    r_b = jnp.where(incl, s4[:, M:, M:], 0.0)

    eye = (si == sj).astype(F32)
    lower1 = (si // 2 == sj // 2) & (si % 2 == 1) & (sj % 2 == 0)
    T = eye + jnp.where(lower1, a_b, 0.0)
    ab16 = a_b.astype(BF16)
    s = 2
    hi_ = lax.broadcasted_iota(jnp.int32, (1, M // 2, M), 1)
    hj_ = lax.broadcasted_iota(jnp.int32, (1, M // 2, M), 2)
    while s < L:
        tb = T.astype(BF16)
        if s < SUBLANES:
            cm = (si // (2 * s) == sj // (2 * s)) & (si % (2 * s) >= s) & (sj % (2 * s) < s)
            tat = _bdot(_bdot(tb, ab16, 'pij,pjk->pik').astype(BF16), tb, 'pij,pjk->pik')
            T = T + jnp.where(cm, tat, 0.0)
        else:
            nblk = M // (2 * s)
            low = jnp.concatenate([T[:, (2 * k + 1) * s:(2 * k + 2) * s, :] for k in range(nblk)], axis=1)
            upd = _bdot(_bdot(low.astype(BF16), ab16, 'pij,pjk->pik').astype(BF16), tb, 'pij,pjk->pik')
            upd = jnp.where((hj_ // (2 * s) == hi_ // s) & (hj_ % (2 * s) < s), upd, 0.0)
            zero = jnp.zeros((P, s, M), F32)
            T = T + jnp.concatenate([blk for k in range(nblk) for blk in (zero, upd[:, k * s:(k + 1) * s, :])],
                                    axis=1)
        s *= 2

    x_s = ah[:, :M, :] + _bdot(a_k.astype(BF16), v_s.astype(BF16), 'pij,pjl->pil')
    u_s = _bdot(T.astype(BF16), x_s.astype(BF16), 'pij,pjl->pil')
    vu = jnp.concatenate([v_s, u_s], axis=1).astype(BF16)
    rkb = jnp.concatenate([r_k, r_b], axis=2).astype(BF16)
    y_s = ah[:, M:, :] + _bdot(rkb, vu, 'pij,pjl->pil')
    y = y_s[:, :L, :] + y_s[:, L:, :]

    kb_e = jnp.concatenate([stacked(k_e), stacked(b_e)], axis=1).astype(BF16)
    s_new = s_old * jnp.exp(cum_end) + _bdot(vu, kb_e, 'pti,ptj->pij')

    mean = segsum(y) * (1.0 / N)
    yc = y - mean
    var = segsum(yc * yc, terms=1) * (1.0 / N)
    yn = yc * lax.rsqrt(var + RW_GN_EPS) * vec(lg_ref) + vec(lb_ref)
    bonus = segsum(r * k * vec(rk_ref)) * v
    out = (yn + bonus) * pairs(g_ref)
    for p in range(P):
        o_ref[0, rows, LANES * p:LANES * (p + 1)] = out[p].astype(o_ref.dtype)
    return s_new


def _rwkv_rec(r, k, v, ld, a, g, k_k, k_a, r_k, lnx_g, lnx_b, chunks_per_step=4):
    bsz, seq, d = r.shape
    L = RW_CHUNK * min(chunks_per_step, seq // RW_CHUNK)
    row = lambda b, j: (b, j, 0)
    fix = lambda b, j: (0, 0)
    return pl.pallas_call(
        _rwkv_rec_kernel,
        grid=(bsz, seq // L),
        in_specs=[pl.BlockSpec((1, L, d), row)] * 6 + [pl.BlockSpec((1, d), fix)] * 5,
        out_specs=pl.BlockSpec((1, L, d), row),
        out_shape=jax.ShapeDtypeStruct((bsz, seq, d), BF16),
        scratch_shapes=[pltpu.VMEM((RW_PAIRS, LANES, LANES), F32)],
        compiler_params=_cparams("parallel", "arbitrary"),
        name="rwkv_rec",
    )(r, k, v, ld, a, g, k_k, k_a, r_k, lnx_g, lnx_b)


def _rwkv_sublayer(h, mu, w_r, w_k, w_v, w0, w1, w2, a0, a1, a2, g1, g2, k_k, k_a, r_k,
                   lnx_g, lnx_b, w_o):
    bsz, seq, d = h.shape
    bf = lambda w: w.astype(BF16)
    row = lambda w: w.reshape(1, d)
    mu8 = jnp.pad(mu, ((0, SUBLANES - mu.shape[0]), (0, 0)))
    r, k, v, ld, a, gg = _rwkv_in(h, mu8, bf(w_r), bf(w_k), bf(w_v), row(w0), bf(w1), bf(w2),
                                  row(a0), bf(a1), bf(a2), bf(g1), bf(g2))
    y = _rwkv_rec(r, k, v, ld, a, gg, row(k_k), row(k_a), row(r_k), row(lnx_g), row(lnx_b))
    return _proj_ln(y.reshape(bsz * seq, d), bf(w_o), h.reshape(bsz * seq, d)).reshape(bsz, seq, d)


def _chunk_attn_kernel(q_ref, k_ref, v_ref, tbl_ref, o_ref):
    seq = q_ref.shape[1]
    blocks_back = (ATT_WIN - ATT_TQ) // ATT_TQ
    lane = lax.broadcasted_iota(jnp.int32, (ATT_TQ, LANES), 1)
    nq = seq // ATT_TQ

    def window(qi):
        back = min(qi, blocks_back)
        ws = (qi - back) * ATT_TQ
        width = (back + 1) * ATT_TQ
        off = (blocks_back - back) * ATT_TQ
        return ws, width, off

    def scores(qi):
        ws, width, _ = window(qi)
        kw = k_ref[0, ws:ws + width, :]
        q = q_ref[0, qi * ATT_TQ:(qi + 1) * ATT_TQ, :]
        out = []
        for hh in range(2):
            qh = jnp.where((lane // ATT_HEAD_DIM) == hh, q, jnp.zeros_like(q))
            out.append(lax.dot_general(qh, kw, (((1,), (1,)), ((), ())), preferred_element_type=F32))
        return out

    s_next = scores(0)
    for qi in range(nq):
        s_cur = s_next
        if qi + 1 < nq:
            s_next = scores(qi + 1)
        ws, width, off = window(qi)
        vw = v_ref[0, ws:ws + width, :]
        vlane = lax.broadcasted_iota(jnp.int32, vw.shape, 1)
        pv = []
        for hh in range(2):
            s = s_cur[hh] + tbl_ref[0, hh, :, off:off + width]
            m = jnp.max(s, axis=-1, keepdims=True)
            e = jnp.exp((s - m).astype(BF16))
            vh = jnp.where((vlane // ATT_HEAD_DIM) == hh, vw, jnp.ones_like(vw))
            pv.append(jnp.dot(e, vh, preferred_element_type=F32))
        num = jnp.where(lane < ATT_HEAD_DIM, pv[0], pv[1])
        den = jnp.where(lane < ATT_HEAD_DIM, pltpu.roll(pv[0], ATT_HEAD_DIM, 1), pltpu.roll(pv[1], ATT_HEAD_DIM, 1))
        o_ref[0, qi * ATT_TQ:(qi + 1) * ATT_TQ, :] = (num / den).astype(o_ref.dtype)


def _chunk_attn(qkv, tbl):
    bsz, seq, _ = qkv.shape
    return pl.pallas_call(
        _chunk_attn_kernel,
        grid=(bsz, ATT_PAIRS),
        in_specs=[pl.BlockSpec((1, seq, LANES), lambda b, p: (b, 0, p)),
                  pl.BlockSpec((1, seq, LANES), lambda b, p: (b, 0, ATT_PAIRS + p)),
                  pl.BlockSpec((1, seq, LANES), lambda b, p: (b, 0, 2 * ATT_PAIRS + p)),
                  pl.BlockSpec((1, 2, ATT_TQ, ATT_TBL), lambda b, p: (p, 0, 0, 0))],
        out_specs=pl.BlockSpec((1, seq, LANES), lambda b, p: (b, 0, p)),
        out_shape=jax.ShapeDtypeStruct((bsz, seq, D_MODEL), BF16),
        compiler_params=_cparams("parallel", "parallel"),
        name="chunk_attn",
    )(qkv, qkv, qkv, tbl)


ATT_PERIOD = ATT_TBL + ATT_TQ


def _attn_table_kernel(vec_ref, o_ref):
    p = pl.program_id(0)
    ic = lax.broadcasted_iota(jnp.int32, (ATT_TQ, ATT_TBL), 0) // CHUNK
    mc = lax.broadcasted_iota(jnp.int32, (ATT_TQ, ATT_TBL), 1) // CHUNK
    valid = (mc >= ic) & (mc <= ic + BAND_CHUNKS - 1)
    for hh in range(2):
        row = vec_ref[pl.ds(2 * p + hh, 1), :]
        skew = pltpu.roll(jnp.broadcast_to(row, (ATT_TQ, ATT_PERIOD)), 0, 1, stride=1, stride_axis=0)
        o_ref[0, hh] = jnp.where(valid, skew[:, :ATT_TBL], NEG_INF)


def _attn_bias_table(rel_bias):
    left = ATT_WIN - ATT_TQ
    j = jnp.arange(ATT_PERIOD)
    diff = jnp.where(j < ATT_TBL, j, j - ATT_PERIOD)
    vec = rel_bias[:, jnp.clip(left - diff, -MAX_REL, MAX_REL) + MAX_REL].astype(F32)
    return pl.pallas_call(
        _attn_table_kernel,
        grid=(ATT_PAIRS,),
        in_specs=[pl.BlockSpec(vec.shape, lambda p: (0, 0))],
        out_specs=pl.BlockSpec((1, 2, ATT_TQ, ATT_TBL), lambda p: (p, 0, 0, 0)),
        out_shape=jax.ShapeDtypeStruct((ATT_PAIRS, 2, ATT_TQ, ATT_TBL), F32),
        compiler_params=_cparams("parallel"),
        name="attn_table",
    )(vec)


def _attn_sublayer(h, w_qkv, rel_bias, w_o):
    bsz, seq, d = h.shape
    assert seq % ATT_TQ == 0 and seq >= ATT_WIN
    scale = jnp.concatenate([jnp.full((d,), ATT_HEAD_DIM ** -0.5, F32), jnp.ones((2 * d,), F32)])
    qkv = _proj(h.reshape(bsz * seq, d), (w_qkv * scale).astype(BF16), BF16)
    o = _chunk_attn(qkv.reshape(bsz, seq, 3 * d), _attn_bias_table(rel_bias))
    return _proj_ln(o.reshape(bsz * seq, d), w_o.astype(BF16), h.reshape(bsz * seq, d)).reshape(bsz, seq, d)


def _mem_attn_kernel(z_ref, gin_ref, bin_ref, kv_ref, wq_ref, wo_ref, o_ref, *, sub):
    tiles = [slice(r0, r0 + sub) for r0 in range(0, z_ref.shape[1], sub)]

    def front(rows):
        h = _ln(z_ref[0, rows, :], gin_ref[...], bin_ref[...])
        q = jnp.dot(h.astype(BF16), wq_ref[...], preferred_element_type=F32).astype(BF16)
        return h, [lax.dot_general(q[:, hh * MEM_HEAD_DIM:(hh + 1) * MEM_HEAD_DIM],
                                   kv_ref[0, :, hh * MEM_HEAD_DIM:(hh + 1) * MEM_HEAD_DIM],
                                   (((1,), (1,)), ((), ())), preferred_element_type=F32) for hh in range(MEM_HEADS)]

    def back(rows, h, scores):
        outs = []
        for hh, s in enumerate(scores):
            vh = kv_ref[0, :, D_MODEL + hh * MEM_HEAD_DIM:D_MODEL + (hh + 1) * MEM_HEAD_DIM]
            m = jnp.max(s, axis=-1, keepdims=True)
            e = jnp.exp(s - m)
            l = jnp.sum(e, axis=-1, keepdims=True)
            outs.append((jnp.dot(e.astype(BF16), vh, preferred_element_type=F32) / l).astype(BF16))
        y = jnp.dot(jnp.concatenate(outs, axis=1), wo_ref[...], preferred_element_type=F32)
        o_ref[0, rows, :] = DEEPNORM_ALPHA * h + y

    nxt = front(tiles[0])
    for i, rows in enumerate(tiles):
        cur = nxt
        if i + 1 < len(tiles):
            nxt = front(tiles[i + 1])
        back(rows, *cur)


def _mem_sublayer(z, g_in, b_in, mem, w_q, w_kv, w_o, tm=1024, sub=256):
    bsz, seq, d = z.shape
    nm = mem.shape[1]
    tm = min(tm, seq)
    kv = _proj(mem.reshape(bsz * nm, d), w_kv.astype(BF16), BF16).reshape(bsz, nm, 2 * d)
    wq = (w_q * (MEM_HEAD_DIM ** -0.5)).astype(BF16)
    row = lambda b_, i: (b_, i, 0)
    fix = lambda b_, i: (0, 0)
    return pl.pallas_call(
        functools.partial(_mem_attn_kernel, sub=min(sub, tm)),
        grid=(bsz, seq // tm),
        in_specs=[pl.BlockSpec((1, tm, d), row),
                  pl.BlockSpec((1, d), fix), pl.BlockSpec((1, d), fix),
                  pl.BlockSpec((1, nm, 2 * d), lambda b_, i: (b_, 0, 0)),
                  pl.BlockSpec((d, d), fix, pipeline_mode=pl.Buffered(1)),
                  pl.BlockSpec((d, d), fix, pipeline_mode=pl.Buffered(1))],
        out_specs=pl.BlockSpec((1, tm, d), row),
        out_shape=jax.ShapeDtypeStruct((bsz, seq, d), F32),
        compiler_params=_cparams("parallel", "parallel"),
        name="mem_attn",
    )(z, g_in, b_in, kv, wq, w_o.astype(BF16))


def _mlp_kernel(z_ref, gin_ref, bin_ref, w1_ref, w2_ref, g_ref, b_ref, o_ref, *, sub):
    for r0 in range(0, z_ref.shape[0], sub):
        rows = slice(r0, r0 + sub)
        h = _ln(z_ref[rows, :], gin_ref[...], bin_ref[...])
        u = jnp.dot(h.astype(BF16), w1_ref[...], preferred_element_type=F32)
        u = jnp.maximum(u, 0.0)
        y = jnp.dot((u * u).astype(BF16), w2_ref[...], preferred_element_type=F32)
        o_ref[rows, :] = _residual_ln(h, y, g_ref[...], b_ref[...])


def _mlp_sublayer(z, g_in, b_in, w1, w2, g, b, tm=1024, sub=256):
    bsz, seq, d = z.shape
    t = bsz * seq
    tm = min(tm, t)
    ff = w1.shape[1]
    resident = pl.Buffered(1)
    out = pl.pallas_call(
        functools.partial(_mlp_kernel, sub=min(sub, tm)),
        grid=(t // tm,),
        in_specs=[pl.BlockSpec((tm, d), lambda i: (i, 0)),
                  pl.BlockSpec((1, d), lambda i: (0, 0)),
                  pl.BlockSpec((1, d), lambda i: (0, 0)),
                  pl.BlockSpec((d, ff), lambda i: (0, 0), pipeline_mode=resident),
                  pl.BlockSpec((ff, d), lambda i: (0, 0), pipeline_mode=resident),
                  pl.BlockSpec((1, d), lambda i: (0, 0)),
                  pl.BlockSpec((1, d), lambda i: (0, 0))],
        out_specs=pl.BlockSpec((tm, d), lambda i: (i, 0)),
        out_shape=jax.ShapeDtypeStruct((t, d), F32),
        compiler_params=_cparams("parallel"),
        name="mlp",
    )(z.reshape(t, d), g_in, b_in, w1.astype(BF16), w2.astype(BF16), g, b)
    return out.reshape(bsz, seq, d)


def kernel(x, mem, ln_g, ln_b, lru_w_in, lru_conv_w, lru_conv_b, lru_gate_w, lru_gate_b, lru_lambda, lru_w_out, rw_mu, rw_w_r, rw_w_k, rw_w_v, rw_w0, rw_w1, rw_w2, rw_a0, rw_a1, rw_a2, rw_g1, rw_g2, rw_k_k, rw_k_a, rw_r_k, rw_lnx_g, rw_lnx_b, rw_w_o, ca_w_qkv, ca_rel_bias, ca_w_o, mx_w_q, mx_w_kv, mx_w_o, mlp_w1, mlp_w2):
    h = x
    for i in range(DEPTH):
        kind, j = i % N_MIXERS, i // N_MIXERS
        g = lambda s: ln_g[i, s][None, :]
        b = lambda s: ln_b[i, s][None, :]
        if kind == 0:
            z = _rglru_sublayer(h, lru_w_in[j], lru_conv_w[j], lru_conv_b[j], lru_gate_w[j], lru_gate_b[j],
                                lru_lambda[j], lru_w_out[j])
        elif kind == 1:
            z = _rwkv_sublayer(h, rw_mu[j], rw_w_r[j], rw_w_k[j], rw_w_v[j], rw_w0[j], rw_w1[j], rw_w2[j],
                               rw_a0[j], rw_a1[j], rw_a2[j], rw_g1[j], rw_g2[j], rw_k_k[j], rw_k_a[j],
                               rw_r_k[j], rw_lnx_g[j], rw_lnx_b[j], rw_w_o[j])
        else:
            z = _attn_sublayer(h, ca_w_qkv[j], ca_rel_bias[j], ca_w_o[j])
        z = _mem_sublayer(z, g(0), b(0), mem, mx_w_q[i], mx_w_kv[i], mx_w_o[i])
        h = _mlp_sublayer(z, g(1), b(1), mlp_w1[i], mlp_w2[i], g(2), b(2))
    return h
```

```python
import functools

import jax
import jax.numpy as jnp
from jax import lax
from jax.experimental import pallas as pl
from jax.experimental.pallas import tpu as pltpu

F32 = jnp.float32
BF16 = jnp.bfloat16

D_MODEL = 1024
DEPTH = 4
CHUNK = 64
N_MIXERS = 3
DEEPNORM_ALPHA = (2 * DEPTH) ** 0.25
LN_EPS = 1e-5

D_RNN = 1344
LRU_BLOCKS = 16
LRU_BLOCK_SIZE = D_RNN // LRU_BLOCKS
CONV_WIDTH = 4
RG_LRU_C = 8.0
LANES = 128
SUBLANES = 8
D_RNN_PAD = -(-D_RNN // LANES) * LANES

RW_HEAD_SIZE = 64
RW_HEADS = D_MODEL // RW_HEAD_SIZE
RW_PAIRS = D_MODEL // LANES
RW_GN_EPS = 64e-5
RW_CHUNK = 64

ATT_HEADS = 16
ATT_HEAD_DIM = D_MODEL // ATT_HEADS
ATT_PAIRS = D_MODEL // LANES
BAND_CHUNKS = 9
MAX_REL = 2 * CHUNK
NEG_INF = -1e30
ATT_TQ = 512
ATT_WIN = ATT_TQ + (BAND_CHUNKS - 1) * CHUNK
ATT_TBL = ATT_WIN + (BAND_CHUNKS - 1) * CHUNK

MEM_HEADS = 4
MEM_HEAD_DIM = D_MODEL // MEM_HEADS
D_FF = 4 * D_MODEL

VMEM_LIMIT = 56 * 1024 * 1024


def _cparams(*sem):
    return pltpu.CompilerParams(dimension_semantics=sem, vmem_limit_bytes=VMEM_LIMIT)


def _residual_ln(hres, y, g, b):
    return _ln(DEEPNORM_ALPHA * hres + y, g, b)


def _ln(z, g, b):
    mu = jnp.mean(z, axis=-1, keepdims=True)
    zc = z - mu
    var = jnp.mean(zc * zc, axis=-1, keepdims=True)
    return zc * lax.rsqrt(var + LN_EPS) * g + b


def _gelu_tanh(x):
    return 0.5 * x * (1.0 + jnp.tanh(0.7978845608028654 * (x + 0.044715 * (x * x * x))))


def _softplus(x):
    return jnp.maximum(x, 0.0) + jnp.log1p(jnp.exp(-jnp.abs(x)))


def _sigmoid(x):
    return 0.5 * jnp.tanh(0.5 * x) + 0.5


def _split2(x):
    hi = x.astype(BF16)
    lo = (x - hi.astype(F32)).astype(BF16)
    return hi, lo


def _proj_kernel(x_ref, w_ref, o_ref, *, chunk):
    x = x_ref[...].astype(BF16)
    n = w_ref.shape[1]
    for n0 in range(0, n, chunk):
        y = jnp.dot(x, w_ref[:, n0:n0 + chunk], preferred_element_type=F32)
        o_ref[:, n0:n0 + chunk] = y.astype(o_ref.dtype)


def _proj(x2d, w, out_dtype, tm=512, chunk=512):
    t, k = x2d.shape
    n = w.shape[1]
    tm = min(tm, t)
    return pl.pallas_call(
        functools.partial(_proj_kernel, chunk=chunk),
        grid=(t // tm,),
        in_specs=[pl.BlockSpec((tm, k), lambda i: (i, 0)),
                  pl.BlockSpec((k, n), lambda i: (0, 0))],
        out_specs=pl.BlockSpec((tm, n), lambda i: (i, 0)),
        out_shape=jax.ShapeDtypeStruct((t, n), out_dtype),
        compiler_params=_cparams("parallel"),
        name="proj",
    )(x2d, w)


def _proj_ln_kernel(y_ref, w_ref, h_ref, o_ref, *, sub):
    for r0 in range(0, y_ref.shape[0], sub):
        rows = slice(r0, r0 + sub)
        y = jnp.dot(y_ref[rows, :].astype(BF16), w_ref[...], preferred_element_type=F32)
        o_ref[rows, :] = DEEPNORM_ALPHA * h_ref[rows, :] + y


def _proj_ln(y2d, w, h2d, tm=2048, sub=256):
    t, k = y2d.shape
    d = w.shape[1]
    tm = min(tm, t)
    return pl.pallas_call(
        functools.partial(_proj_ln_kernel, sub=min(sub, tm)),
        grid=(t // tm,),
        in_specs=[pl.BlockSpec((tm, k), lambda i: (i, 0)),
                  pl.BlockSpec((k, d), lambda i: (0, 0)),
                  pl.BlockSpec((tm, d), lambda i: (i, 0))],
        out_specs=pl.BlockSpec((tm, d), lambda i: (i, 0)),
        out_shape=jax.ShapeDtypeStruct((t, d), F32),
        compiler_params=_cparams("parallel"),
        name="proj_ln",
    )(y2d, w, h2d)


LRU_GATE_TILE = 256
LRU_GATE_WIN = 512


def _lru_gate_window_starts():
    starts = []
    for n0 in range(0, D_RNN_PAD, LRU_GATE_TILE):
        first_block = min(n0, D_RNN - 1) // LRU_BLOCK_SIZE
        last_block = min(n0 + LRU_GATE_TILE - 1, D_RNN - 1) // LRU_BLOCK_SIZE
        lo = (first_block * LRU_BLOCK_SIZE) // LANES * LANES
        lo = min(lo, D_RNN_PAD - LRU_GATE_WIN)
        assert (last_block + 1) * LRU_BLOCK_SIZE <= lo + LRU_GATE_WIN
        starts.append(lo)
    return starts


LRU_HIST = (CONV_WIDTH - 1) * SUBLANES


def _lru_kernel(h_ref, win_ref, cw_ref, cb_ref, gw_ref, gb_ref, lam_ref, wout_ref, o_ref, xbuf, hstate):
    @pl.when(pl.program_id(0) == 0)
    def _():
        xbuf[:, 0:LRU_HIST, :] = jnp.zeros((xbuf.shape[0], LRU_HIST, xbuf.shape[2]), F32)
        hstate[...] = jnp.zeros(hstate.shape, F32)

    ng = h_ref.shape[0] // SUBLANES
    fronts = [_lru_front(gi, h_ref, win_ref, cw_ref, cb_ref, gw_ref, xbuf) for gi in range(ng)]
    for gi in range(ng):
        _lru_back(gi, *fronts[gi], gb_ref, lam_ref, wout_ref, o_ref, hstate)


def _lru_front(gi, h_ref, win_ref, cw_ref, cb_ref, gw_ref, xbuf):
    tt, d = h_ref.shape[1], h_ref.shape[2]
    c = cw_ref.shape[1]
    ts = tt * SUBLANES
    hres = h_ref[gi * SUBLANES:(gi + 1) * SUBLANES]
    ht = pltpu.einshape("btd->tbd", hres).reshape(ts, d)
    u = jnp.dot(ht.astype(BF16), win_ref[...], preferred_element_type=F32)
    gate_branch = _gelu_tanh(u[:, :c])
    x = u[:, c:]
    xbuf[gi, LRU_HIST:LRU_HIST + ts, :] = x
    xr = cb_ref[...] + cw_ref[3:4, :] * x
    for k in range(CONV_WIDTH - 1):
        xr = xr + cw_ref[k:k + 1, :] * xbuf[gi, k * SUBLANES:k * SUBLANES + ts, :]
    xbuf[gi, 0:LRU_HIST, :] = xbuf[gi, ts:ts + LRU_HIST, :]

    xb = xr.astype(BF16)
    tiles = [jnp.dot(xb[:, lo:lo + LRU_GATE_WIN], gw_ref[n], preferred_element_type=F32)
             for n, lo in enumerate(_lru_gate_window_starts())]
    r_pre = jnp.concatenate([t[:, :LRU_GATE_TILE] for t in tiles], axis=1)[:, :c]
    i_pre = jnp.concatenate([t[:, LRU_GATE_TILE:] for t in tiles], axis=1)[:, :c]
    return hres, gate_branch, xr, r_pre, i_pre


def _lru_back(gi, hres, gate_branch, xr, r_pre, i_pre, gb_ref, lam_ref, wout_ref, o_ref, hstate):
    ts, c = xr.shape
    tt = ts // SUBLANES
    r_gate = _sigmoid(r_pre + gb_ref[0:1, :])
    i_gate = _sigmoid(i_pre + gb_ref[1:2, :])
    log_a = (-RG_LRU_C) * r_gate * _softplus(-lam_ref[...])
    a = jnp.exp(log_a)
    bt = jnp.sqrt(-jnp.tanh(log_a) * (a * a + 1.0)) * (i_gate * xr)

    h = hstate[gi]
    steps = []
    for t in range(tt):
        slab = slice(t * SUBLANES, (t + 1) * SUBLANES)
        h = a[slab, :] * h + bt[slab, :]
        steps.append(h)
    hstate[gi] = h
    hs = jnp.concatenate(steps, axis=0)

    y = jnp.dot((hs * gate_branch).astype(BF16), wout_ref[...], preferred_element_type=F32)
    yb = pltpu.einshape("tbd->btd", y.reshape(tt, SUBLANES, y.shape[1]))
    o_ref[gi * SUBLANES:(gi + 1) * SUBLANES] = DEEPNORM_ALPHA * hres + yb


def _lru(h, win, cw, cb, gw, gb, lam, wout, tt=32):
    bsz, seq, d = h.shape
    c = cw.shape[1]
    assert bsz % SUBLANES == 0
    tt = min(tt, seq)
    ng = bsz // SUBLANES
    resident = lambda w: pl.BlockSpec(w.shape, lambda j: (0,) * w.ndim, pipeline_mode=pl.Buffered(1))
    return pl.pallas_call(
        _lru_kernel,
        grid=(seq // tt,),
        in_specs=[pl.BlockSpec((bsz, tt, d), lambda j: (0, j, 0))]
                 + [resident(w) for w in (win, cw, cb, gw, gb, lam, wout)],
        out_specs=pl.BlockSpec((bsz, tt, d), lambda j: (0, j, 0)),
        out_shape=jax.ShapeDtypeStruct((bsz, seq, d), F32),
        scratch_shapes=[pltpu.VMEM((ng, LRU_HIST + tt * SUBLANES, c), F32), pltpu.VMEM((ng, SUBLANES, c), F32)],
        compiler_params=_cparams("arbitrary"),
        name="lru",
    )(h, win, cw, cb, gw, gb, lam, wout)


def _pad_last(x, n):
    return jnp.pad(x, [(0, 0)] * (x.ndim - 1) + [(0, n - x.shape[-1])])


def _rglru_sublayer(h, w_in, conv_w, conv_b, gate_w, gate_b, lam, w_out):
    bsz, seq, d = h.shape
    c = D_RNN_PAD
    w_cat = jnp.concatenate([_pad_last(w_in[:, :D_RNN], c), _pad_last(w_in[:, D_RNN:], c)], axis=1).astype(BF16)
    eye = jnp.eye(LRU_BLOCKS, dtype=F32)
    dense = jnp.einsum('gncd,nm->gncmd', gate_w, eye).reshape(2, D_RNN, D_RNN)
    starts = _lru_gate_window_starts()
    dense = jnp.pad(dense, ((0, 0), (0, c - D_RNN), (0, len(starts) * LRU_GATE_TILE - D_RNN)))
    gw = jnp.stack([jnp.concatenate([dense[gi, lo:lo + LRU_GATE_WIN, n * LRU_GATE_TILE:(n + 1) * LRU_GATE_TILE]
                                     for gi in range(2)], axis=1) for n, lo in enumerate(starts)]).astype(BF16)
    wout = jnp.pad(w_out, ((0, c - D_RNN), (0, 0))).astype(BF16)
    return _lru(h, w_cat, _pad_last(conv_w, c), _pad_last(conv_b[None, :], c), gw,
                _pad_last(gate_b, c), _pad_last(lam[None, :], c), wout)


def _rwkv_in_kernel(x_ref, xp_ref, mu_ref, wr_ref, wk_ref, wv_ref, w0_ref, w1_ref, w2_ref,
                    a0_ref, a1_ref, a2_ref, g1_ref, g2_ref,
                    r_ref, k_ref, v_ref, ld_ref, a_ref, g_ref, *, sub):
    i = pl.program_id(1)
    x_all = x_ref[0]
    row = lax.broadcasted_iota(jnp.int32, x_all.shape, 0)
    prev = jnp.where(i > 0, xp_ref[0, SUBLANES - 1:SUBLANES, :], 0.0)
    xx_all = jnp.where(row == 0, prev, pltpu.roll(x_all, 1, 0)) - x_all
    dot = lambda p, q: jnp.dot(p, q, preferred_element_type=F32)
    for r0 in range(0, x_all.shape[0], sub):
        rows = slice(r0, r0 + sub)
        x, xx = x_all[rows, :], xx_all[rows, :]
        mix = lambda c: (x + xx * mu_ref[c:c + 1, :]).astype(BF16)
        r_ref[0, rows, :] = dot(mix(0), wr_ref[...]).astype(r_ref.dtype)
        k_ref[0, rows, :] = dot(mix(2), wk_ref[...]).astype(k_ref.dtype)
        v_ref[0, rows, :] = dot(mix(3), wv_ref[...]).astype(v_ref.dtype)
        wl = w0_ref[...] + dot(jnp.tanh(dot(mix(1), w1_ref[...])).astype(BF16), w2_ref[...])
        w_log = -_softplus(-wl) - 0.5
        ld_ref[0, rows, :] = -jnp.exp(w_log)
        a_ref[0, rows, :] = _sigmoid(a0_ref[...] + dot(dot(mix(4), a1_ref[...]).astype(BF16), a2_ref[...]))
        g_ref[0, rows, :] = dot(_sigmoid(dot(mix(5), g1_ref[...])).astype(BF16), g2_ref[...]).astype(g_ref.dtype)


def _rwkv_in(h, mu, wr, wk, wv, w0, w1, w2, a0, a1, a2, g1, g2, tm=512, sub=256):
    bsz, seq, d = h.shape
    tm = min(tm, seq)
    row = lambda b, i: (b, i, 0)
    fix = lambda b, i: (0, 0)
    prev = lambda b, i: (b, jnp.maximum(i * (tm // SUBLANES) - 1, 0), 0)
    ws = [mu, wr, wk, wv, w0, w1, w2, a0, a1, a2, g1, g2]
    return pl.pallas_call(
        functools.partial(_rwkv_in_kernel, sub=min(sub, tm)),
        grid=(bsz, seq // tm),
        in_specs=[pl.BlockSpec((1, tm, d), row), pl.BlockSpec((1, SUBLANES, d), prev)]
                 + [pl.BlockSpec(w.shape, fix) for w in ws],
        out_specs=[pl.BlockSpec((1, tm, d), row)] * 6,
        out_shape=[jax.ShapeDtypeStruct((bsz, seq, d), dt) for dt in (BF16, BF16, BF16, F32, F32, BF16)],
        compiler_params=_cparams("parallel", "parallel"),
        name="rwkv_in",
    )(h, h, *ws)


def _bdot(a, b, dims):
    return jnp.einsum(dims, a, b, preferred_element_type=F32)


def _rwkv_rec_kernel(r_ref, k_ref, v_ref, ld_ref, a_ref, g_ref, kk_ref, ka_ref, rk_ref, lg_ref, lb_ref,
                     o_ref, state):
    @pl.when(pl.program_id(1) == 0)
    def _():
        state[...] = jnp.zeros(state.shape, F32)

    L = RW_CHUNK
    P = RW_PAIRS
    N = RW_HEAD_SIZE
    nc = r_ref.shape[1] // L
    B = nc * P
    cp = [(c, p) for c in range(nc) for p in range(P)]
    pairs = lambda ref: jnp.stack([ref[0, c * L:(c + 1) * L, LANES * p:LANES * (p + 1)]
                                   for c, p in cp]).astype(F32)
    vec = lambda ref: jnp.stack([ref[:, LANES * p:LANES * (p + 1)] for _, p in cp])

    lane = lax.broadcasted_iota(jnp.int32, (1, 1, LANES), 2)
    head0 = lane < N
    ri = lax.broadcasted_iota(jnp.int32, (LANES, LANES), 0)
    ci = lax.broadcasted_iota(jnp.int32, (LANES, LANES), 1)
    seg = ((ri // N) == (ci // N)).astype(BF16)

    def segsum(x, terms=2):
        flat = x.reshape(B * L, LANES)
        hi = flat.astype(BF16)
        out = jnp.dot(hi, seg, preferred_element_type=F32)
        if terms == 2:
            lo = (flat - hi.astype(F32)).astype(BF16)
            out = out + jnp.dot(lo, seg, preferred_element_type=F32)
        return out.reshape(B, L, LANES)

    r = pairs(r_ref)
    k = pairs(k_ref)
    v = pairs(v_ref)
    a = pairs(a_ref)

    kk = k * vec(kk_ref)
    kk = kk * lax.rsqrt(jnp.maximum(segsum(kk * kk, terms=1), 1e-24))
    k = k * (1.0 + (a - 1.0) * vec(ka_ref))
    aa = -kk
    bb = kk * a

    ld_full = ld_ref[0]
    tr = lax.broadcasted_iota(jnp.int32, (nc * L, nc * L), 0)
    tc = lax.broadcasted_iota(jnp.int32, (nc * L, nc * L), 1)
    tri = ((tc <= tr) & (tc // L == tr // L)).astype(BF16)
    cum_full = sum(jnp.dot(tri, t, preferred_element_type=F32) for t in _split2(ld_full))
    cum = jnp.stack([cum_full[c * L:(c + 1) * L, LANES * p:LANES * (p + 1)] for c, p in cp])
    ld = pairs(ld_ref)
    gam = jnp.exp(cum)
    gam_prev = jnp.exp(cum - ld)
    cum_end = cum[:, L - 1:L, :]
    r_t = r * gam
    a_t = aa * gam_prev
    k_t = k * jnp.exp(-cum)
    b_t = bb * jnp.exp(-cum)
    k_e = k * jnp.exp(cum_end - cum)
    b_e = bb * jnp.exp(cum_end - cum)

    def stacked(x):
        return jnp.concatenate([jnp.where(head0, x, 0.0), jnp.where(head0, 0.0, x)], axis=1)

    ar = jnp.concatenate([stacked(a_t), stacked(r_t)], axis=1).astype(BF16)
    kb = jnp.concatenate([stacked(k_t), stacked(b_t)], axis=1).astype(BF16)
    v_s = stacked(v)

    s4 = _bdot(ar, kb, 'pik,pjk->pij')
    M = 2 * L
    si = lax.broadcasted_iota(jnp.int32, (1, M, M), 1)
    sj = lax.broadcasted_iota(jnp.int32, (1, M, M), 2)
    same = (si // L) == (sj // L)
    strict = same & ((sj % L) < (si % L))
    incl = same & ((sj % L) <= (si % L))
    a_k = jnp.where(strict, s4[:, :M, :M], 0.0)
    a_b = jnp.where(strict, s4[:, :M, M:], 0.0)
    r_k = jnp.where(incl, s4[:, M:, :M], 0.0)
    r_b = jnp.where(incl, s4[:, M:, M:], 0.0)

    eye = (si == sj).astype(F32)
    lower1 = (si // 2 == sj // 2) & (si % 2 == 1) & (sj % 2 == 0)
    T = eye + jnp.where(lower1, a_b, 0.0)
    ab16 = a_b.astype(BF16)
    s = 2
    hi_ = lax.broadcasted_iota(jnp.int32, (1, M // 2, M), 1)
    hj_ = lax.broadcasted_iota(jnp.int32, (1, M // 2, M), 2)
    while s < L:
        tb = T.astype(BF16)
        if s < SUBLANES:
            cm = (si // (2 * s) == sj // (2 * s)) & (si % (2 * s) >= s) & (sj % (2 * s) < s)
            tat = _bdot(_bdot(tb, ab16, 'pij,pjk->pik').astype(BF16), tb, 'pij,pjk->pik')
            T = T + jnp.where(cm, tat, 0.0)
        else:
            nblk = M // (2 * s)
            low = jnp.concatenate([T[:, (2 * k + 1) * s:(2 * k + 2) * s, :] for k in range(nblk)], axis=1)
            upd = _bdot(_bdot(low.astype(BF16), ab16, 'pij,pjk->pik').astype(BF16), tb, 'pij,pjk->pik')
            upd = jnp.where((hj_ // (2 * s) == hi_ // s) & (hj_ % (2 * s) < s), upd, 0.0)
            zero = jnp.zeros((B, s, M), F32)
            T = T + jnp.concatenate([blk for k in range(nblk) for blk in (zero, upd[:, k * s:(k + 1) * s, :])],
                                    axis=1)
        s *= 2

    v16 = v_s.astype(BF16)
    akv = _bdot(a_k.astype(BF16), v16, 'pij,pjl->pil')
    t16 = T.astype(BF16)
    rkb = jnp.concatenate([r_k, r_b], axis=2).astype(BF16)
    kb_e = jnp.concatenate([stacked(k_e), stacked(b_e)], axis=1).astype(BF16)
    gam_end = jnp.exp(cum_end)

    s_val = state[...]
    y_parts = []
    for c in range(nc):
        sl = slice(c * P, (c + 1) * P)
        ah = _bdot(ar[sl], s_val.astype(BF16), 'pik,plk->pil')
        x_s = ah[:, :M, :] + akv[sl]
        u_s = _bdot(t16[sl], x_s.astype(BF16), 'pij,pjl->pil')
        vu = jnp.concatenate([v16[sl], u_s.astype(BF16)], axis=1)
        y_parts.append(ah[:, M:, :] + _bdot(rkb[sl], vu, 'pij,pjl->pil'))
        s_val = s_val * gam_end[sl] + _bdot(vu, kb_e[sl], 'pti,ptj->pij')
    state[...] = s_val
    y_s = jnp.concatenate(y_parts, axis=0)
    y = y_s[:, :L, :] + y_s[:, L:, :]

    mean = segsum(y) * (1.0 / N)
    yc = y - mean
    var = segsum(yc * yc, terms=1) * (1.0 / N)
    yn = yc * lax.rsqrt(var + RW_GN_EPS) * vec(lg_ref) + vec(lb_ref)
    bonus = segsum(r * k * vec(rk_ref)) * v
    out = (yn + bonus) * pairs(g_ref)
    for i, (c, p) in enumerate(cp):
        o_ref[0, c * L:(c + 1) * L, LANES * p:LANES * (p + 1)] = out[i].astype(o_ref.dtype)


def _rwkv_rec(r, k, v, ld, a, g, k_k, k_a, r_k, lnx_g, lnx_b, chunks_per_step=4):
    bsz, seq, d = r.shape
    L = RW_CHUNK * min(chunks_per_step, seq // RW_CHUNK)
    row = lambda b, j: (b, j, 0)
    fix = lambda b, j: (0, 0)
    return pl.pallas_call(
        _rwkv_rec_kernel,
        grid=(bsz, seq // L),
        in_specs=[pl.BlockSpec((1, L, d), row)] * 6 + [pl.BlockSpec((1, d), fix)] * 5,
        out_specs=pl.BlockSpec((1, L, d), row),
        out_shape=jax.ShapeDtypeStruct((bsz, seq, d), BF16),
        scratch_shapes=[pltpu.VMEM((RW_PAIRS, LANES, LANES), F32)],
        compiler_params=_cparams("parallel", "arbitrary"),
        name="rwkv_rec",
    )(r, k, v, ld, a, g, k_k, k_a, r_k, lnx_g, lnx_b)


def _rwkv_sublayer(h, mu, w_r, w_k, w_v, w0, w1, w2, a0, a1, a2, g1, g2, k_k, k_a, r_k,
                   lnx_g, lnx_b, w_o):
    bsz, seq, d = h.shape
    bf = lambda w: w.astype(BF16)
    row = lambda w: w.reshape(1, d)
    mu8 = jnp.pad(mu, ((0, SUBLANES - mu.shape[0]), (0, 0)))
    r, k, v, ld, a, gg = _rwkv_in(h, mu8, bf(w_r), bf(w_k), bf(w_v), row(w0), bf(w1), bf(w2),
                                  row(a0), bf(a1), bf(a2), bf(g1), bf(g2))
    y = _rwkv_rec(r, k, v, ld, a, gg, row(k_k), row(k_a), row(r_k), row(lnx_g), row(lnx_b))
    return _proj_ln(y.reshape(bsz * seq, d), bf(w_o), h.reshape(bsz * seq, d)).reshape(bsz, seq, d)


def _chunk_attn_kernel(q_ref, k_ref, v_ref, tbl_ref, o_ref):
    seq = q_ref.shape[1]
    blocks_back = (ATT_WIN - ATT_TQ) // ATT_TQ
    lane = lax.broadcasted_iota(jnp.int32, (ATT_TQ, LANES), 1)
    nq = seq // ATT_TQ

    def window(qi):
        back = min(qi, blocks_back)
        ws = (qi - back) * ATT_TQ
        width = (back + 1) * ATT_TQ
        off = (blocks_back - back) * ATT_TQ
        return ws, width, off

    def scores(qi):
        ws, width, _ = window(qi)
        kw = k_ref[0, ws:ws + width, :]
        q = q_ref[0, qi * ATT_TQ:(qi + 1) * ATT_TQ, :]
        out = []
        for hh in range(2):
            qh = jnp.where((lane // ATT_HEAD_DIM) == hh, q, jnp.zeros_like(q))
            out.append(lax.dot_general(qh, kw, (((1,), (1,)), ((), ())), preferred_element_type=F32))
        return out

    s_next = scores(0)
    for qi in range(nq):
        s_cur = s_next
        if qi + 1 < nq:
            s_next = scores(qi + 1)
        ws, width, off = window(qi)
        vw = v_ref[0, ws:ws + width, :]
        vlane = lax.broadcasted_iota(jnp.int32, vw.shape, 1)
        pv = []
        for hh in range(2):
            s = s_cur[hh] + tbl_ref[0, hh, :, off:off + width]
            m = jnp.max(s, axis=-1, keepdims=True)
            e = jnp.exp((s - m).astype(BF16))
            vh = jnp.where((vlane // ATT_HEAD_DIM) == hh, vw, jnp.ones_like(vw))
            pv.append(jnp.dot(e, vh, preferred_element_type=F32))
        num = jnp.where(lane < ATT_HEAD_DIM, pv[0], pv[1])
        den = jnp.where(lane < ATT_HEAD_DIM, pltpu.roll(pv[0], ATT_HEAD_DIM, 1), pltpu.roll(pv[1], ATT_HEAD_DIM, 1))
        o_ref[0, qi * ATT_TQ:(qi + 1) * ATT_TQ, :] = (num / den).astype(o_ref.dtype)


def _chunk_attn(qkv, tbl):
    bsz, seq, _ = qkv.shape
    return pl.pallas_call(
        _chunk_attn_kernel,
        grid=(bsz, ATT_PAIRS),
        in_specs=[pl.BlockSpec((1, seq, LANES), lambda b, p: (b, 0, p)),
                  pl.BlockSpec((1, seq, LANES), lambda b, p: (b, 0, ATT_PAIRS + p)),
                  pl.BlockSpec((1, seq, LANES), lambda b, p: (b, 0, 2 * ATT_PAIRS + p)),
                  pl.BlockSpec((1, 2, ATT_TQ, ATT_TBL), lambda b, p: (p, 0, 0, 0))],
        out_specs=pl.BlockSpec((1, seq, LANES), lambda b, p: (b, 0, p)),
        out_shape=jax.ShapeDtypeStruct((bsz, seq, D_MODEL), BF16),
        compiler_params=_cparams("parallel", "parallel"),
        name="chunk_attn",
    )(qkv, qkv, qkv, tbl)


ATT_PERIOD = ATT_TBL + ATT_TQ


def _attn_table_kernel(vec_ref, o_ref):
    p = pl.program_id(0)
    ic = lax.broadcasted_iota(jnp.int32, (ATT_TQ, ATT_TBL), 0) // CHUNK
    mc = lax.broadcasted_iota(jnp.int32, (ATT_TQ, ATT_TBL), 1) // CHUNK
    valid = (mc >= ic) & (mc <= ic + BAND_CHUNKS - 1)
    for hh in range(2):
        row = vec_ref[pl.ds(2 * p + hh, 1), :]
        skew = pltpu.roll(jnp.broadcast_to(row, (ATT_TQ, ATT_PERIOD)), 0, 1, stride=1, stride_axis=0)
        o_ref[0, hh] = jnp.where(valid, skew[:, :ATT_TBL], NEG_INF)


def _attn_bias_table(rel_bias):
    left = ATT_WIN - ATT_TQ
    j = jnp.arange(ATT_PERIOD)
    diff = jnp.where(j < ATT_TBL, j, j - ATT_PERIOD)
    vec = rel_bias[:, jnp.clip(left - diff, -MAX_REL, MAX_REL) + MAX_REL].astype(F32)
    return pl.pallas_call(
        _attn_table_kernel,
        grid=(ATT_PAIRS,),
        in_specs=[pl.BlockSpec(vec.shape, lambda p: (0, 0))],
        out_specs=pl.BlockSpec((1, 2, ATT_TQ, ATT_TBL), lambda p: (p, 0, 0, 0)),
        out_shape=jax.ShapeDtypeStruct((ATT_PAIRS, 2, ATT_TQ, ATT_TBL), F32),
        compiler_params=_cparams("parallel"),
        name="attn_table",
    )(vec)


def _attn_sublayer(h, w_qkv, rel_bias, w_o):
    bsz, seq, d = h.shape
    assert seq % ATT_TQ == 0 and seq >= ATT_WIN
    scale = jnp.concatenate([jnp.full((d,), ATT_HEAD_DIM ** -0.5, F32), jnp.ones((2 * d,), F32)])
    qkv = _proj(h.reshape(bsz * seq, d), (w_qkv * scale).astype(BF16), BF16)
    o = _chunk_attn(qkv.reshape(bsz, seq, 3 * d), _attn_bias_table(rel_bias))
    return _proj_ln(o.reshape(bsz * seq, d), w_o.astype(BF16), h.reshape(bsz * seq, d)).reshape(bsz, seq, d)


def _mem_attn_kernel(z_ref, gin_ref, bin_ref, kv_ref, wq_ref, wo_ref, o_ref, *, sub):
    tiles = [slice(r0, r0 + sub) for r0 in range(0, z_ref.shape[1], sub)]

    def front(rows):
        h = _ln(z_ref[0, rows, :], gin_ref[...], bin_ref[...])
        q = jnp.dot(h.astype(BF16), wq_ref[...], preferred_element_type=F32).astype(BF16)
        return h, [lax.dot_general(q[:, hh * MEM_HEAD_DIM:(hh + 1) * MEM_HEAD_DIM],
                                   kv_ref[0, :, hh * MEM_HEAD_DIM:(hh + 1) * MEM_HEAD_DIM],
                                   (((1,), (1,)), ((), ())), preferred_element_type=F32) for hh in range(MEM_HEADS)]

    def back(rows, h, scores):
        outs = []
        for hh, s in enumerate(scores):
            vh = kv_ref[0, :, D_MODEL + hh * MEM_HEAD_DIM:D_MODEL + (hh + 1) * MEM_HEAD_DIM]
            m = jnp.max(s, axis=-1, keepdims=True)
            e = jnp.exp(s - m)
            l = jnp.sum(e, axis=-1, keepdims=True)
            outs.append((jnp.dot(e.astype(BF16), vh, preferred_element_type=F32) / l).astype(BF16))
        y = jnp.dot(jnp.concatenate(outs, axis=1), wo_ref[...], preferred_element_type=F32)
        o_ref[0, rows, :] = DEEPNORM_ALPHA * h + y

    nxt = front(tiles[0])
    for i, rows in enumerate(tiles):
        cur = nxt
        if i + 1 < len(tiles):
            nxt = front(tiles[i + 1])
        back(rows, *cur)


def _mem_sublayer(z, g_in, b_in, mem, w_q, w_kv, w_o, tm=1024, sub=256):
    bsz, seq, d = z.shape
    nm = mem.shape[1]
    tm = min(tm, seq)
    kv = _proj(mem.reshape(bsz * nm, d), w_kv.astype(BF16), BF16).reshape(bsz, nm, 2 * d)
    wq = (w_q * (MEM_HEAD_DIM ** -0.5)).astype(BF16)
    row = lambda b_, i: (b_, i, 0)
    fix = lambda b_, i: (0, 0)
    return pl.pallas_call(
        functools.partial(_mem_attn_kernel, sub=min(sub, tm)),
        grid=(bsz, seq // tm),
        in_specs=[pl.BlockSpec((1, tm, d), row),
                  pl.BlockSpec((1, d), fix), pl.BlockSpec((1, d), fix),
                  pl.BlockSpec((1, nm, 2 * d), lambda b_, i: (b_, 0, 0)),
                  pl.BlockSpec((d, d), fix, pipeline_mode=pl.Buffered(1)),
                  pl.BlockSpec((d, d), fix, pipeline_mode=pl.Buffered(1))],
        out_specs=pl.BlockSpec((1, tm, d), row),
        out_shape=jax.ShapeDtypeStruct((bsz, seq, d), F32),
        compiler_params=_cparams("parallel", "parallel"),
        name="mem_attn",
    )(z, g_in, b_in, kv, wq, w_o.astype(BF16))


def _mlp_kernel(z_ref, gin_ref, bin_ref, w1_ref, w2_ref, g_ref, b_ref, o_ref, *, sub):
    for r0 in range(0, z_ref.shape[0], sub):
        rows = slice(r0, r0 + sub)
        h = _ln(z_ref[rows, :], gin_ref[...], bin_ref[...])
        u = jnp.dot(h.astype(BF16), w1_ref[...], preferred_element_type=F32)
        u = jnp.maximum(u, 0.0)
        y = jnp.dot((u * u).astype(BF16), w2_ref[...], preferred_element_type=F32)
        o_ref[rows, :] = _residual_ln(h, y, g_ref[...], b_ref[...])


def _mlp_sublayer(z, g_in, b_in, w1, w2, g, b, tm=1024, sub=256):
    bsz, seq, d = z.shape
    t = bsz * seq
    tm = min(tm, t)
    ff = w1.shape[1]
    resident = pl.Buffered(1)
    out = pl.pallas_call(
        functools.partial(_mlp_kernel, sub=min(sub, tm)),
        grid=(t // tm,),
        in_specs=[pl.BlockSpec((tm, d), lambda i: (i, 0)),
                  pl.BlockSpec((1, d), lambda i: (0, 0)),
                  pl.BlockSpec((1, d), lambda i: (0, 0)),
                  pl.BlockSpec((d, ff), lambda i: (0, 0), pipeline_mode=resident),
                  pl.BlockSpec((ff, d), lambda i: (0, 0), pipeline_mode=resident),
                  pl.BlockSpec((1, d), lambda i: (0, 0)),
                  pl.BlockSpec((1, d), lambda i: (0, 0))],
        out_specs=pl.BlockSpec((tm, d), lambda i: (i, 0)),
        out_shape=jax.ShapeDtypeStruct((t, d), F32),
        compiler_params=_cparams("parallel"),
        name="mlp",
    )(z.reshape(t, d), g_in, b_in, w1.astype(BF16), w2.astype(BF16), g, b)
    return out.reshape(bsz, seq, d)


def kernel(x, mem, ln_g, ln_b, lru_w_in, lru_conv_w, lru_conv_b, lru_gate_w, lru_gate_b, lru_lambda, lru_w_out, rw_mu, rw_w_r, rw_w_k, rw_w_v, rw_w0, rw_w1, rw_w2, rw_a0, rw_a1, rw_a2, rw_g1, rw_g2, rw_k_k, rw_k_a, rw_r_k, rw_lnx_g, rw_lnx_b, rw_w_o, ca_w_qkv, ca_rel_bias, ca_w_o, mx_w_q, mx_w_kv, mx_w_o, mlp_w1, mlp_w2):
    h = x
    for i in range(DEPTH):
        kind, j = i % N_MIXERS, i // N_MIXERS
        g = lambda s: ln_g[i, s][None, :]
        b = lambda s: ln_b[i, s][None, :]
        if kind == 0:
            z = _rglru_sublayer(h, lru_w_in[j], lru_conv_w[j], lru_conv_b[j], lru_gate_w[j], lru_gate_b[j],
                                lru_lambda[j], lru_w_out[j])
        elif kind == 1:
            z = _rwkv_sublayer(h, rw_mu[j], rw_w_r[j], rw_w_k[j], rw_w_v[j], rw_w0[j], rw_w1[j], rw_w2[j],
                               rw_a0[j], rw_a1[j], rw_a2[j], rw_g1[j], rw_g2[j], rw_k_k[j], rw_k_a[j],
                               rw_r_k[j], rw_lnx_g[j], rw_lnx_b[j], rw_w_o[j])
        else:
            z = _attn_sublayer(h, ca_w_qkv[j], ca_rel_bias[j], ca_w_o[j])
        z = _mem_sublayer(z, g(0), b(0), mem, mx_w_q[i], mx_w_kv[i], mx_w_o[i])
        h = _mlp_sublayer(z, g(1), b(1), mlp_w1[i], mlp_w2[i], g(2), b(2))
    return h
```

```python
import functools

import jax
import jax.numpy as jnp
from jax import lax
from jax.experimental import pallas as pl
from jax.experimental.pallas import tpu as pltpu

F32 = jnp.float32
BF16 = jnp.bfloat16

D_MODEL = 1024
DEPTH = 4
CHUNK = 64
N_MIXERS = 3
DEEPNORM_ALPHA = (2 * DEPTH) ** 0.25
LN_EPS = 1e-5

D_RNN = 1344
LRU_BLOCKS = 16
LRU_BLOCK_SIZE = D_RNN // LRU_BLOCKS
CONV_WIDTH = 4
RG_LRU_C = 8.0
LANES = 128
SUBLANES = 8
D_RNN_PAD = -(-D_RNN // LANES) * LANES

RW_HEAD_SIZE = 64
RW_HEADS = D_MODEL // RW_HEAD_SIZE
RW_PAIRS = D_MODEL // LANES
RW_GN_EPS = 64e-5
RW_CHUNK = 64

ATT_HEADS = 16
ATT_HEAD_DIM = D_MODEL // ATT_HEADS
ATT_PAIRS = D_MODEL // LANES
BAND_CHUNKS = 9
MAX_REL = 2 * CHUNK
NEG_INF = -1e30
ATT_TQ = 512
ATT_WIN = ATT_TQ + (BAND_CHUNKS - 1) * CHUNK
ATT_TBL = ATT_WIN + (BAND_CHUNKS - 1) * CHUNK

MEM_HEADS = 4
MEM_HEAD_DIM = D_MODEL // MEM_HEADS
D_FF = 4 * D_MODEL

VMEM_LIMIT = 56 * 1024 * 1024


def _cparams(*sem):
    return pltpu.CompilerParams(dimension_semantics=sem, vmem_limit_bytes=VMEM_LIMIT)


def _residual_ln(hres, y, g, b):
    return _ln(DEEPNORM_ALPHA * hres + y, g, b)


def _ln(z, g, b):
    mu = jnp.mean(z, axis=-1, keepdims=True)
    zc = z - mu
    var = jnp.mean(zc * zc, axis=-1, keepdims=True)
    return zc * lax.rsqrt(var + LN_EPS) * g + b


def _gelu_tanh(x):
    return 0.5 * x * (1.0 + jnp.tanh(0.7978845608028654 * (x + 0.044715 * (x * x * x))))


def _softplus(x, wide=False):
    t = jnp.exp(-jnp.abs(x))
    return jnp.maximum(x, 0.0) + (jnp.log(1.0 + t) if wide else jnp.log1p(t))


def _sigmoid(x):
    return 0.5 * jnp.tanh(0.5 * x) + 0.5


def _split2(x):
    hi = x.astype(BF16)
    lo = (x - hi.astype(F32)).astype(BF16)
    return hi, lo


def _proj_kernel(x_ref, w_ref, o_ref, *, chunk):
    x = x_ref[...].astype(BF16)
    n = w_ref.shape[1]
    for n0 in range(0, n, chunk):
        y = jnp.dot(x, w_ref[:, n0:n0 + chunk], preferred_element_type=F32)
        o_ref[:, n0:n0 + chunk] = y.astype(o_ref.dtype)


def _proj(x2d, w, out_dtype, tm=512, chunk=512):
    t, k = x2d.shape
    n = w.shape[1]
    tm = min(tm, t)
    return pl.pallas_call(
        functools.partial(_proj_kernel, chunk=chunk),
        grid=(t // tm,),
        in_specs=[pl.BlockSpec((tm, k), lambda i: (i, 0)),
                  pl.BlockSpec((k, n), lambda i: (0, 0))],
        out_specs=pl.BlockSpec((tm, n), lambda i: (i, 0)),
        out_shape=jax.ShapeDtypeStruct((t, n), out_dtype),
        compiler_params=_cparams("parallel"),
        name="proj",
    )(x2d, w)


LRU_GATE_TILE = 256
LRU_GATE_WIN = 512


def _lru_gate_window_starts():
    starts = []
    for n0 in range(0, D_RNN_PAD, LRU_GATE_TILE):
        first_block = min(n0, D_RNN - 1) // LRU_BLOCK_SIZE
        last_block = min(n0 + LRU_GATE_TILE - 1, D_RNN - 1) // LRU_BLOCK_SIZE
        lo = (first_block * LRU_BLOCK_SIZE) // LANES * LANES
        lo = min(lo, D_RNN_PAD - LRU_GATE_WIN)
        assert (last_block + 1) * LRU_BLOCK_SIZE <= lo + LRU_GATE_WIN
        starts.append(lo)
    return starts


LRU_HIST = (CONV_WIDTH - 1) * SUBLANES


def _lru_kernel(h_ref, win_ref, cw_ref, cb_ref, gw_ref, gb_ref, lam_ref, wout_ref, o_ref, xbuf, hstate):
    @pl.when(pl.program_id(0) == 0)
    def _():
        xbuf[:, 0:LRU_HIST, :] = jnp.zeros((xbuf.shape[0], LRU_HIST, xbuf.shape[2]), F32)
        hstate[...] = jnp.zeros(hstate.shape, F32)

    ng = h_ref.shape[0] // SUBLANES
    fronts = [_lru_front(gi, h_ref, win_ref, cw_ref, cb_ref, gw_ref, xbuf) for gi in range(ng)]
    for gi in range(ng):
        _lru_back(gi, *fronts[gi], gb_ref, lam_ref, wout_ref, o_ref, hstate)


def _lru_front(gi, h_ref, win_ref, cw_ref, cb_ref, gw_ref, xbuf):
    tt, d = h_ref.shape[1], h_ref.shape[2]
    c = cw_ref.shape[1]
    ts = tt * SUBLANES
    hres = h_ref[gi * SUBLANES:(gi + 1) * SUBLANES]
    ht = pltpu.einshape("btd->tbd", hres).reshape(ts, d)
    u = jnp.dot(ht.astype(BF16), win_ref[...], preferred_element_type=F32)
    gate_branch = _gelu_tanh(u[:, :c])
    x = u[:, c:]
    xbuf[gi, LRU_HIST:LRU_HIST + ts, :] = x
    xr = cb_ref[...] + cw_ref[3:4, :] * x
    for k in range(CONV_WIDTH - 1):
        xr = xr + cw_ref[k:k + 1, :] * xbuf[gi, k * SUBLANES:k * SUBLANES + ts, :]
    xbuf[gi, 0:LRU_HIST, :] = xbuf[gi, ts:ts + LRU_HIST, :]

    xb = xr.astype(BF16)
    tiles = [jnp.dot(xb[:, lo:lo + LRU_GATE_WIN], gw_ref[n], preferred_element_type=F32)
             for n, lo in enumerate(_lru_gate_window_starts())]
    r_pre = jnp.concatenate([t[:, :LRU_GATE_TILE] for t in tiles], axis=1)[:, :c]
    i_pre = jnp.concatenate([t[:, LRU_GATE_TILE:] for t in tiles], axis=1)[:, :c]
    return hres, gate_branch, xr, r_pre, i_pre


def _lru_back(gi, hres, gate_branch, xr, r_pre, i_pre, gb_ref, lam_ref, wout_ref, o_ref, hstate):
    ts, c = xr.shape
    tt = ts // SUBLANES
    r_gate = _sigmoid(r_pre + gb_ref[0:1, :])
    i_gate = _sigmoid(i_pre + gb_ref[1:2, :])
    log_a = (-RG_LRU_C) * r_gate * _softplus(-lam_ref[...])
    a = jnp.exp(log_a)
    bt = jnp.sqrt(-jnp.tanh(log_a) * (a * a + 1.0)) * (i_gate * xr)

    h = hstate[gi]
    steps = []
    for t in range(tt):
        slab = slice(t * SUBLANES, (t + 1) * SUBLANES)
        h = a[slab, :] * h + bt[slab, :]
        steps.append(h)
    hstate[gi] = h
    hs = jnp.concatenate(steps, axis=0)

    y = jnp.dot((hs * gate_branch).astype(BF16), wout_ref[...], preferred_element_type=F32)
    yb = pltpu.einshape("tbd->btd", y.reshape(tt, SUBLANES, y.shape[1]))
    o_ref[gi * SUBLANES:(gi + 1) * SUBLANES] = DEEPNORM_ALPHA * hres + yb


def _lru(h, win, cw, cb, gw, gb, lam, wout, tt=32):
    bsz, seq, d = h.shape
    c = cw.shape[1]
    assert bsz % SUBLANES == 0
    tt = min(tt, seq)
    ng = bsz // SUBLANES
    resident = lambda w: pl.BlockSpec(w.shape, lambda j: (0,) * w.ndim, pipeline_mode=pl.Buffered(1))
    return pl.pallas_call(
        _lru_kernel,
        grid=(seq // tt,),
        in_specs=[pl.BlockSpec((bsz, tt, d), lambda j: (0, j, 0))]
                 + [resident(w) for w in (win, cw, cb, gw, gb, lam, wout)],
        out_specs=pl.BlockSpec((bsz, tt, d), lambda j: (0, j, 0)),
        out_shape=jax.ShapeDtypeStruct((bsz, seq, d), F32),
        scratch_shapes=[pltpu.VMEM((ng, LRU_HIST + tt * SUBLANES, c), F32), pltpu.VMEM((ng, SUBLANES, c), F32)],
        compiler_params=_cparams("arbitrary"),
        name="lru",
    )(h, win, cw, cb, gw, gb, lam, wout)


def _pad_last(x, n):
    return jnp.pad(x, [(0, 0)] * (x.ndim - 1) + [(0, n - x.shape[-1])])


def _rglru_sublayer(h, w_in, conv_w, conv_b, gate_w, gate_b, lam, w_out):
    bsz, seq, d = h.shape
    c = D_RNN_PAD
    w_cat = jnp.concatenate([_pad_last(w_in[:, :D_RNN], c), _pad_last(w_in[:, D_RNN:], c)], axis=1).astype(BF16)
    eye = jnp.eye(LRU_BLOCKS, dtype=F32)
    dense = jnp.einsum('gncd,nm->gncmd', gate_w, eye).reshape(2, D_RNN, D_RNN)
    starts = _lru_gate_window_starts()
    dense = jnp.pad(dense, ((0, 0), (0, c - D_RNN), (0, len(starts) * LRU_GATE_TILE - D_RNN)))
    gw = jnp.stack([jnp.concatenate([dense[gi, lo:lo + LRU_GATE_WIN, n * LRU_GATE_TILE:(n + 1) * LRU_GATE_TILE]
                                     for gi in range(2)], axis=1) for n, lo in enumerate(starts)]).astype(BF16)
    wout = jnp.pad(w_out, ((0, c - D_RNN), (0, 0))).astype(BF16)
    return _lru(h, w_cat, _pad_last(conv_w, c), _pad_last(conv_b[None, :], c), gw,
                _pad_last(gate_b, c), _pad_last(lam[None, :], c), wout)


def _rwkv_in_kernel(x_ref, xp_ref, mu_ref, wr_ref, wk_ref, wv_ref, w0_ref, w1_ref, w2_ref,
                    a0_ref, a1_ref, a2_ref, g1_ref, g2_ref,
                    r_ref, k_ref, v_ref, ld_ref, a_ref, g_ref, *, sub):
    i = pl.program_id(1)
    x_all = x_ref[0]
    row = lax.broadcasted_iota(jnp.int32, x_all.shape, 0)
    prev = jnp.where(i > 0, xp_ref[0, SUBLANES - 1:SUBLANES, :], 0.0)
    xx_all = jnp.where(row == 0, prev, pltpu.roll(x_all, 1, 0)) - x_all
    dot = lambda p, q: jnp.dot(p, q, preferred_element_type=F32)
    for r0 in range(0, x_all.shape[0], sub):
        rows = slice(r0, r0 + sub)
        x, xx = x_all[rows, :], xx_all[rows, :]
        mix = lambda c: (x + xx * mu_ref[c:c + 1, :]).astype(BF16)
        r_ref[0, rows, :] = dot(mix(0), wr_ref[...]).astype(r_ref.dtype)
        k_ref[0, rows, :] = dot(mix(2), wk_ref[...]).astype(k_ref.dtype)
        v_ref[0, rows, :] = dot(mix(3), wv_ref[...]).astype(v_ref.dtype)
        wl = w0_ref[...] + dot(jnp.tanh(dot(mix(1), w1_ref[...])).astype(BF16), w2_ref[...])
        w_log = -_softplus(-wl, wide=True) - 0.5
        ld_ref[0, rows, :] = -jnp.exp(w_log)
        a_ref[0, rows, :] = _sigmoid(a0_ref[...] + dot(dot(mix(4), a1_ref[...]).astype(BF16), a2_ref[...]))
        g_ref[0, rows, :] = dot(_sigmoid(dot(mix(5), g1_ref[...])).astype(BF16), g2_ref[...]).astype(g_ref.dtype)


def _rwkv_in(h, mu, wr, wk, wv, w0, w1, w2, a0, a1, a2, g1, g2, tm=512, sub=256):
    bsz, seq, d = h.shape
    tm = min(tm, seq)
    row = lambda b, i: (b, i, 0)
    fix = lambda b, i: (0, 0)
    prev = lambda b, i: (b, jnp.maximum(i * (tm // SUBLANES) - 1, 0), 0)
    ws = [mu, wr, wk, wv, w0, w1, w2, a0, a1, a2, g1, g2]
    return pl.pallas_call(
        functools.partial(_rwkv_in_kernel, sub=min(sub, tm)),
        grid=(bsz, seq // tm),
        in_specs=[pl.BlockSpec((1, tm, d), row), pl.BlockSpec((1, SUBLANES, d), prev)]
                 + [pl.BlockSpec(w.shape, fix) for w in ws],
        out_specs=[pl.BlockSpec((1, tm, d), row)] * 6,
        out_shape=[jax.ShapeDtypeStruct((bsz, seq, d), dt) for dt in (BF16, BF16, BF16, F32, F32, BF16)],
        compiler_params=_cparams("parallel", "parallel"),
        name="rwkv_in",
    )(h, h, *ws)


def _bdot(a, b, dims):
    return jnp.einsum(dims, a, b, preferred_element_type=F32)


def _rwkv_rec_kernel(r_ref, k_ref, v_ref, ld_ref, a_ref, g_ref, kk_ref, ka_ref, rk_ref, lg_ref, lb_ref,
                     o_ref, state):
    @pl.when(pl.program_id(1) == 0)
    def _():
        state[...] = jnp.zeros(state.shape, F32)

    L = RW_CHUNK
    P = RW_PAIRS
    N = RW_HEAD_SIZE
    nc = r_ref.shape[1] // L
    B = nc * P
    cp = [(c, p) for c in range(nc) for p in range(P)]
    pairs = lambda ref: jnp.stack([ref[0, c * L:(c + 1) * L, LANES * p:LANES * (p + 1)]
                                   for c, p in cp]).astype(F32)
    vec = lambda ref: jnp.stack([ref[:, LANES * p:LANES * (p + 1)] for _, p in cp])

    lane = lax.broadcasted_iota(jnp.int32, (1, 1, LANES), 2)
    head0 = lane < N
    ri = lax.broadcasted_iota(jnp.int32, (LANES, LANES), 0)
    ci = lax.broadcasted_iota(jnp.int32, (LANES, LANES), 1)
    seg = ((ri // N) == (ci // N)).astype(BF16)

    def segsum(x, terms=2):
        flat = x.reshape(B * L, LANES)
        hi = flat.astype(BF16)
        out = jnp.dot(hi, seg, preferred_element_type=F32)
        if terms == 2:
            lo = (flat - hi.astype(F32)).astype(BF16)
            out = out + jnp.dot(lo, seg, preferred_element_type=F32)
        return out.reshape(B, L, LANES)

    r = pairs(r_ref)
    k = pairs(k_ref)
    v = pairs(v_ref)
    a = pairs(a_ref)

    kk = k * vec(kk_ref)
    kk = kk * lax.rsqrt(jnp.maximum(segsum(kk * kk, terms=1), 1e-24))
    k = k * (1.0 + (a - 1.0) * vec(ka_ref))
    aa = -kk
    bb = kk * a

    ld_full = ld_ref[0]
    tr = lax.broadcasted_iota(jnp.int32, (nc * L, nc * L), 0)
    tc = lax.broadcasted_iota(jnp.int32, (nc * L, nc * L), 1)
    tri = ((tc <= tr) & (tc // L == tr // L)).astype(BF16)
    cum_full = sum(jnp.dot(tri, t, preferred_element_type=F32) for t in _split2(ld_full))
    cum = jnp.stack([cum_full[c * L:(c + 1) * L, LANES * p:LANES * (p + 1)] for c, p in cp])
    ld = pairs(ld_ref)
    gam = jnp.exp(cum)
    gam_prev = jnp.exp(cum - ld)
    cum_end = cum[:, L - 1:L, :]
    r_t = r * gam
    a_t = aa * gam_prev
    k_t = k * jnp.exp(-cum)
    b_t = bb * jnp.exp(-cum)
    k_e = k * jnp.exp(cum_end - cum)
    b_e = bb * jnp.exp(cum_end - cum)

    def stacked(x):
        return jnp.concatenate([jnp.where(head0, x, 0.0), jnp.where(head0, 0.0, x)], axis=1)

    ar = jnp.concatenate([stacked(a_t), stacked(r_t)], axis=1).astype(BF16)
    kb = jnp.concatenate([stacked(k_t), stacked(b_t)], axis=1).astype(BF16)
    v_s = stacked(v)

    s4 = _bdot(ar, kb, 'pik,pjk->pij')
    M = 2 * L
    si = lax.broadcasted_iota(jnp.int32, (1, M, M), 1)
    sj = lax.broadcasted_iota(jnp.int32, (1, M, M), 2)
    same = (si // L) == (sj // L)
    strict = same & ((sj % L) < (si % L))
    incl = same & ((sj % L) <= (si % L))
    a_k = jnp.where(strict, s4[:, :M, :M], 0.0)
    a_b = jnp.where(strict, s4[:, :M, M:], 0.0)
    r_k = jnp.where(incl, s4[:, M:, :M], 0.0)
    r_b = jnp.where(incl, s4[:, M:, M:], 0.0)

    eye = (si == sj).astype(F32)
    lower1 = (si // 2 == sj // 2) & (si % 2 == 1) & (sj % 2 == 0)
    T = eye + jnp.where(lower1, a_b, 0.0)
    ab16 = a_b.astype(BF16)
    s = 2
    hi_ = lax.broadcasted_iota(jnp.int32, (1, M // 2, M), 1)
    hj_ = lax.broadcasted_iota(jnp.int32, (1, M // 2, M), 2)
    while s < L:
        tb = T.astype(BF16)
        if s < SUBLANES:
            cm = (si // (2 * s) == sj // (2 * s)) & (si % (2 * s) >= s) & (sj % (2 * s) < s)
            tat = _bdot(_bdot(tb, ab16, 'pij,pjk->pik').astype(BF16), tb, 'pij,pjk->pik')
            T = T + jnp.where(cm, tat, 0.0)
        else:
            nblk = M // (2 * s)
            low = jnp.concatenate([T[:, (2 * k + 1) * s:(2 * k + 2) * s, :] for k in range(nblk)], axis=1)
            upd = _bdot(_bdot(low.astype(BF16), ab16, 'pij,pjk->pik').astype(BF16), tb, 'pij,pjk->pik')
            upd = jnp.where((hj_ // (2 * s) == hi_ // s) & (hj_ % (2 * s) < s), upd, 0.0)
            zero = jnp.zeros((B, s, M), F32)
            T = T + jnp.concatenate([blk for k in range(nblk) for blk in (zero, upd[:, k * s:(k + 1) * s, :])],
                                    axis=1)
        s *= 2

    v16 = v_s.astype(BF16)
    akv = _bdot(a_k.astype(BF16), v16, 'pij,pjl->pil')
    t16 = T.astype(BF16)
    rkb = jnp.concatenate([r_k, r_b], axis=2).astype(BF16)
    kb_e = jnp.concatenate([stacked(k_e), stacked(b_e)], axis=1).astype(BF16)
    gam_end = jnp.exp(cum_end)

    s_val = state[...]
    y_parts = []
    for c in range(nc):
        sl = slice(c * P, (c + 1) * P)
        ah = _bdot(ar[sl], s_val.astype(BF16), 'pik,plk->pil')
        x_s = ah[:, :M, :] + akv[sl]
        u_s = _bdot(t16[sl], x_s.astype(BF16), 'pij,pjl->pil')
        vu = jnp.concatenate([v16[sl], u_s.astype(BF16)], axis=1)
        y_parts.append(ah[:, M:, :] + _bdot(rkb[sl], vu, 'pij,pjl->pil'))
        s_val = s_val * gam_end[sl] + _bdot(vu, kb_e[sl], 'pti,ptj->pij')
    state[...] = s_val
    y_s = jnp.concatenate(y_parts, axis=0)
    y = y_s[:, :L, :] + y_s[:, L:, :]

    mean = segsum(y) * (1.0 / N)
    yc = y - mean
    var = segsum(yc * yc, terms=1) * (1.0 / N)
    yn = yc * lax.rsqrt(var + RW_GN_EPS) * vec(lg_ref) + vec(lb_ref)
    bonus = segsum(r * k * vec(rk_ref)) * v
    out = (yn + bonus) * pairs(g_ref)
    for i, (c, p) in enumerate(cp):
        o_ref[0, c * L:(c + 1) * L, LANES * p:LANES * (p + 1)] = out[i].astype(o_ref.dtype)


def _rwkv_rec(r, k, v, ld, a, g, k_k, k_a, r_k, lnx_g, lnx_b, chunks_per_step=4):
    bsz, seq, d = r.shape
    L = RW_CHUNK * min(chunks_per_step, seq // RW_CHUNK)
    row = lambda b, j: (b, j, 0)
    fix = lambda b, j: (0, 0)
    return pl.pallas_call(
        _rwkv_rec_kernel,
        grid=(bsz, seq // L),
        in_specs=[pl.BlockSpec((1, L, d), row)] * 6 + [pl.BlockSpec((1, d), fix)] * 5,
        out_specs=pl.BlockSpec((1, L, d), row),
        out_shape=jax.ShapeDtypeStruct((bsz, seq, d), BF16),
        scratch_shapes=[pltpu.VMEM((RW_PAIRS, LANES, LANES), F32)],
        compiler_params=_cparams("parallel", "arbitrary"),
        name="rwkv_rec",
    )(r, k, v, ld, a, g, k_k, k_a, r_k, lnx_g, lnx_b)


def _rwkv_sublayer(h, mu, w_r, w_k, w_v, w0, w1, w2, a0, a1, a2, g1, g2, k_k, k_a, r_k,
                   lnx_g, lnx_b, w_o):
    bsz, seq, d = h.shape
    bf = lambda w: w.astype(BF16)
    row = lambda w: w.reshape(1, d)
    mu8 = jnp.pad(mu, ((0, SUBLANES - mu.shape[0]), (0, 0)))
    r, k, v, ld, a, gg = _rwkv_in(h, mu8, bf(w_r), bf(w_k), bf(w_v), row(w0), bf(w1), bf(w2),
                                  row(a0), bf(a1), bf(a2), bf(g1), bf(g2))
    y = _rwkv_rec(r, k, v, ld, a, gg, row(k_k), row(k_a), row(r_k), row(lnx_g), row(lnx_b))
    return y, bf(w_o), h


def _chunk_attn_kernel(q_ref, k_ref, v_ref, tbl_ref, o_ref):
    seq = q_ref.shape[1]
    blocks_back = (ATT_WIN - ATT_TQ) // ATT_TQ
    lane = lax.broadcasted_iota(jnp.int32, (ATT_TQ, LANES), 1)
    nq = seq // ATT_TQ

    def window(qi):
        back = min(qi, blocks_back)
        ws = (qi - back) * ATT_TQ
        width = (back + 1) * ATT_TQ
        off = (blocks_back - back) * ATT_TQ
        return ws, width, off

    head = lax.broadcasted_iota(jnp.int32, (2, 1, LANES), 0)
    own = (lax.broadcasted_iota(jnp.int32, (2, 1, LANES), 2) // ATT_HEAD_DIM) == head
    for qi in range(nq):
        ws, width, off = window(qi)
        q = q_ref[0, qi * ATT_TQ:(qi + 1) * ATT_TQ, :]
        kw = k_ref[0, ws:ws + width, :]
        vw = v_ref[0, ws:ws + width, :]
        qh = jnp.where(own, q[None], jnp.zeros_like(q)[None])
        s = jnp.einsum('hqd,hkd->hqk', qh, jnp.broadcast_to(kw[None], (2,) + kw.shape),
                       preferred_element_type=F32)
        s = s + tbl_ref[0, :, :, off:off + width]
        m = jnp.max(s, axis=-1, keepdims=True)
        e = jnp.exp((s - m).astype(BF16))
        vh = jnp.where(own, vw[None], jnp.ones_like(vw)[None])
        pv = jnp.einsum('hqk,hkd->hqd', e, vh, preferred_element_type=F32)
        num = jnp.where(lane < ATT_HEAD_DIM, pv[0], pv[1])
        den = jnp.where(lane < ATT_HEAD_DIM, pltpu.roll(pv[0], ATT_HEAD_DIM, 1), pltpu.roll(pv[1], ATT_HEAD_DIM, 1))
        o_ref[0, qi * ATT_TQ:(qi + 1) * ATT_TQ, :] = (num / den).astype(o_ref.dtype)


def _chunk_attn(qkv, tbl):
    bsz, seq, _ = qkv.shape
    return pl.pallas_call(
        _chunk_attn_kernel,
        grid=(bsz, ATT_PAIRS),
        in_specs=[pl.BlockSpec((1, seq, LANES), lambda b, p: (b, 0, p)),
                  pl.BlockSpec((1, seq, LANES), lambda b, p: (b, 0, ATT_PAIRS + p)),
                  pl.BlockSpec((1, seq, LANES), lambda b, p: (b, 0, 2 * ATT_PAIRS + p)),
                  pl.BlockSpec((1, 2, ATT_TQ, ATT_TBL), lambda b, p: (p, 0, 0, 0))],
        out_specs=pl.BlockSpec((1, seq, LANES), lambda b, p: (b, 0, p)),
        out_shape=jax.ShapeDtypeStruct((bsz, seq, D_MODEL), BF16),
        compiler_params=_cparams("parallel", "parallel"),
        name="chunk_attn",
    )(qkv, qkv, qkv, tbl)


ATT_PERIOD = ATT_TBL + ATT_TQ


def _attn_table_kernel(vec_ref, o_ref):
    p = pl.program_id(0)
    ic = lax.broadcasted_iota(jnp.int32, (ATT_TQ, ATT_TBL), 0) // CHUNK
    mc = lax.broadcasted_iota(jnp.int32, (ATT_TQ, ATT_TBL), 1) // CHUNK
    valid = (mc >= ic) & (mc <= ic + BAND_CHUNKS - 1)
    for hh in range(2):
        row = vec_ref[pl.ds(2 * p + hh, 1), :]
        skew = pltpu.roll(jnp.broadcast_to(row, (ATT_TQ, ATT_PERIOD)), 0, 1, stride=1, stride_axis=0)
        o_ref[0, hh] = jnp.where(valid, skew[:, :ATT_TBL], NEG_INF)


def _attn_bias_table(rel_bias):
    left = ATT_WIN - ATT_TQ
    j = jnp.arange(ATT_PERIOD)
    diff = jnp.where(j < ATT_TBL, j, j - ATT_PERIOD)
    vec = rel_bias[:, jnp.clip(left - diff, -MAX_REL, MAX_REL) + MAX_REL].astype(F32)
    return pl.pallas_call(
        _attn_table_kernel,
        grid=(ATT_PAIRS,),
        in_specs=[pl.BlockSpec(vec.shape, lambda p: (0, 0))],
        out_specs=pl.BlockSpec((1, 2, ATT_TQ, ATT_TBL), lambda p: (p, 0, 0, 0)),
        out_shape=jax.ShapeDtypeStruct((ATT_PAIRS, 2, ATT_TQ, ATT_TBL), F32),
        compiler_params=_cparams("parallel"),
        name="attn_table",
    )(vec)


def _attn_sublayer(h, w_qkv, rel_bias, w_o):
    bsz, seq, d = h.shape
    assert seq % ATT_TQ == 0 and seq >= ATT_WIN
    scale = jnp.concatenate([jnp.full((d,), ATT_HEAD_DIM ** -0.5, F32), jnp.ones((2 * d,), F32)])
    qkv = _proj(h.reshape(bsz * seq, d), (w_qkv * scale).astype(BF16), BF16)
    o = _chunk_attn(qkv.reshape(bsz, seq, 3 * d), _attn_bias_table(rel_bias))
    return o, w_o.astype(BF16), h


def _mem_attn_kernel(*refs, sub, pre_proj):
    if pre_proj:
        y_ref, wpre_ref, x_ref, gin_ref, bin_ref, kv_ref, wq_ref, wo_ref, o_ref = refs
    else:
        z_ref, gin_ref, bin_ref, kv_ref, wq_ref, wo_ref, o_ref = refs
    tiles = [slice(r0, r0 + sub) for r0 in range(0, o_ref.shape[1], sub)]

    def front(rows):
        if pre_proj:
            z = DEEPNORM_ALPHA * x_ref[0, rows, :] + jnp.dot(y_ref[0, rows, :], wpre_ref[...],
                                                             preferred_element_type=F32)
        else:
            z = z_ref[0, rows, :]
        h = _ln(z, gin_ref[...], bin_ref[...])
        q = jnp.dot(h.astype(BF16), wq_ref[...], preferred_element_type=F32).astype(BF16)
        return h, [lax.dot_general(q[:, hh * MEM_HEAD_DIM:(hh + 1) * MEM_HEAD_DIM],
                                   kv_ref[0, :, hh * MEM_HEAD_DIM:(hh + 1) * MEM_HEAD_DIM],
                                   (((1,), (1,)), ((), ())), preferred_element_type=F32) for hh in range(MEM_HEADS)]

    def back(rows, h, scores):
        outs = []
        for hh, s in enumerate(scores):
            vh = kv_ref[0, :, D_MODEL + hh * MEM_HEAD_DIM:D_MODEL + (hh + 1) * MEM_HEAD_DIM]
            m = jnp.max(s, axis=-1, keepdims=True)
            e = jnp.exp(s - m)
            l = jnp.sum(e, axis=-1, keepdims=True)
            outs.append((jnp.dot(e.astype(BF16), vh, preferred_element_type=F32) / l).astype(BF16))
        y = jnp.dot(jnp.concatenate(outs, axis=1), wo_ref[...], preferred_element_type=F32)
        o_ref[0, rows, :] = DEEPNORM_ALPHA * h + y

    nxt = front(tiles[0])
    for i, rows in enumerate(tiles):
        cur = nxt
        if i + 1 < len(tiles):
            nxt = front(tiles[i + 1])
        back(rows, *cur)


def _mem_sublayer(mixed, g_in, b_in, mem, w_q, w_kv, w_o, tm=1024, sub=256):
    pre_proj = isinstance(mixed, tuple)
    bsz, seq, d = mixed[2].shape if pre_proj else mixed.shape
    nm = mem.shape[1]
    tm = min(tm, seq)
    kv = _proj(mem.reshape(bsz * nm, d), w_kv.astype(BF16), BF16).reshape(bsz, nm, 2 * d)
    wq = (w_q * (MEM_HEAD_DIM ** -0.5)).astype(BF16)
    row = lambda b_, i: (b_, i, 0)
    fix = lambda b_, i: (0, 0)
    if pre_proj:
        y, w_pre, x = mixed
        lead = [y, w_pre, x]
        lead_specs = [pl.BlockSpec((1, tm, y.shape[-1]), row),
                      pl.BlockSpec(w_pre.shape, fix, pipeline_mode=pl.Buffered(1)),
                      pl.BlockSpec((1, tm, d), row)]
    else:
        lead, lead_specs = [mixed], [pl.BlockSpec((1, tm, d), row)]
    return pl.pallas_call(
        functools.partial(_mem_attn_kernel, sub=min(sub, tm), pre_proj=pre_proj),
        grid=(bsz, seq // tm),
        in_specs=lead_specs + [
                  pl.BlockSpec((1, d), fix), pl.BlockSpec((1, d), fix),
                  pl.BlockSpec((1, nm, 2 * d), lambda b_, i: (b_, 0, 0)),
                  pl.BlockSpec((d, d), fix, pipeline_mode=pl.Buffered(1)),
                  pl.BlockSpec((d, d), fix, pipeline_mode=pl.Buffered(1))],
        out_specs=pl.BlockSpec((1, tm, d), row),
        out_shape=jax.ShapeDtypeStruct((bsz, seq, d), F32),
        compiler_params=_cparams("parallel", "parallel"),
        name="mem_attn",
    )(*lead, g_in, b_in, kv, wq, w_o.astype(BF16))


def _mlp_kernel(z_ref, gin_ref, bin_ref, w1_ref, w2_ref, g_ref, b_ref, o_ref, *, sub):
    for r0 in range(0, z_ref.shape[0], sub):
        rows = slice(r0, r0 + sub)
        h = _ln(z_ref[rows, :], gin_ref[...], bin_ref[...])
        u = jnp.dot(h.astype(BF16), w1_ref[...], preferred_element_type=F32)
        u = jnp.maximum(u, 0.0)
        y = jnp.dot((u * u).astype(BF16), w2_ref[...], preferred_element_type=F32)
        o_ref[rows, :] = _residual_ln(h, y, g_ref[...], b_ref[...])


def _mlp_sublayer(z, g_in, b_in, w1, w2, g, b, tm=1024, sub=256):
    bsz, seq, d = z.shape
    t = bsz * seq
    tm = min(tm, t)
    ff = w1.shape[1]
    resident = pl.Buffered(1)
    out = pl.pallas_call(
        functools.partial(_mlp_kernel, sub=min(sub, tm)),
        grid=(t // tm,),
        in_specs=[pl.BlockSpec((tm, d), lambda i: (i, 0)),
                  pl.BlockSpec((1, d), lambda i: (0, 0)),
                  pl.BlockSpec((1, d), lambda i: (0, 0)),
                  pl.BlockSpec((d, ff), lambda i: (0, 0), pipeline_mode=resident),
                  pl.BlockSpec((ff, d), lambda i: (0, 0), pipeline_mode=resident),
                  pl.BlockSpec((1, d), lambda i: (0, 0)),
                  pl.BlockSpec((1, d), lambda i: (0, 0))],
        out_specs=pl.BlockSpec((tm, d), lambda i: (i, 0)),
        out_shape=jax.ShapeDtypeStruct((t, d), F32),
        compiler_params=_cparams("parallel"),
        name="mlp",
    )(z.reshape(t, d), g_in, b_in, w1.astype(BF16), w2.astype(BF16), g, b)
    return out.reshape(bsz, seq, d)


def kernel(x, mem, ln_g, ln_b, lru_w_in, lru_conv_w, lru_conv_b, lru_gate_w, lru_gate_b, lru_lambda, lru_w_out, rw_mu, rw_w_r, rw_w_k, rw_w_v, rw_w0, rw_w1, rw_w2, rw_a0, rw_a1, rw_a2, rw_g1, rw_g2, rw_k_k, rw_k_a, rw_r_k, rw_lnx_g, rw_lnx_b, rw_w_o, ca_w_qkv, ca_rel_bias, ca_w_o, mx_w_q, mx_w_kv, mx_w_o, mlp_w1, mlp_w2):
    h = x
    for i in range(DEPTH):
        kind, j = i % N_MIXERS, i // N_MIXERS
        g = lambda s: ln_g[i, s][None, :]
        b = lambda s: ln_b[i, s][None, :]
        if kind == 0:
            z = _rglru_sublayer(h, lru_w_in[j], lru_conv_w[j], lru_conv_b[j], lru_gate_w[j], lru_gate_b[j],
                                lru_lambda[j], lru_w_out[j])
        elif kind == 1:
            z = _rwkv_sublayer(h, rw_mu[j], rw_w_r[j], rw_w_k[j], rw_w_v[j], rw_w0[j], rw_w1[j], rw_w2[j],
                               rw_a0[j], rw_a1[j], rw_a2[j], rw_g1[j], rw_g2[j], rw_k_k[j], rw_k_a[j],
                               rw_r_k[j], rw_lnx_g[j], rw_lnx_b[j], rw_w_o[j])
        else:
            z = _attn_sublayer(h, ca_w_qkv[j], ca_rel_bias[j], ca_w_o[j])
        z = _mem_sublayer(z, g(0), b(0), mem, mx_w_q[i], mx_w_kv[i], mx_w_o[i])
        h = _mlp_sublayer(z, g(1), b(1), mlp_w1[i], mlp_w2[i], g(2), b(2))
    return h
```

```python
import functools

import jax
import jax.numpy as jnp
from jax import lax
from jax.experimental import pallas as pl
from jax.experimental.pallas import tpu as pltpu

F32 = jnp.float32
BF16 = jnp.bfloat16

D_MODEL = 1024
DEPTH = 4
CHUNK = 64
N_MIXERS = 3
DEEPNORM_ALPHA = (2 * DEPTH) ** 0.25
LN_EPS = 1e-5

D_RNN = 1344
LRU_BLOCKS = 16
LRU_BLOCK_SIZE = D_RNN // LRU_BLOCKS
CONV_WIDTH = 4
RG_LRU_C = 8.0
LANES = 128
SUBLANES = 8
D_RNN_PAD = -(-D_RNN // LANES) * LANES

RW_HEAD_SIZE = 64
RW_HEADS = D_MODEL // RW_HEAD_SIZE
RW_PAIRS = D_MODEL // LANES
RW_GN_EPS = 64e-5
RW_CHUNK = 64

ATT_HEADS = 16
ATT_HEAD_DIM = D_MODEL // ATT_HEADS
ATT_PAIRS = D_MODEL // LANES
BAND_CHUNKS = 9
MAX_REL = 2 * CHUNK
NEG_INF = -1e30
ATT_TQ = 512
ATT_WIN = ATT_TQ + (BAND_CHUNKS - 1) * CHUNK
ATT_TBL = ATT_WIN + (BAND_CHUNKS - 1) * CHUNK

MEM_HEADS = 4
MEM_HEAD_DIM = D_MODEL // MEM_HEADS
D_FF = 4 * D_MODEL

VMEM_LIMIT = 56 * 1024 * 1024


def _cparams(*sem):
    return pltpu.CompilerParams(dimension_semantics=sem, vmem_limit_bytes=VMEM_LIMIT)


def _residual_ln(hres, y, g, b):
    return _ln(DEEPNORM_ALPHA * hres + y, g, b)


def _ln(z, g, b):
    mu = jnp.mean(z, axis=-1, keepdims=True)
    zc = z - mu
    var = jnp.mean(zc * zc, axis=-1, keepdims=True)
    return zc * lax.rsqrt(var + LN_EPS) * g + b


def _gelu_tanh(x):
    return 0.5 * x * (1.0 + jnp.tanh(0.7978845608028654 * (x + 0.044715 * (x * x * x))))


def _softplus(x, wide=False):
    t = jnp.exp(-jnp.abs(x))
    return jnp.maximum(x, 0.0) + (jnp.log(1.0 + t) if wide else jnp.log1p(t))


def _sigmoid(x):
    return 0.5 * jnp.tanh(0.5 * x) + 0.5


def _split2(x):
    hi = x.astype(BF16)
    lo = (x - hi.astype(F32)).astype(BF16)
    return hi, lo


def _proj_kernel(x_ref, w_ref, o_ref, *, chunk):
    x = x_ref[...].astype(BF16)
    n = w_ref.shape[1]
    for n0 in range(0, n, chunk):
        y = jnp.dot(x, w_ref[:, n0:n0 + chunk], preferred_element_type=F32)
        o_ref[:, n0:n0 + chunk] = y.astype(o_ref.dtype)


def _proj(x2d, w, out_dtype, tm=512, chunk=512):
    t, k = x2d.shape
    n = w.shape[1]
    tm = min(tm, t)
    return pl.pallas_call(
        functools.partial(_proj_kernel, chunk=chunk),
        grid=(t // tm,),
        in_specs=[pl.BlockSpec((tm, k), lambda i: (i, 0)),
                  pl.BlockSpec((k, n), lambda i: (0, 0))],
        out_specs=pl.BlockSpec((tm, n), lambda i: (i, 0)),
        out_shape=jax.ShapeDtypeStruct((t, n), out_dtype),
        compiler_params=_cparams("parallel"),
        name="proj",
    )(x2d, w)


def _proj_ln_kernel(y_ref, w_ref, h_ref, o_ref, *, sub):
    for r0 in range(0, y_ref.shape[0], sub):
        rows = slice(r0, r0 + sub)
        y = jnp.dot(y_ref[rows, :].astype(BF16), w_ref[...], preferred_element_type=F32)
        o_ref[rows, :] = DEEPNORM_ALPHA * h_ref[rows, :] + y


def _proj_ln(y2d, w, h2d, tm=2048, sub=256):
    t, k = y2d.shape
    d = w.shape[1]
    tm = min(tm, t)
    return pl.pallas_call(
        functools.partial(_proj_ln_kernel, sub=min(sub, tm)),
        grid=(t // tm,),
        in_specs=[pl.BlockSpec((tm, k), lambda i: (i, 0)),
                  pl.BlockSpec((k, d), lambda i: (0, 0)),
                  pl.BlockSpec((tm, d), lambda i: (i, 0))],
        out_specs=pl.BlockSpec((tm, d), lambda i: (i, 0)),
        out_shape=jax.ShapeDtypeStruct((t, d), F32),
        compiler_params=_cparams("parallel"),
        name="proj_ln",
    )(y2d, w, h2d)


LRU_GATE_TILE = 256
LRU_GATE_WIN = 512


def _lru_gate_window_starts():
    starts = []
    for n0 in range(0, D_RNN_PAD, LRU_GATE_TILE):
        first_block = min(n0, D_RNN - 1) // LRU_BLOCK_SIZE
        last_block = min(n0 + LRU_GATE_TILE - 1, D_RNN - 1) // LRU_BLOCK_SIZE
        lo = (first_block * LRU_BLOCK_SIZE) // LANES * LANES
        lo = min(lo, D_RNN_PAD - LRU_GATE_WIN)
        assert (last_block + 1) * LRU_BLOCK_SIZE <= lo + LRU_GATE_WIN
        starts.append(lo)
    return starts


LRU_HIST = (CONV_WIDTH - 1) * SUBLANES


def _lru_kernel(h_ref, win_ref, cw_ref, cb_ref, gw_ref, gb_ref, lam_ref, wout_ref, o_ref, xbuf, hstate):
    @pl.when(pl.program_id(0) == 0)
    def _():
        xbuf[:, 0:LRU_HIST, :] = jnp.zeros((xbuf.shape[0], LRU_HIST, xbuf.shape[2]), F32)
        hstate[...] = jnp.zeros(hstate.shape, F32)

    ng = h_ref.shape[0] // SUBLANES
    fronts = [_lru_front(gi, h_ref, win_ref, cw_ref, cb_ref, gw_ref, xbuf) for gi in range(ng)]
    for gi in range(ng):
        _lru_back(gi, *fronts[gi], gb_ref, lam_ref, wout_ref, o_ref, hstate)


def _lru_front(gi, h_ref, win_ref, cw_ref, cb_ref, gw_ref, xbuf):
    tt, d = h_ref.shape[1], h_ref.shape[2]
    c = cw_ref.shape[1]
    ts = tt * SUBLANES
    hres = h_ref[gi * SUBLANES:(gi + 1) * SUBLANES]
    ht = pltpu.einshape("btd->tbd", hres).reshape(ts, d)
    u = jnp.dot(ht.astype(BF16), win_ref[...], preferred_element_type=F32)
    gate_branch = _gelu_tanh(u[:, :c])
    x = u[:, c:]
    xbuf[gi, LRU_HIST:LRU_HIST + ts, :] = x
    xr = cb_ref[...] + cw_ref[3:4, :] * x
    for k in range(CONV_WIDTH - 1):
        xr = xr + cw_ref[k:k + 1, :] * xbuf[gi, k * SUBLANES:k * SUBLANES + ts, :]
    xbuf[gi, 0:LRU_HIST, :] = xbuf[gi, ts:ts + LRU_HIST, :]

    xb = xr.astype(BF16)
    tiles = [jnp.dot(xb[:, lo:lo + LRU_GATE_WIN], gw_ref[n], preferred_element_type=F32)
             for n, lo in enumerate(_lru_gate_window_starts())]
    r_pre = jnp.concatenate([t[:, :LRU_GATE_TILE] for t in tiles], axis=1)[:, :c]
    i_pre = jnp.concatenate([t[:, LRU_GATE_TILE:] for t in tiles], axis=1)[:, :c]
    return hres, gate_branch, xr, r_pre, i_pre


def _lru_back(gi, hres, gate_branch, xr, r_pre, i_pre, gb_ref, lam_ref, wout_ref, o_ref, hstate):
    ts, c = xr.shape
    tt = ts // SUBLANES
    r_gate = _sigmoid(r_pre + gb_ref[0:1, :])
    i_gate = _sigmoid(i_pre + gb_ref[1:2, :])
    log_a = (-RG_LRU_C) * r_gate * _softplus(-lam_ref[...])
    a = jnp.exp(log_a)
    bt = jnp.sqrt(-jnp.tanh(log_a) * (a * a + 1.0)) * (i_gate * xr)

    h = hstate[gi]
    steps = []
    for t in range(tt):
        slab = slice(t * SUBLANES, (t + 1) * SUBLANES)
        h = a[slab, :] * h + bt[slab, :]
        steps.append(h)
    hstate[gi] = h
    hs = jnp.concatenate(steps, axis=0)

    y = jnp.dot((hs * gate_branch).astype(BF16), wout_ref[...], preferred_element_type=F32)
    yb = pltpu.einshape("tbd->btd", y.reshape(tt, SUBLANES, y.shape[1]))
    o_ref[gi * SUBLANES:(gi + 1) * SUBLANES] = DEEPNORM_ALPHA * hres + yb


def _lru(h, win, cw, cb, gw, gb, lam, wout, tt=32):
    bsz, seq, d = h.shape
    c = cw.shape[1]
    assert bsz % SUBLANES == 0
    tt = min(tt, seq)
    ng = bsz // SUBLANES
    resident = lambda w: pl.BlockSpec(w.shape, lambda j: (0,) * w.ndim, pipeline_mode=pl.Buffered(1))
    return pl.pallas_call(
        _lru_kernel,
        grid=(seq // tt,),
        in_specs=[pl.BlockSpec((bsz, tt, d), lambda j: (0, j, 0))]
                 + [resident(w) for w in (win, cw, cb, gw, gb, lam, wout)],
        out_specs=pl.BlockSpec((bsz, tt, d), lambda j: (0, j, 0)),
        out_shape=jax.ShapeDtypeStruct((bsz, seq, d), F32),
        scratch_shapes=[pltpu.VMEM((ng, LRU_HIST + tt * SUBLANES, c), F32), pltpu.VMEM((ng, SUBLANES, c), F32)],
        compiler_params=_cparams("arbitrary"),
        name="lru",
    )(h, win, cw, cb, gw, gb, lam, wout)


def _pad_last(x, n):
    return jnp.pad(x, [(0, 0)] * (x.ndim - 1) + [(0, n - x.shape[-1])])


def _rglru_sublayer(h, w_in, conv_w, conv_b, gate_w, gate_b, lam, w_out):
    bsz, seq, d = h.shape
    c = D_RNN_PAD
    w_cat = jnp.concatenate([_pad_last(w_in[:, :D_RNN], c), _pad_last(w_in[:, D_RNN:], c)], axis=1).astype(BF16)
    eye = jnp.eye(LRU_BLOCKS, dtype=F32)
    dense = jnp.einsum('gncd,nm->gncmd', gate_w, eye).reshape(2, D_RNN, D_RNN)
    starts = _lru_gate_window_starts()
    dense = jnp.pad(dense, ((0, 0), (0, c - D_RNN), (0, len(starts) * LRU_GATE_TILE - D_RNN)))
    gw = jnp.stack([jnp.concatenate([dense[gi, lo:lo + LRU_GATE_WIN, n * LRU_GATE_TILE:(n + 1) * LRU_GATE_TILE]
                                     for gi in range(2)], axis=1) for n, lo in enumerate(starts)]).astype(BF16)
    wout = jnp.pad(w_out, ((0, c - D_RNN), (0, 0))).astype(BF16)
    return _lru(h, w_cat, _pad_last(conv_w, c), _pad_last(conv_b[None, :], c), gw,
                _pad_last(gate_b, c), _pad_last(lam[None, :], c), wout)


def _rwkv_in_kernel(x_ref, xp_ref, mu_ref, wr_ref, wk_ref, wv_ref, w0_ref, w1_ref, w2_ref,
                    a0_ref, a1_ref, a2_ref, g1_ref, g2_ref,
                    r_ref, k_ref, v_ref, ld_ref, a_ref, g_ref, *, sub):
    i = pl.program_id(1)
    x_all = x_ref[0]
    row = lax.broadcasted_iota(jnp.int32, x_all.shape, 0)
    prev = jnp.where(i > 0, xp_ref[0, SUBLANES - 1:SUBLANES, :], 0.0)
    xx_all = jnp.where(row == 0, prev, pltpu.roll(x_all, 1, 0)) - x_all
    dot = lambda p, q: jnp.dot(p, q, preferred_element_type=F32)
    for r0 in range(0, x_all.shape[0], sub):
        rows = slice(r0, r0 + sub)
        x, xx = x_all[rows, :], xx_all[rows, :]
        mix = lambda c: (x + xx * mu_ref[c:c + 1, :]).astype(BF16)
        r_ref[0, rows, :] = dot(mix(0), wr_ref[...]).astype(r_ref.dtype)
        k_ref[0, rows, :] = dot(mix(2), wk_ref[...]).astype(k_ref.dtype)
        v_ref[0, rows, :] = dot(mix(3), wv_ref[...]).astype(v_ref.dtype)
        wl = w0_ref[...] + dot(jnp.tanh(dot(mix(1), w1_ref[...])).astype(BF16), w2_ref[...])
        w_log = -_softplus(-wl, wide=True) - 0.5
        ld_ref[0, rows, :] = -jnp.exp(w_log)
        a_ref[0, rows, :] = _sigmoid(a0_ref[...] + dot(dot(mix(4), a1_ref[...]).astype(BF16), a2_ref[...]))
        g_ref[0, rows, :] = dot(_sigmoid(dot(mix(5), g1_ref[...])).astype(BF16), g2_ref[...]).astype(g_ref.dtype)


def _rwkv_in(h, mu, wr, wk, wv, w0, w1, w2, a0, a1, a2, g1, g2, tm=512, sub=256):
    bsz, seq, d = h.shape
    tm = min(tm, seq)
    row = lambda b, i: (b, i, 0)
    fix = lambda b, i: (0, 0)
    prev = lambda b, i: (b, jnp.maximum(i * (tm // SUBLANES) - 1, 0), 0)
    ws = [mu, wr, wk, wv, w0, w1, w2, a0, a1, a2, g1, g2]
    return pl.pallas_call(
        functools.partial(_rwkv_in_kernel, sub=min(sub, tm)),
        grid=(bsz, seq // tm),
        in_specs=[pl.BlockSpec((1, tm, d), row), pl.BlockSpec((1, SUBLANES, d), prev)]
                 + [pl.BlockSpec(w.shape, fix) for w in ws],
        out_specs=[pl.BlockSpec((1, tm, d), row)] * 6,
        out_shape=[jax.ShapeDtypeStruct((bsz, seq, d), dt) for dt in (BF16, BF16, BF16, F32, F32, BF16)],
        compiler_params=_cparams("parallel", "parallel"),
        name="rwkv_in",
    )(h, h, *ws)


def _bdot(a, b, dims):
    return jnp.einsum(dims, a, b, preferred_element_type=F32)


def _rwkv_rec_kernel(r_ref, k_ref, v_ref, ld_ref, a_ref, g_ref, kk_ref, ka_ref, rk_ref, lg_ref, lb_ref,
                     o_ref, state):
    @pl.when(pl.program_id(1) == 0)
    def _():
        state[...] = jnp.zeros(state.shape, F32)

    L = RW_CHUNK
    P = RW_PAIRS
    N = RW_HEAD_SIZE
    nc = r_ref.shape[1] // L
    B = nc * P
    cp = [(c, p) for c in range(nc) for p in range(P)]
    pairs = lambda ref: jnp.stack([ref[0, c * L:(c + 1) * L, LANES * p:LANES * (p + 1)]
                                   for c, p in cp]).astype(F32)
    vec = lambda ref: jnp.stack([ref[:, LANES * p:LANES * (p + 1)] for _, p in cp])

    lane = lax.broadcasted_iota(jnp.int32, (1, 1, LANES), 2)
    head0 = lane < N
    ri = lax.broadcasted_iota(jnp.int32, (LANES, LANES), 0)
    ci = lax.broadcasted_iota(jnp.int32, (LANES, LANES), 1)
    seg = ((ri // N) == (ci // N)).astype(BF16)

    def segsum(x, terms=2):
        flat = x.reshape(B * L, LANES)
        hi = flat.astype(BF16)
        out = jnp.dot(hi, seg, preferred_element_type=F32)
        if terms == 2:
            lo = (flat - hi.astype(F32)).astype(BF16)
            out = out + jnp.dot(lo, seg, preferred_element_type=F32)
        return out.reshape(B, L, LANES)

    r = pairs(r_ref)
    k = pairs(k_ref)
    v = pairs(v_ref)
    a = pairs(a_ref)

    kk = k * vec(kk_ref)
    kk = kk * lax.rsqrt(jnp.maximum(segsum(kk * kk, terms=1), 1e-24))
    k = k * (1.0 + (a - 1.0) * vec(ka_ref))
    aa = -kk
    bb = kk * a

    ld_full = ld_ref[0]
    tr = lax.broadcasted_iota(jnp.int32, (nc * L, nc * L), 0)
    tc = lax.broadcasted_iota(jnp.int32, (nc * L, nc * L), 1)
    tri = ((tc <= tr) & (tc // L == tr // L)).astype(BF16)
    cum_full = sum(jnp.dot(tri, t, preferred_element_type=F32) for t in _split2(ld_full))
    cum = jnp.stack([cum_full[c * L:(c + 1) * L, LANES * p:LANES * (p + 1)] for c, p in cp])
    ld = pairs(ld_ref)
    gam = jnp.exp(cum)
    gam_prev = jnp.exp(cum - ld)
    cum_end = cum[:, L - 1:L, :]
    r_t = r * gam
    a_t = aa * gam_prev
    k_t = k * jnp.exp(-cum)
    b_t = bb * jnp.exp(-cum)
    k_e = k * jnp.exp(cum_end - cum)
    b_e = bb * jnp.exp(cum_end - cum)

    def stacked(x):
        return jnp.concatenate([jnp.where(head0, x, 0.0), jnp.where(head0, 0.0, x)], axis=1)

    ar = jnp.concatenate([stacked(a_t), stacked(r_t)], axis=1).astype(BF16)
    kb = jnp.concatenate([stacked(k_t), stacked(b_t)], axis=1).astype(BF16)
    v_s = stacked(v)

    s4 = _bdot(ar, kb, 'pik,pjk->pij')
    M = 2 * L
    si = lax.broadcasted_iota(jnp.int32, (1, M, M), 1)
    sj = lax.broadcasted_iota(jnp.int32, (1, M, M), 2)
    same = (si // L) == (sj // L)
    strict = same & ((sj % L) < (si % L))
    incl = same & ((sj % L) <= (si % L))
    a_k = jnp.where(strict, s4[:, :M, :M], 0.0)
    a_b = jnp.where(strict, s4[:, :M, M:], 0.0)
    r_k = jnp.where(incl, s4[:, M:, :M], 0.0)
    r_b = jnp.where(incl, s4[:, M:, M:], 0.0)

    eye = (si == sj).astype(F32)
    lower1 = (si // 2 == sj // 2) & (si % 2 == 1) & (sj % 2 == 0)
    T = eye + jnp.where(lower1, a_b, 0.0)
    ab16 = a_b.astype(BF16)
    s = 2
    hi_ = lax.broadcasted_iota(jnp.int32, (1, M // 2, M), 1)
    hj_ = lax.broadcasted_iota(jnp.int32, (1, M // 2, M), 2)
    while s < L:
        tb = T.astype(BF16)
        if s < SUBLANES:
            cm = (si // (2 * s) == sj // (2 * s)) & (si % (2 * s) >= s) & (sj % (2 * s) < s)
            tat = _bdot(_bdot(tb, ab16, 'pij,pjk->pik').astype(BF16), tb, 'pij,pjk->pik')
            T = T + jnp.where(cm, tat, 0.0)
        else:
            nblk = M // (2 * s)
            low = jnp.concatenate([T[:, (2 * k + 1) * s:(2 * k + 2) * s, :] for k in range(nblk)], axis=1)
            upd = _bdot(_bdot(low.astype(BF16), ab16, 'pij,pjk->pik').astype(BF16), tb, 'pij,pjk->pik')
            upd = jnp.where((hj_ // (2 * s) == hi_ // s) & (hj_ % (2 * s) < s), upd, 0.0)
            zero = jnp.zeros((B, s, M), F32)
            T = T + jnp.concatenate([blk for k in range(nblk) for blk in (zero, upd[:, k * s:(k + 1) * s, :])],
                                    axis=1)
        s *= 2

    v16 = v_s.astype(BF16)
    akv = _bdot(a_k.astype(BF16), v16, 'pij,pjl->pil')
    t16 = T.astype(BF16)
    rkb = jnp.concatenate([r_k, r_b], axis=2).astype(BF16)
    kb_e = jnp.concatenate([stacked(k_e), stacked(b_e)], axis=1).astype(BF16)
    gam_end = jnp.exp(cum_end)

    s_val = state[...]
    y_parts = []
    for c in range(nc):
        sl = slice(c * P, (c + 1) * P)
        ah = _bdot(ar[sl], s_val.astype(BF16), 'pik,plk->pil')
        x_s = ah[:, :M, :] + akv[sl]
        u_s = _bdot(t16[sl], x_s.astype(BF16), 'pij,pjl->pil')
        vu = jnp.concatenate([v16[sl], u_s.astype(BF16)], axis=1)
        y_parts.append(ah[:, M:, :] + _bdot(rkb[sl], vu, 'pij,pjl->pil'))
        s_val = s_val * gam_end[sl] + _bdot(vu, kb_e[sl], 'pti,ptj->pij')
    state[...] = s_val
    y_s = jnp.concatenate(y_parts, axis=0)
    y = y_s[:, :L, :] + y_s[:, L:, :]

    mean = segsum(y) * (1.0 / N)
    yc = y - mean
    var = segsum(yc * yc, terms=1) * (1.0 / N)
    yn = yc * lax.rsqrt(var + RW_GN_EPS) * vec(lg_ref) + vec(lb_ref)
    bonus = segsum(r * k * vec(rk_ref)) * v
    out = (yn + bonus) * pairs(g_ref)
    for i, (c, p) in enumerate(cp):
        o_ref[0, c * L:(c + 1) * L, LANES * p:LANES * (p + 1)] = out[i].astype(o_ref.dtype)


def _rwkv_rec(r, k, v, ld, a, g, k_k, k_a, r_k, lnx_g, lnx_b, chunks_per_step=4):
    bsz, seq, d = r.shape
    L = RW_CHUNK * min(chunks_per_step, seq // RW_CHUNK)
    row = lambda b, j: (b, j, 0)
    fix = lambda b, j: (0, 0)
    return pl.pallas_call(
        _rwkv_rec_kernel,
        grid=(bsz, seq // L),
        in_specs=[pl.BlockSpec((1, L, d), row)] * 6 + [pl.BlockSpec((1, d), fix)] * 5,
        out_specs=pl.BlockSpec((1, L, d), row),
        out_shape=jax.ShapeDtypeStruct((bsz, seq, d), BF16),
        scratch_shapes=[pltpu.VMEM((RW_PAIRS, LANES, LANES), F32)],
        compiler_params=_cparams("parallel", "arbitrary"),
        name="rwkv_rec",
    )(r, k, v, ld, a, g, k_k, k_a, r_k, lnx_g, lnx_b)


def _rwkv_sublayer(h, mu, w_r, w_k, w_v, w0, w1, w2, a0, a1, a2, g1, g2, k_k, k_a, r_k,
                   lnx_g, lnx_b, w_o):
    bsz, seq, d = h.shape
    bf = lambda w: w.astype(BF16)
    row = lambda w: w.reshape(1, d)
    mu8 = jnp.pad(mu, ((0, SUBLANES - mu.shape[0]), (0, 0)))
    r, k, v, ld, a, gg = _rwkv_in(h, mu8, bf(w_r), bf(w_k), bf(w_v), row(w0), bf(w1), bf(w2),
                                  row(a0), bf(a1), bf(a2), bf(g1), bf(g2))
    y = _rwkv_rec(r, k, v, ld, a, gg, row(k_k), row(k_a), row(r_k), row(lnx_g), row(lnx_b))
    return _proj_ln(y.reshape(bsz * seq, d), bf(w_o), h.reshape(bsz * seq, d)).reshape(bsz, seq, d)


def _chunk_attn_kernel(q_ref, k_ref, v_ref, tbl_ref, o_ref):
    seq = q_ref.shape[1]
    blocks_back = (ATT_WIN - ATT_TQ) // ATT_TQ
    lane = lax.broadcasted_iota(jnp.int32, (ATT_TQ, LANES), 1)
    nq = seq // ATT_TQ

    def window(qi):
        back = min(qi, blocks_back)
        ws = (qi - back) * ATT_TQ
        width = (back + 1) * ATT_TQ
        off = (blocks_back - back) * ATT_TQ
        return ws, width, off

    def scores(qi):
        ws, width, _ = window(qi)
        kw = k_ref[0, ws:ws + width, :]
        q = q_ref[0, qi * ATT_TQ:(qi + 1) * ATT_TQ, :]
        out = []
        for hh in range(2):
            qh = jnp.where((lane // ATT_HEAD_DIM) == hh, q, jnp.zeros_like(q))
            out.append(lax.dot_general(qh, kw, (((1,), (1,)), ((), ())), preferred_element_type=F32))
        return out

    s_next = scores(0)
    for qi in range(nq):
        s_cur = s_next
        if qi + 1 < nq:
            s_next = scores(qi + 1)
        ws, width, off = window(qi)
        vw = v_ref[0, ws:ws + width, :]
        vlane = lax.broadcasted_iota(jnp.int32, vw.shape, 1)
        pv = []
        for hh in range(2):
            s = s_cur[hh] + tbl_ref[0, hh, :, off:off + width]
            m = jnp.max(s, axis=-1, keepdims=True)
            e = jnp.exp((s - m).astype(BF16))
            vh = jnp.where((vlane // ATT_HEAD_DIM) == hh, vw, jnp.ones_like(vw))
            pv.append(jnp.dot(e, vh, preferred_element_type=F32))
        num = jnp.where(lane < ATT_HEAD_DIM, pv[0], pv[1])
        den = jnp.where(lane < ATT_HEAD_DIM, pltpu.roll(pv[0], ATT_HEAD_DIM, 1), pltpu.roll(pv[1], ATT_HEAD_DIM, 1))
        o_ref[0, qi * ATT_TQ:(qi + 1) * ATT_TQ, :] = (num / den).astype(o_ref.dtype)


def _chunk_attn(qkv, tbl):
    bsz, seq, _ = qkv.shape
    return pl.pallas_call(
        _chunk_attn_kernel,
        grid=(bsz, ATT_PAIRS),
        in_specs=[pl.BlockSpec((1, seq, LANES), lambda b, p: (b, 0, p)),
                  pl.BlockSpec((1, seq, LANES), lambda b, p: (b, 0, ATT_PAIRS + p)),
                  pl.BlockSpec((1, seq, LANES), lambda b, p: (b, 0, 2 * ATT_PAIRS + p)),
                  pl.BlockSpec((1, 2, ATT_TQ, ATT_TBL), lambda b, p: (p, 0, 0, 0))],
        out_specs=pl.BlockSpec((1, seq, LANES), lambda b, p: (b, 0, p)),
        out_shape=jax.ShapeDtypeStruct((bsz, seq, D_MODEL), BF16),
        compiler_params=_cparams("parallel", "parallel"),
        name="chunk_attn",
    )(qkv, qkv, qkv, tbl)


ATT_PERIOD = ATT_TBL + ATT_TQ


def _attn_table_kernel(vec_ref, o_ref):
    p = pl.program_id(0)
    ic = lax.broadcasted_iota(jnp.int32, (ATT_TQ, ATT_TBL), 0) // CHUNK
    mc = lax.broadcasted_iota(jnp.int32, (ATT_TQ, ATT_TBL), 1) // CHUNK
    valid = (mc >= ic) & (mc <= ic + BAND_CHUNKS - 1)
    for hh in range(2):
        row = vec_ref[pl.ds(2 * p + hh, 1), :]
        skew = pltpu.roll(jnp.broadcast_to(row, (ATT_TQ, ATT_PERIOD)), 0, 1, stride=1, stride_axis=0)
        o_ref[0, hh] = jnp.where(valid, skew[:, :ATT_TBL], NEG_INF)


def _attn_bias_table(rel_bias):
    left = ATT_WIN - ATT_TQ
    j = jnp.arange(ATT_PERIOD)
    diff = jnp.where(j < ATT_TBL, j, j - ATT_PERIOD)
    vec = rel_bias[:, jnp.clip(left - diff, -MAX_REL, MAX_REL) + MAX_REL].astype(F32)
    return pl.pallas_call(
        _attn_table_kernel,
        grid=(ATT_PAIRS,),
        in_specs=[pl.BlockSpec(vec.shape, lambda p: (0, 0))],
        out_specs=pl.BlockSpec((1, 2, ATT_TQ, ATT_TBL), lambda p: (p, 0, 0, 0)),
        out_shape=jax.ShapeDtypeStruct((ATT_PAIRS, 2, ATT_TQ, ATT_TBL), F32),
        compiler_params=_cparams("parallel"),
        name="attn_table",
    )(vec)


def _attn_sublayer(h, w_qkv, rel_bias, w_o):
    bsz, seq, d = h.shape
    assert seq % ATT_TQ == 0 and seq >= ATT_WIN
    scale = jnp.concatenate([jnp.full((d,), ATT_HEAD_DIM ** -0.5, F32), jnp.ones((2 * d,), F32)])
    qkv = _proj(h.reshape(bsz * seq, d), (w_qkv * scale).astype(BF16), BF16)
    o = _chunk_attn(qkv.reshape(bsz, seq, 3 * d), _attn_bias_table(rel_bias))
    return _proj_ln(o.reshape(bsz * seq, d), w_o.astype(BF16), h.reshape(bsz * seq, d)).reshape(bsz, seq, d)


def _mem_attn_kernel(z_ref, gin_ref, bin_ref, kv_ref, wq_ref, wo_ref, o_ref, *, sub):
    tiles = [slice(r0, r0 + sub) for r0 in range(0, z_ref.shape[1], sub)]

    def front(rows):
        h = _ln(z_ref[0, rows, :], gin_ref[...], bin_ref[...])
        q = jnp.dot(h.astype(BF16), wq_ref[...], preferred_element_type=F32).astype(BF16)
        return h, [lax.dot_general(q[:, hh * MEM_HEAD_DIM:(hh + 1) * MEM_HEAD_DIM],
                                   kv_ref[0, :, hh * MEM_HEAD_DIM:(hh + 1) * MEM_HEAD_DIM],
                                   (((1,), (1,)), ((), ())), preferred_element_type=F32) for hh in range(MEM_HEADS)]

    def back(rows, h, scores):
        outs = []
        for hh, s in enumerate(scores):
            vh = kv_ref[0, :, D_MODEL + hh * MEM_HEAD_DIM:D_MODEL + (hh + 1) * MEM_HEAD_DIM]
            m = jnp.max(s, axis=-1, keepdims=True)
            e = jnp.exp(s - m)
            l = jnp.sum(e, axis=-1, keepdims=True)
            outs.append((jnp.dot(e.astype(BF16), vh, preferred_element_type=F32) / l).astype(BF16))
        y = jnp.dot(jnp.concatenate(outs, axis=1), wo_ref[...], preferred_element_type=F32)
        o_ref[0, rows, :] = DEEPNORM_ALPHA * h + y

    nxt = front(tiles[0])
    for i, rows in enumerate(tiles):
        cur = nxt
        if i + 1 < len(tiles):
            nxt = front(tiles[i + 1])
        back(rows, *cur)


def _mem_sublayer(z, g_in, b_in, mem, w_q, w_kv, w_o, tm=1024, sub=256):
    bsz, seq, d = z.shape
    nm = mem.shape[1]
    tm = min(tm, seq)
    kv = _proj(mem.reshape(bsz * nm, d), w_kv.astype(BF16), BF16).reshape(bsz, nm, 2 * d)
    wq = (w_q * (MEM_HEAD_DIM ** -0.5)).astype(BF16)
    row = lambda b_, i: (b_, i, 0)
    fix = lambda b_, i: (0, 0)
    return pl.pallas_call(
        functools.partial(_mem_attn_kernel, sub=min(sub, tm)),
        grid=(bsz, seq // tm),
        in_specs=[pl.BlockSpec((1, tm, d), row),
                  pl.BlockSpec((1, d), fix), pl.BlockSpec((1, d), fix),
                  pl.BlockSpec((1, nm, 2 * d), lambda b_, i: (b_, 0, 0)),
                  pl.BlockSpec((d, d), fix, pipeline_mode=pl.Buffered(1)),
                  pl.BlockSpec((d, d), fix, pipeline_mode=pl.Buffered(1))],
        out_specs=pl.BlockSpec((1, tm, d), row),
        out_shape=jax.ShapeDtypeStruct((bsz, seq, d), F32),
        compiler_params=_cparams("parallel", "parallel"),
        name="mem_attn",
    )(z, g_in, b_in, kv, wq, w_o.astype(BF16))


def _mlp_kernel(z_ref, gin_ref, bin_ref, w1_ref, w2_ref, g_ref, b_ref, o_ref, *, sub):
    for r0 in range(0, z_ref.shape[0], sub):
        rows = slice(r0, r0 + sub)
        h = _ln(z_ref[rows, :], gin_ref[...], bin_ref[...])
        u = jnp.dot(h.astype(BF16), w1_ref[...], preferred_element_type=F32)
        u = jnp.maximum(u, 0.0)
        y = jnp.dot((u * u).astype(BF16), w2_ref[...], preferred_element_type=F32)
        o_ref[rows, :] = _residual_ln(h, y, g_ref[...], b_ref[...])


def _mlp_sublayer(z, g_in, b_in, w1, w2, g, b, tm=1024, sub=256):
    bsz, seq, d = z.shape
    t = bsz * seq
    tm = min(tm, t)
    ff = w1.shape[1]
    resident = pl.Buffered(1)
    out = pl.pallas_call(
        functools.partial(_mlp_kernel, sub=min(sub, tm)),
        grid=(t // tm,),
        in_specs=[pl.BlockSpec((tm, d), lambda i: (i, 0)),
                  pl.BlockSpec((1, d), lambda i: (0, 0)),
                  pl.BlockSpec((1, d), lambda i: (0, 0)),
                  pl.BlockSpec((d, ff), lambda i: (0, 0), pipeline_mode=resident),
                  pl.BlockSpec((ff, d), lambda i: (0, 0), pipeline_mode=resident),
                  pl.BlockSpec((1, d), lambda i: (0, 0)),
                  pl.BlockSpec((1, d), lambda i: (0, 0))],
        out_specs=pl.BlockSpec((tm, d), lambda i: (i, 0)),
        out_shape=jax.ShapeDtypeStruct((t, d), F32),
        compiler_params=_cparams("parallel"),
        name="mlp",
    )(z.reshape(t, d), g_in, b_in, w1.astype(BF16), w2.astype(BF16), g, b)
    return out.reshape(bsz, seq, d)


def kernel(x, mem, ln_g, ln_b, lru_w_in, lru_conv_w, lru_conv_b, lru_gate_w, lru_gate_b, lru_lambda, lru_w_out, rw_mu, rw_w_r, rw_w_k, rw_w_v, rw_w0, rw_w1, rw_w2, rw_a0, rw_a1, rw_a2, rw_g1, rw_g2, rw_k_k, rw_k_a, rw_r_k, rw_lnx_g, rw_lnx_b, rw_w_o, ca_w_qkv, ca_rel_bias, ca_w_o, mx_w_q, mx_w_kv, mx_w_o, mlp_w1, mlp_w2):
    h = x
    for i in range(DEPTH):
        kind, j = i % N_MIXERS, i // N_MIXERS
        g = lambda s: ln_g[i, s][None, :]
        b = lambda s: ln_b[i, s][None, :]
        if kind == 0:
            z = _rglru_sublayer(h, lru_w_in[j], lru_conv_w[j], lru_conv_b[j], lru_gate_w[j], lru_gate_b[j],
                                lru_lambda[j], lru_w_out[j])
        elif kind == 1:
            z = _rwkv_sublayer(h, rw_mu[j], rw_w_r[j], rw_w_k[j], rw_w_v[j], rw_w0[j], rw_w1[j], rw_w2[j],
                               rw_a0[j], rw_a1[j], rw_a2[j], rw_g1[j], rw_g2[j], rw_k_k[j], rw_k_a[j],
                               rw_r_k[j], rw_lnx_g[j], rw_lnx_b[j], rw_w_o[j])
        else:
            z = _attn_sublayer(h, ca_w_qkv[j], ca_rel_bias[j], ca_w_o[j])
        z = _mem_sublayer(z, g(0), b(0), mem, mx_w_q[i], mx_w_kv[i], mx_w_o[i])
        h = _mlp_sublayer(z, g(1), b(1), mlp_w1[i], mlp_w2[i], g(2), b(2))
    return h
```

```python
import functools

import jax
import jax.numpy as jnp
from jax import lax
from jax.experimental import pallas as pl
from jax.experimental.pallas import tpu as pltpu

F32 = jnp.float32
BF16 = jnp.bfloat16

D_MODEL = 1024
DEPTH = 4
CHUNK = 64
N_MIXERS = 3
DEEPNORM_ALPHA = (2 * DEPTH) ** 0.25
LN_EPS = 1e-5

D_RNN = 1344
LRU_BLOCKS = 16
LRU_BLOCK_SIZE = D_RNN // LRU_BLOCKS
CONV_WIDTH = 4
RG_LRU_C = 8.0
LANES = 128
SUBLANES = 8
D_RNN_PAD = -(-D_RNN // LANES) * LANES

RW_HEAD_SIZE = 64
RW_PAIRS = D_MODEL // LANES
RW_GN_EPS = 64e-5
RW_CHUNK = 64

ATT_HEADS = 16
ATT_HEAD_DIM = D_MODEL // ATT_HEADS
ATT_PAIRS = D_MODEL // LANES
BAND_CHUNKS = 9
MAX_REL = 2 * CHUNK
NEG_INF = -1e30
ATT_TQ = 512
ATT_WIN = ATT_TQ + (BAND_CHUNKS - 1) * CHUNK
ATT_TBL = ATT_WIN + (BAND_CHUNKS - 1) * CHUNK

MEM_HEADS = 4
MEM_HEAD_DIM = D_MODEL // MEM_HEADS

VMEM_LIMIT = 56 * 1024 * 1024


def _cparams(*sem):
    return pltpu.CompilerParams(dimension_semantics=sem, vmem_limit_bytes=VMEM_LIMIT)


def _residual_ln(hres, y, g, b):
    return _ln(DEEPNORM_ALPHA * hres + y, g, b)


def _ln(z, g, b):
    mu = jnp.mean(z, axis=-1, keepdims=True)
    zc = z - mu
    var = jnp.mean(zc * zc, axis=-1, keepdims=True)
    return zc * lax.rsqrt(var + LN_EPS) * g + b


def _gelu_tanh(x):
    return 0.5 * x * (1.0 + jnp.tanh(0.7978845608028654 * (x + 0.044715 * (x * x * x))))


def _softplus(x, wide=False):
    t = jnp.exp(-jnp.abs(x))
    return jnp.maximum(x, 0.0) + (jnp.log(1.0 + t) if wide else jnp.log1p(t))


def _sigmoid(x):
    return 0.5 * jnp.tanh(0.5 * x) + 0.5


def _split2(x):
    hi = x.astype(BF16)
    lo = (x - hi.astype(F32)).astype(BF16)
    return hi, lo


def _proj_kernel(x_ref, w_ref, o_ref, *, chunk):
    x = x_ref[...].astype(BF16)
    n = w_ref.shape[1]
    for n0 in range(0, n, chunk):
        y = jnp.dot(x, w_ref[:, n0:n0 + chunk], preferred_element_type=F32)
        o_ref[:, n0:n0 + chunk] = y.astype(o_ref.dtype)


def _proj(x2d, w, out_dtype, tm=512, chunk=512):
    t, k = x2d.shape
    n = w.shape[1]
    tm = min(tm, t)
    return pl.pallas_call(
        functools.partial(_proj_kernel, chunk=chunk),
        grid=(t // tm,),
        in_specs=[pl.BlockSpec((tm, k), lambda i: (i, 0)),
                  pl.BlockSpec((k, n), lambda i: (0, 0))],
        out_specs=pl.BlockSpec((tm, n), lambda i: (i, 0)),
        out_shape=jax.ShapeDtypeStruct((t, n), out_dtype),
        compiler_params=_cparams("parallel"),
        name="proj",
    )(x2d, w)


def _proj_residual_kernel(y_ref, w_ref, h_ref, o_ref, *, sub):
    for r0 in range(0, y_ref.shape[0], sub):
        rows = slice(r0, r0 + sub)
        y = jnp.dot(y_ref[rows, :].astype(BF16), w_ref[...], preferred_element_type=F32)
        o_ref[rows, :] = DEEPNORM_ALPHA * h_ref[rows, :] + y


def _proj_residual(y2d, w, h2d, tm=2048, sub=256):
    t, k = y2d.shape
    d = w.shape[1]
    tm = min(tm, t)
    return pl.pallas_call(
        functools.partial(_proj_residual_kernel, sub=min(sub, tm)),
        grid=(t // tm,),
        in_specs=[pl.BlockSpec((tm, k), lambda i: (i, 0)),
                  pl.BlockSpec((k, d), lambda i: (0, 0)),
                  pl.BlockSpec((tm, d), lambda i: (i, 0))],
        out_specs=pl.BlockSpec((tm, d), lambda i: (i, 0)),
        out_shape=jax.ShapeDtypeStruct((t, d), F32),
        compiler_params=_cparams("parallel"),
        name="proj_residual",
    )(y2d, w, h2d)


LRU_GATE_TILE = 256
LRU_GATE_WIN = 512


def _lru_gate_window_starts():
    starts = []
    for n0 in range(0, D_RNN_PAD, LRU_GATE_TILE):
        first_block = min(n0, D_RNN - 1) // LRU_BLOCK_SIZE
        last_block = min(n0 + LRU_GATE_TILE - 1, D_RNN - 1) // LRU_BLOCK_SIZE
        lo = (first_block * LRU_BLOCK_SIZE) // LANES * LANES
        lo = min(lo, D_RNN_PAD - LRU_GATE_WIN)
        assert (last_block + 1) * LRU_BLOCK_SIZE <= lo + LRU_GATE_WIN
        starts.append(lo)
    return starts


LRU_HIST = (CONV_WIDTH - 1) * SUBLANES


def _lru_kernel(h_ref, win_ref, cw_ref, cb_ref, gw_ref, gb_ref, lam_ref, wout_ref, o_ref, xbuf, hstate):
    @pl.when(pl.program_id(0) == 0)
    def _():
        xbuf[:, 0:LRU_HIST, :] = jnp.zeros((xbuf.shape[0], LRU_HIST, xbuf.shape[2]), F32)
        hstate[...] = jnp.zeros(hstate.shape, F32)

    ng = h_ref.shape[0] // SUBLANES
    fronts = [_lru_front(gi, h_ref, win_ref, cw_ref, cb_ref, gw_ref, xbuf) for gi in range(ng)]
    for gi in range(ng):
        _lru_back(gi, *fronts[gi], gb_ref, lam_ref, wout_ref, o_ref, hstate)


def _lru_front(gi, h_ref, win_ref, cw_ref, cb_ref, gw_ref, xbuf):
    tt, d = h_ref.shape[1], h_ref.shape[2]
    c = cw_ref.shape[1]
    ts = tt * SUBLANES
    hres = h_ref[gi * SUBLANES:(gi + 1) * SUBLANES]
    ht = pltpu.einshape("btd->tbd", hres).reshape(ts, d)
    u = jnp.dot(ht.astype(BF16), win_ref[...], preferred_element_type=F32)
    gate_branch = _gelu_tanh(u[:, :c])
    x = u[:, c:]
    xbuf[gi, LRU_HIST:LRU_HIST + ts, :] = x
    xr = cb_ref[...] + cw_ref[3:4, :] * x
    for k in range(CONV_WIDTH - 1):
        xr = xr + cw_ref[k:k + 1, :] * xbuf[gi, k * SUBLANES:k * SUBLANES + ts, :]
    xbuf[gi, 0:LRU_HIST, :] = xbuf[gi, ts:ts + LRU_HIST, :]

    xb = xr.astype(BF16)
    tiles = [jnp.dot(xb[:, lo:lo + LRU_GATE_WIN], gw_ref[n], preferred_element_type=F32)
             for n, lo in enumerate(_lru_gate_window_starts())]
    r_pre = jnp.concatenate([t[:, :LRU_GATE_TILE] for t in tiles], axis=1)[:, :c]
    i_pre = jnp.concatenate([t[:, LRU_GATE_TILE:] for t in tiles], axis=1)[:, :c]
    return hres, gate_branch, xr, r_pre, i_pre


def _lru_back(gi, hres, gate_branch, xr, r_pre, i_pre, gb_ref, lam_ref, wout_ref, o_ref, hstate):
    ts, c = xr.shape
    tt = ts // SUBLANES
    r_gate = _sigmoid(r_pre + gb_ref[0:1, :])
    i_gate = _sigmoid(i_pre + gb_ref[1:2, :])
    log_a = (-RG_LRU_C) * r_gate * _softplus(-lam_ref[...])
    a = jnp.exp(log_a)
    bt = jnp.sqrt(-jnp.tanh(log_a) * (a * a + 1.0)) * (i_gate * xr)

    h = hstate[gi]
    steps = []
    for t in range(tt):
        slab = slice(t * SUBLANES, (t + 1) * SUBLANES)
        h = a[slab, :] * h + bt[slab, :]
        steps.append(h)
    hstate[gi] = h
    hs = jnp.concatenate(steps, axis=0)

    y = jnp.dot((hs * gate_branch).astype(BF16), wout_ref[...], preferred_element_type=F32)
    yb = pltpu.einshape("tbd->btd", y.reshape(tt, SUBLANES, y.shape[1]))
    o_ref[gi * SUBLANES:(gi + 1) * SUBLANES] = DEEPNORM_ALPHA * hres + yb


def _lru(h, win, cw, cb, gw, gb, lam, wout, tt=32):
    bsz, seq, d = h.shape
    c = cw.shape[1]
    assert bsz % SUBLANES == 0
    tt = min(tt, seq)
    ng = bsz // SUBLANES
    resident = lambda w: pl.BlockSpec(w.shape, lambda j: (0,) * w.ndim, pipeline_mode=pl.Buffered(1))
    return pl.pallas_call(
        _lru_kernel,
        grid=(seq // tt,),
        in_specs=[pl.BlockSpec((bsz, tt, d), lambda j: (0, j, 0))]
                 + [resident(w) for w in (win, cw, cb, gw, gb, lam, wout)],
        out_specs=pl.BlockSpec((bsz, tt, d), lambda j: (0, j, 0)),
        out_shape=jax.ShapeDtypeStruct((bsz, seq, d), F32),
        scratch_shapes=[pltpu.VMEM((ng, LRU_HIST + tt * SUBLANES, c), F32), pltpu.VMEM((ng, SUBLANES, c), F32)],
        compiler_params=_cparams("arbitrary"),
        name="lru",
    )(h, win, cw, cb, gw, gb, lam, wout)


def _pad_last(x, n):
    return jnp.pad(x, [(0, 0)] * (x.ndim - 1) + [(0, n - x.shape[-1])])


def _rglru_sublayer(h, w_in, conv_w, conv_b, gate_w, gate_b, lam, w_out):
    bsz, seq, d = h.shape
    c = D_RNN_PAD
    w_cat = jnp.concatenate([_pad_last(w_in[:, :D_RNN], c), _pad_last(w_in[:, D_RNN:], c)], axis=1).astype(BF16)
    eye = jnp.eye(LRU_BLOCKS, dtype=F32)
    dense = jnp.einsum('gncd,nm->gncmd', gate_w, eye).reshape(2, D_RNN, D_RNN)
    starts = _lru_gate_window_starts()
    dense = jnp.pad(dense, ((0, 0), (0, c - D_RNN), (0, len(starts) * LRU_GATE_TILE - D_RNN)))
    gw = jnp.stack([jnp.concatenate([dense[gi, lo:lo + LRU_GATE_WIN, n * LRU_GATE_TILE:(n + 1) * LRU_GATE_TILE]
                                     for gi in range(2)], axis=1) for n, lo in enumerate(starts)]).astype(BF16)
    wout = jnp.pad(w_out, ((0, c - D_RNN), (0, 0))).astype(BF16)
    return _lru(h, w_cat, _pad_last(conv_w, c), _pad_last(conv_b[None, :], c), gw,
                _pad_last(gate_b, c), _pad_last(lam[None, :], c), wout)


def _rwkv_in_kernel(x_ref, xp_ref, mu_ref, wr_ref, wk_ref, wv_ref, w0_ref, w1_ref, w2_ref,
                    a0_ref, a1_ref, a2_ref, g1_ref, g2_ref,
                    r_ref, k_ref, v_ref, ld_ref, a_ref, g_ref, *, sub):
    i = pl.program_id(1)
    x_all = x_ref[0]
    row = lax.broadcasted_iota(jnp.int32, x_all.shape, 0)
    prev = jnp.where(i > 0, xp_ref[0, SUBLANES - 1:SUBLANES, :], 0.0)
    xx_all = jnp.where(row == 0, prev, pltpu.roll(x_all, 1, 0)) - x_all
    dot = lambda p, q: jnp.dot(p, q, preferred_element_type=F32)
    for r0 in range(0, x_all.shape[0], sub):
        rows = slice(r0, r0 + sub)
        x, xx = x_all[rows, :], xx_all[rows, :]
        mix = lambda c: (x + xx * mu_ref[c:c + 1, :]).astype(BF16)
        r_ref[0, rows, :] = dot(mix(0), wr_ref[...]).astype(r_ref.dtype)
        k_ref[0, rows, :] = dot(mix(2), wk_ref[...]).astype(k_ref.dtype)
        v_ref[0, rows, :] = dot(mix(3), wv_ref[...]).astype(v_ref.dtype)
        wl = w0_ref[...] + dot(jnp.tanh(dot(mix(1), w1_ref[...])).astype(BF16), w2_ref[...])
        w_log = -_softplus(-wl, wide=True) - 0.5
        ld_ref[0, rows, :] = -jnp.exp(w_log)
        a_ref[0, rows, :] = _sigmoid(a0_ref[...] + dot(dot(mix(4), a1_ref[...]).astype(BF16), a2_ref[...]))
        g_ref[0, rows, :] = dot(_sigmoid(dot(mix(5), g1_ref[...])).astype(BF16), g2_ref[...]).astype(g_ref.dtype)


def _rwkv_in(h, mu, wr, wk, wv, w0, w1, w2, a0, a1, a2, g1, g2, tm=512, sub=256):
    bsz, seq, d = h.shape
    tm = min(tm, seq)
    row = lambda b, i: (b, i, 0)
    fix = lambda b, i: (0, 0)
    prev = lambda b, i: (b, jnp.maximum(i * (tm // SUBLANES) - 1, 0), 0)
    ws = [mu, wr, wk, wv, w0, w1, w2, a0, a1, a2, g1, g2]
    return pl.pallas_call(
        functools.partial(_rwkv_in_kernel, sub=min(sub, tm)),
        grid=(bsz, seq // tm),
        in_specs=[pl.BlockSpec((1, tm, d), row), pl.BlockSpec((1, SUBLANES, d), prev)]
                 + [pl.BlockSpec(w.shape, fix) for w in ws],
        out_specs=[pl.BlockSpec((1, tm, d), row)] * 6,
        out_shape=[jax.ShapeDtypeStruct((bsz, seq, d), dt) for dt in (BF16, BF16, BF16, F32, F32, BF16)],
        compiler_params=_cparams("parallel", "parallel"),
        name="rwkv_in",
    )(h, h, *ws)


def _bdot(a, b, dims):
    return jnp.einsum(dims, a, b, preferred_element_type=F32)


def _rwkv_rec_kernel(r_ref, k_ref, v_ref, ld_ref, a_ref, g_ref, kk_ref, ka_ref, rk_ref, lg_ref, lb_ref,
                     o_ref, state):
    @pl.when(pl.program_id(1) == 0)
    def _():
        state[...] = jnp.zeros(state.shape, F32)

    L = RW_CHUNK
    P = RW_PAIRS
    N = RW_HEAD_SIZE
    nc = r_ref.shape[1] // L
    B = nc * P
    cp = [(c, p) for c in range(nc) for p in range(P)]
    pairs = lambda ref: jnp.stack([ref[0, c * L:(c + 1) * L, LANES * p:LANES * (p + 1)]
                                   for c, p in cp]).astype(F32)
    vec = lambda ref: jnp.stack([ref[:, LANES * p:LANES * (p + 1)] for _, p in cp])

    lane = lax.broadcasted_iota(jnp.int32, (1, 1, LANES), 2)
    head0 = lane < N
    ri = lax.broadcasted_iota(jnp.int32, (LANES, LANES), 0)
    ci = lax.broadcasted_iota(jnp.int32, (LANES, LANES), 1)
    seg = ((ri // N) == (ci // N)).astype(BF16)

    def segsum(x, terms=2):
        flat = x.reshape(B * L, LANES)
        hi = flat.astype(BF16)
        out = jnp.dot(hi, seg, preferred_element_type=F32)
        if terms == 2:
            lo = (flat - hi.astype(F32)).astype(BF16)
            out = out + jnp.dot(lo, seg, preferred_element_type=F32)
        return out.reshape(B, L, LANES)

    r = pairs(r_ref)
    k = pairs(k_ref)
    v = pairs(v_ref)
    a = pairs(a_ref)

    kk = k * vec(kk_ref)
    kk = kk * lax.rsqrt(jnp.maximum(segsum(kk * kk, terms=1), 1e-24))
    k = k * (1.0 + (a - 1.0) * vec(ka_ref))
    aa = -kk
    bb = kk * a

    ld_full = ld_ref[0]
    tr = lax.broadcasted_iota(jnp.int32, (nc * L, nc * L), 0)
    tc = lax.broadcasted_iota(jnp.int32, (nc * L, nc * L), 1)
    tri = ((tc <= tr) & (tc // L == tr // L)).astype(BF16)
    cum_full = sum(jnp.dot(tri, t, preferred_element_type=F32) for t in _split2(ld_full))
    cum = jnp.stack([cum_full[c * L:(c + 1) * L, LANES * p:LANES * (p + 1)] for c, p in cp])
    ld = pairs(ld_ref)
    gam = jnp.exp(cum)
    gam_prev = jnp.exp(cum - ld)
    cum_end = cum[:, L - 1:L, :]
    r_t = r * gam
    a_t = aa * gam_prev
    k_t = k * jnp.exp(-cum)
    b_t = bb * jnp.exp(-cum)
    k_e = k * jnp.exp(cum_end - cum)
    b_e = bb * jnp.exp(cum_end - cum)

    def stacked(x):
        return jnp.concatenate([jnp.where(head0, x, 0.0), jnp.where(head0, 0.0, x)], axis=1)

    ar = jnp.concatenate([stacked(a_t), stacked(r_t)], axis=1).astype(BF16)
    kb = jnp.concatenate([stacked(k_t), stacked(b_t)], axis=1).astype(BF16)
    v_s = stacked(v)

    s4 = _bdot(ar, kb, 'pik,pjk->pij')
    M = 2 * L
    si = lax.broadcasted_iota(jnp.int32, (1, M, M), 1)
    sj = lax.broadcasted_iota(jnp.int32, (1, M, M), 2)
    same = (si // L) == (sj // L)
    strict = same & ((sj % L) < (si % L))
    incl = same & ((sj % L) <= (si % L))
    a_k = jnp.where(strict, s4[:, :M, :M], 0.0)
    a_b = jnp.where(strict, s4[:, :M, M:], 0.0)
    r_k = jnp.where(incl, s4[:, M:, :M], 0.0)
    r_b = jnp.where(incl, s4[:, M:, M:], 0.0)

    eye = (si == sj).astype(F32)
    lower1 = (si // 2 == sj // 2) & (si % 2 == 1) & (sj % 2 == 0)
    T = eye + jnp.where(lower1, a_b, 0.0)
    ab16 = a_b.astype(BF16)
    s = 2
    hi_ = lax.broadcasted_iota(jnp.int32, (1, M // 2, M), 1)
    hj_ = lax.broadcasted_iota(jnp.int32, (1, M // 2, M), 2)
    while s < L:
        tb = T.astype(BF16)
        if s < SUBLANES:
            cm = (si // (2 * s) == sj // (2 * s)) & (si % (2 * s) >= s) & (sj % (2 * s) < s)
            tat = _bdot(_bdot(tb, ab16, 'pij,pjk->pik').astype(BF16), tb, 'pij,pjk->pik')
            T = T + jnp.where(cm, tat, 0.0)
        else:
            nblk = M // (2 * s)
            low = jnp.concatenate([T[:, (2 * k + 1) * s:(2 * k + 2) * s, :] for k in range(nblk)], axis=1)
            upd = _bdot(_bdot(low.astype(BF16), ab16, 'pij,pjk->pik').astype(BF16), tb, 'pij,pjk->pik')
            upd = jnp.where((hj_ // (2 * s) == hi_ // s) & (hj_ % (2 * s) < s), upd, 0.0)
            zero = jnp.zeros((B, s, M), F32)
            T = T + jnp.concatenate([blk for k in range(nblk) for blk in (zero, upd[:, k * s:(k + 1) * s, :])],
                                    axis=1)
        s *= 2

    v16 = v_s.astype(BF16)
    akv = _bdot(a_k.astype(BF16), v16, 'pij,pjl->pil')
    t16 = T.astype(BF16)
    rkb = jnp.concatenate([r_k, r_b], axis=2).astype(BF16)
    kb_e = jnp.concatenate([stacked(k_e), stacked(b_e)], axis=1).astype(BF16)
    gam_end = jnp.exp(cum_end)

    s_val = state[...]
    y_parts = []
    for c in range(nc):
        sl = slice(c * P, (c + 1) * P)
        ah = _bdot(ar[sl], s_val.astype(BF16), 'pik,plk->pil')
        x_s = ah[:, :M, :] + akv[sl]
        u_s = _bdot(t16[sl], x_s.astype(BF16), 'pij,pjl->pil')
        vu = jnp.concatenate([v16[sl], u_s.astype(BF16)], axis=1)
        y_parts.append(ah[:, M:, :] + _bdot(rkb[sl], vu, 'pij,pjl->pil'))
        s_val = s_val * gam_end[sl] + _bdot(vu, kb_e[sl], 'pti,ptj->pij')
    state[...] = s_val
    y_s = jnp.concatenate(y_parts, axis=0)
    y = y_s[:, :L, :] + y_s[:, L:, :]

    mean = segsum(y) * (1.0 / N)
    yc = y - mean
    var = segsum(yc * yc, terms=1) * (1.0 / N)
    yn = yc * lax.rsqrt(var + RW_GN_EPS) * vec(lg_ref) + vec(lb_ref)
    bonus = segsum(r * k * vec(rk_ref)) * v
    out = (yn + bonus) * pairs(g_ref)
    for i, (c, p) in enumerate(cp):
        o_ref[0, c * L:(c + 1) * L, LANES * p:LANES * (p + 1)] = out[i].astype(o_ref.dtype)


def _rwkv_rec(r, k, v, ld, a, g, k_k, k_a, r_k, lnx_g, lnx_b, chunks_per_step=4):
    bsz, seq, d = r.shape
    L = RW_CHUNK * min(chunks_per_step, seq // RW_CHUNK)
    row = lambda b, j: (b, j, 0)
    fix = lambda b, j: (0, 0)
    return pl.pallas_call(
        _rwkv_rec_kernel,
        grid=(bsz, seq // L),
        in_specs=[pl.BlockSpec((1, L, d), row)] * 6 + [pl.BlockSpec((1, d), fix)] * 5,
        out_specs=pl.BlockSpec((1, L, d), row),
        out_shape=jax.ShapeDtypeStruct((bsz, seq, d), BF16),
        scratch_shapes=[pltpu.VMEM((RW_PAIRS, LANES, LANES), F32)],
        compiler_params=_cparams("parallel", "arbitrary"),
        name="rwkv_rec",
    )(r, k, v, ld, a, g, k_k, k_a, r_k, lnx_g, lnx_b)


def _rwkv_sublayer(h, mu, w_r, w_k, w_v, w0, w1, w2, a0, a1, a2, g1, g2, k_k, k_a, r_k,
                   lnx_g, lnx_b, w_o):
    bsz, seq, d = h.shape
    bf = lambda w: w.astype(BF16)
    row = lambda w: w.reshape(1, d)
    mu8 = jnp.pad(mu, ((0, SUBLANES - mu.shape[0]), (0, 0)))
    r, k, v, ld, a, gg = _rwkv_in(h, mu8, bf(w_r), bf(w_k), bf(w_v), row(w0), bf(w1), bf(w2),
                                  row(a0), bf(a1), bf(a2), bf(g1), bf(g2))
    y = _rwkv_rec(r, k, v, ld, a, gg, row(k_k), row(k_a), row(r_k), row(lnx_g), row(lnx_b))
    return _proj_residual(y.reshape(bsz * seq, d), bf(w_o), h.reshape(bsz * seq, d)).reshape(bsz, seq, d)


def _chunk_attn_kernel(q_ref, k_ref, v_ref, tbl_ref, o_ref):
    seq = q_ref.shape[1]
    blocks_back = (ATT_WIN - ATT_TQ) // ATT_TQ
    lane = lax.broadcasted_iota(jnp.int32, (ATT_TQ, LANES), 1)
    nq = seq // ATT_TQ

    def window(qi):
        back = min(qi, blocks_back)
        ws = (qi - back) * ATT_TQ
        width = (back + 1) * ATT_TQ
        off = (blocks_back - back) * ATT_TQ
        return ws, width, off

    def scores(qi):
        ws, width, _ = window(qi)
        kw = k_ref[0, ws:ws + width, :]
        q = q_ref[0, qi * ATT_TQ:(qi + 1) * ATT_TQ, :]
        out = []
        for hh in range(2):
            qh = jnp.where((lane // ATT_HEAD_DIM) == hh, q, jnp.zeros_like(q))
            out.append(lax.dot_general(qh, kw, (((1,), (1,)), ((), ())), preferred_element_type=F32))
        return out

    s_next = scores(0)
    for qi in range(nq):
        s_cur = s_next
        if qi + 1 < nq:
            s_next = scores(qi + 1)
        ws, width, off = window(qi)
        vw = v_ref[0, ws:ws + width, :]
        vlane = lax.broadcasted_iota(jnp.int32, vw.shape, 1)
        pv = []
        for hh in range(2):
            s = s_cur[hh] + tbl_ref[0, hh, :, off:off + width]
            m = jnp.max(s, axis=-1, keepdims=True)
            e = jnp.exp((s - m).astype(BF16))
            vh = jnp.where((vlane // ATT_HEAD_DIM) == hh, vw, jnp.ones_like(vw))
            pv.append(jnp.dot(e, vh, preferred_element_type=F32))
        num = jnp.where(lane < ATT_HEAD_DIM, pv[0], pv[1])
        den = jnp.where(lane < ATT_HEAD_DIM, pltpu.roll(pv[0], ATT_HEAD_DIM, 1), pltpu.roll(pv[1], ATT_HEAD_DIM, 1))
        o_ref[0, qi * ATT_TQ:(qi + 1) * ATT_TQ, :] = (num / den).astype(o_ref.dtype)


def _chunk_attn(qkv, tbl):
    bsz, seq, _ = qkv.shape
    return pl.pallas_call(
        _chunk_attn_kernel,
        grid=(bsz, ATT_PAIRS),
        in_specs=[pl.BlockSpec((1, seq, LANES), lambda b, p: (b, 0, p)),
                  pl.BlockSpec((1, seq, LANES), lambda b, p: (b, 0, ATT_PAIRS + p)),
                  pl.BlockSpec((1, seq, LANES), lambda b, p: (b, 0, 2 * ATT_PAIRS + p)),
                  pl.BlockSpec((1, 2, ATT_TQ, ATT_TBL), lambda b, p: (p, 0, 0, 0))],
        out_specs=pl.BlockSpec((1, seq, LANES), lambda b, p: (b, 0, p)),
        out_shape=jax.ShapeDtypeStruct((bsz, seq, D_MODEL), BF16),
        compiler_params=_cparams("parallel", "parallel"),
        name="chunk_attn",
    )(qkv, qkv, qkv, tbl)


ATT_PERIOD = ATT_TBL + ATT_TQ


def _attn_table_kernel(vec_ref, o_ref):
    p = pl.program_id(0)
    ic = lax.broadcasted_iota(jnp.int32, (ATT_TQ, ATT_TBL), 0) // CHUNK
    mc = lax.broadcasted_iota(jnp.int32, (ATT_TQ, ATT_TBL), 1) // CHUNK
    valid = (mc >= ic) & (mc <= ic + BAND_CHUNKS - 1)
    for hh in range(2):
        row = vec_ref[pl.ds(2 * p + hh, 1), :]
        skew = pltpu.roll(jnp.broadcast_to(row, (ATT_TQ, ATT_PERIOD)), 0, 1, stride=1, stride_axis=0)
        o_ref[0, hh] = jnp.where(valid, skew[:, :ATT_TBL], NEG_INF)


def _attn_bias_table(rel_bias):
    left = ATT_WIN - ATT_TQ
    j = jnp.arange(ATT_PERIOD)
    diff = jnp.where(j < ATT_TBL, j, j - ATT_PERIOD)
    vec = rel_bias[:, jnp.clip(left - diff, -MAX_REL, MAX_REL) + MAX_REL].astype(F32)
    return pl.pallas_call(
        _attn_table_kernel,
        grid=(ATT_PAIRS,),
        in_specs=[pl.BlockSpec(vec.shape, lambda p: (0, 0))],
        out_specs=pl.BlockSpec((1, 2, ATT_TQ, ATT_TBL), lambda p: (p, 0, 0, 0)),
        out_shape=jax.ShapeDtypeStruct((ATT_PAIRS, 2, ATT_TQ, ATT_TBL), F32),
        compiler_params=_cparams("parallel"),
        name="attn_table",
    )(vec)


def _attn_sublayer(h, w_qkv, rel_bias, w_o):
    bsz, seq, d = h.shape
    assert seq % ATT_TQ == 0 and seq >= ATT_WIN
    scale = jnp.concatenate([jnp.full((d,), ATT_HEAD_DIM ** -0.5, F32), jnp.ones((2 * d,), F32)])
    qkv = _proj(h.reshape(bsz * seq, d), (w_qkv * scale).astype(BF16), BF16)
    o = _chunk_attn(qkv.reshape(bsz, seq, 3 * d), _attn_bias_table(rel_bias))
    return _proj_residual(o.reshape(bsz * seq, d), w_o.astype(BF16), h.reshape(bsz * seq, d)).reshape(bsz, seq, d)


def _mem_attn_kernel(z_ref, gin_ref, bin_ref, kv_ref, wq_ref, wo_ref, o_ref, *, sub):
    tiles = [slice(r0, r0 + sub) for r0 in range(0, z_ref.shape[1], sub)]

    def front(rows):
        h = _ln(z_ref[0, rows, :], gin_ref[...], bin_ref[...])
        q = jnp.dot(h.astype(BF16), wq_ref[...], preferred_element_type=F32).astype(BF16)
        return h, [lax.dot_general(q[:, hh * MEM_HEAD_DIM:(hh + 1) * MEM_HEAD_DIM],
                                   kv_ref[0, :, hh * MEM_HEAD_DIM:(hh + 1) * MEM_HEAD_DIM],
                                   (((1,), (1,)), ((), ())), preferred_element_type=F32) for hh in range(MEM_HEADS)]

    def back(rows, h, scores):
        outs = []
        for hh, s in enumerate(scores):
            vh = kv_ref[0, :, D_MODEL + hh * MEM_HEAD_DIM:D_MODEL + (hh + 1) * MEM_HEAD_DIM]
            m = jnp.max(s, axis=-1, keepdims=True)
            e = jnp.exp(s - m)
            l = jnp.sum(e, axis=-1, keepdims=True)
            outs.append((jnp.dot(e.astype(BF16), vh, preferred_element_type=F32) / l).astype(BF16))
        y = jnp.dot(jnp.concatenate(outs, axis=1), wo_ref[...], preferred_element_type=F32)
        o_ref[0, rows, :] = DEEPNORM_ALPHA * h + y

    nxt = front(tiles[0])
    for i, rows in enumerate(tiles):
        cur = nxt
        if i + 1 < len(tiles):
            nxt = front(tiles[i + 1])
        back(rows, *cur)


def _mem_sublayer(z, g_in, b_in, mem, w_q, w_kv, w_o, tm=1024, sub=256):
    bsz, seq, d = z.shape
    nm = mem.shape[1]
    tm = min(tm, seq)
    kv = _proj(mem.reshape(bsz * nm, d), w_kv.astype(BF16), BF16).reshape(bsz, nm, 2 * d)
    wq = (w_q * (MEM_HEAD_DIM ** -0.5)).astype(BF16)
    row = lambda b_, i: (b_, i, 0)
    fix = lambda b_, i: (0, 0)
    return pl.pallas_call(
        functools.partial(_mem_attn_kernel, sub=min(sub, tm)),
        grid=(bsz, seq // tm),
        in_specs=[pl.BlockSpec((1, tm, d), row),
                  pl.BlockSpec((1, d), fix), pl.BlockSpec((1, d), fix),
                  pl.BlockSpec((1, nm, 2 * d), lambda b_, i: (b_, 0, 0)),
                  pl.BlockSpec((d, d), fix, pipeline_mode=pl.Buffered(1)),
                  pl.BlockSpec((d, d), fix, pipeline_mode=pl.Buffered(1))],
        out_specs=pl.BlockSpec((1, tm, d), row),
        out_shape=jax.ShapeDtypeStruct((bsz, seq, d), F32),
        compiler_params=_cparams("parallel", "parallel"),
        name="mem_attn",
    )(z, g_in, b_in, kv, wq, w_o.astype(BF16))


def _mlp_kernel(z_ref, gin_ref, bin_ref, w1_ref, w2_ref, g_ref, b_ref, o_ref, *, sub):
    for r0 in range(0, z_ref.shape[0], sub):
        rows = slice(r0, r0 + sub)
        h = _ln(z_ref[rows, :], gin_ref[...], bin_ref[...])
        u = jnp.dot(h.astype(BF16), w1_ref[...], preferred_element_type=F32)
        u = jnp.maximum(u, 0.0)
        y = jnp.dot((u * u).astype(BF16), w2_ref[...], preferred_element_type=F32)
        o_ref[rows, :] = _residual_ln(h, y, g_ref[...], b_ref[...])


def _mlp_sublayer(z, g_in, b_in, w1, w2, g, b, tm=1024, sub=256):
    bsz, seq, d = z.shape
    t = bsz * seq
    tm = min(tm, t)
    ff = w1.shape[1]
    resident = pl.Buffered(1)
    out = pl.pallas_call(
        functools.partial(_mlp_kernel, sub=min(sub, tm)),
        grid=(t // tm,),
        in_specs=[pl.BlockSpec((tm, d), lambda i: (i, 0)),
                  pl.BlockSpec((1, d), lambda i: (0, 0)),
                  pl.BlockSpec((1, d), lambda i: (0, 0)),
                  pl.BlockSpec((d, ff), lambda i: (0, 0), pipeline_mode=resident),
                  pl.BlockSpec((ff, d), lambda i: (0, 0), pipeline_mode=resident),
                  pl.BlockSpec((1, d), lambda i: (0, 0)),
                  pl.BlockSpec((1, d), lambda i: (0, 0))],
        out_specs=pl.BlockSpec((tm, d), lambda i: (i, 0)),
        out_shape=jax.ShapeDtypeStruct((t, d), F32),
        compiler_params=_cparams("parallel"),
        name="mlp",
    )(z.reshape(t, d), g_in, b_in, w1.astype(BF16), w2.astype(BF16), g, b)
    return out.reshape(bsz, seq, d)


def kernel(x, mem, ln_g, ln_b, lru_w_in, lru_conv_w, lru_conv_b, lru_gate_w, lru_gate_b, lru_lambda, lru_w_out, rw_mu, rw_w_r, rw_w_k, rw_w_v, rw_w0, rw_w1, rw_w2, rw_a0, rw_a1, rw_a2, rw_g1, rw_g2, rw_k_k, rw_k_a, rw_r_k, rw_lnx_g, rw_lnx_b, rw_w_o, ca_w_qkv, ca_rel_bias, ca_w_o, mx_w_q, mx_w_kv, mx_w_o, mlp_w1, mlp_w2):
    h = x
    for i in range(DEPTH):
        kind, j = i % N_MIXERS, i // N_MIXERS
        g = lambda s: ln_g[i, s][None, :]
        b = lambda s: ln_b[i, s][None, :]
        if kind == 0:
            z = _rglru_sublayer(h, lru_w_in[j], lru_conv_w[j], lru_conv_b[j], lru_gate_w[j], lru_gate_b[j],
                                lru_lambda[j], lru_w_out[j])
        elif kind == 1:
            z = _rwkv_sublayer(h, rw_mu[j], rw_w_r[j], rw_w_k[j], rw_w_v[j], rw_w0[j], rw_w1[j], rw_w2[j],
                               rw_a0[j], rw_a1[j], rw_a2[j], rw_g1[j], rw_g2[j], rw_k_k[j], rw_k_a[j],
                               rw_r_k[j], rw_lnx_g[j], rw_lnx_b[j], rw_w_o[j])
        else:
            z = _attn_sublayer(h, ca_w_qkv[j], ca_rel_bias[j], ca_w_o[j])
        z = _mem_sublayer(z, g(0), b(0), mem, mx_w_q[i], mx_w_kv[i], mx_w_o[i])
        h = _mlp_sublayer(z, g(1), b(1), mlp_w1[i], mlp_w2[i], g(2), b(2))
    return h
```

```python
import functools

import jax
import jax.numpy as jnp
from jax import lax
from jax.experimental import pallas as pl
from jax.experimental.pallas import tpu as pltpu

F32 = jnp.float32
BF16 = jnp.bfloat16

D_MODEL = 1024
DEPTH = 4
CHUNK = 64
N_MIXERS = 3
DEEPNORM_ALPHA = (2 * DEPTH) ** 0.25
LN_EPS = 1e-5

D_RNN = 1344
LRU_BLOCKS = 16
LRU_BLOCK_SIZE = D_RNN // LRU_BLOCKS
CONV_WIDTH = 4
RG_LRU_C = 8.0
LANES = 128
SUBLANES = 8
D_RNN_PAD = -(-D_RNN // LANES) * LANES

RW_HEAD_SIZE = 64
RW_PAIRS = D_MODEL // LANES
RW_GN_EPS = 64e-5
RW_CHUNK = 64

ATT_HEADS = 16
ATT_HEAD_DIM = D_MODEL // ATT_HEADS
ATT_PAIRS = D_MODEL // LANES
BAND_CHUNKS = 9
MAX_REL = 2 * CHUNK
NEG_INF = -1e30
ATT_TQ = 256
ATT_GROUP = 3
ATT_WIN = ATT_TQ + (BAND_CHUNKS - 1) * CHUNK
ATT_TBL = ATT_WIN + (BAND_CHUNKS - 1) * CHUNK

MEM_HEADS = 4
MEM_HEAD_DIM = D_MODEL // MEM_HEADS

VMEM_LIMIT = 56 * 1024 * 1024


def _cparams(*sem):
    return pltpu.CompilerParams(dimension_semantics=sem, vmem_limit_bytes=VMEM_LIMIT)


def _residual_ln(hres, y, g, b):
    return _ln(DEEPNORM_ALPHA * hres + y, g, b)


def _ln(z, g, b):
    mu = jnp.mean(z, axis=-1, keepdims=True)
    zc = z - mu
    var = jnp.mean(zc * zc, axis=-1, keepdims=True)
    return zc * lax.rsqrt(var + LN_EPS) * g + b


def _gelu_tanh(x):
    return 0.5 * x * (1.0 + jnp.tanh(0.7978845608028654 * (x + 0.044715 * (x * x * x))))


def _softplus(x, wide=False):
    t = jnp.exp(-jnp.abs(x))
    return jnp.maximum(x, 0.0) + (jnp.log(1.0 + t) if wide else jnp.log1p(t))


def _sigmoid(x):
    return 0.5 * jnp.tanh(0.5 * x) + 0.5


def _split2(x):
    hi = x.astype(BF16)
    lo = (x - hi.astype(F32)).astype(BF16)
    return hi, lo


def _proj_kernel(x_ref, w_ref, o_ref, *, chunk):
    x = x_ref[...].astype(BF16)
    n = w_ref.shape[1]
    for n0 in range(0, n, chunk):
        y = jnp.dot(x, w_ref[:, n0:n0 + chunk], preferred_element_type=F32)
        o_ref[:, n0:n0 + chunk] = y.astype(o_ref.dtype)


def _proj(x2d, w, out_dtype, tm=512, chunk=512):
    t, k = x2d.shape
    n = w.shape[1]
    tm = min(tm, t)
    return pl.pallas_call(
        functools.partial(_proj_kernel, chunk=chunk),
        grid=(t // tm,),
        in_specs=[pl.BlockSpec((tm, k), lambda i: (i, 0)),
                  pl.BlockSpec((k, n), lambda i: (0, 0))],
        out_specs=pl.BlockSpec((tm, n), lambda i: (i, 0)),
        out_shape=jax.ShapeDtypeStruct((t, n), out_dtype),
        compiler_params=_cparams("parallel"),
        name="proj",
    )(x2d, w)


def _proj_residual_kernel(y_ref, w_ref, h_ref, o_ref, *, sub):
    for r0 in range(0, y_ref.shape[0], sub):
        rows = slice(r0, r0 + sub)
        y = jnp.dot(y_ref[rows, :].astype(BF16), w_ref[...], preferred_element_type=F32)
        o_ref[rows, :] = DEEPNORM_ALPHA * h_ref[rows, :] + y


def _proj_residual(y2d, w, h2d, tm=2048, sub=256):
    t, k = y2d.shape
    d = w.shape[1]
    tm = min(tm, t)
    return pl.pallas_call(
        functools.partial(_proj_residual_kernel, sub=min(sub, tm)),
        grid=(t // tm,),
        in_specs=[pl.BlockSpec((tm, k), lambda i: (i, 0)),
                  pl.BlockSpec((k, d), lambda i: (0, 0)),
                  pl.BlockSpec((tm, d), lambda i: (i, 0))],
        out_specs=pl.BlockSpec((tm, d), lambda i: (i, 0)),
        out_shape=jax.ShapeDtypeStruct((t, d), F32),
        compiler_params=_cparams("parallel"),
        name="proj_residual",
    )(y2d, w, h2d)


LRU_GATE_TILE = 256
LRU_GATE_WIN = 512


def _lru_gate_window_starts():
    starts = []
    for n0 in range(0, D_RNN_PAD, LRU_GATE_TILE):
        first_block = min(n0, D_RNN - 1) // LRU_BLOCK_SIZE
        last_block = min(n0 + LRU_GATE_TILE - 1, D_RNN - 1) // LRU_BLOCK_SIZE
        lo = (first_block * LRU_BLOCK_SIZE) // LANES * LANES
        lo = min(lo, D_RNN_PAD - LRU_GATE_WIN)
        assert (last_block + 1) * LRU_BLOCK_SIZE <= lo + LRU_GATE_WIN
        starts.append(lo)
    return starts


LRU_HIST = (CONV_WIDTH - 1) * SUBLANES


def _lru_kernel(h_ref, win_ref, cw_ref, cb_ref, gw_ref, gb_ref, lam_ref, wout_ref, o_ref, xbuf, hstate):
    @pl.when(pl.program_id(0) == 0)
    def _():
        xbuf[:, 0:LRU_HIST, :] = jnp.zeros((xbuf.shape[0], LRU_HIST, xbuf.shape[2]), F32)
        hstate[...] = jnp.zeros(hstate.shape, F32)

    ng = h_ref.shape[0] // SUBLANES
    fronts = [_lru_front(gi, h_ref, win_ref, cw_ref, cb_ref, gw_ref, xbuf) for gi in range(ng)]
    for gi in range(ng):
        _lru_back(gi, *fronts[gi], gb_ref, lam_ref, wout_ref, o_ref, hstate)


def _lru_front(gi, h_ref, win_ref, cw_ref, cb_ref, gw_ref, xbuf):
    tt, d = h_ref.shape[1], h_ref.shape[2]
    c = cw_ref.shape[1]
    ts = tt * SUBLANES
    hres = h_ref[gi * SUBLANES:(gi + 1) * SUBLANES]
    ht = pltpu.einshape("btd->tbd", hres).reshape(ts, d)
    u = jnp.dot(ht.astype(BF16), win_ref[...], preferred_element_type=F32)
    gate_branch = _gelu_tanh(u[:, :c])
    x = u[:, c:]
    xbuf[gi, LRU_HIST:LRU_HIST + ts, :] = x
    xr = cb_ref[...] + cw_ref[3:4, :] * x
    for k in range(CONV_WIDTH - 1):
        xr = xr + cw_ref[k:k + 1, :] * xbuf[gi, k * SUBLANES:k * SUBLANES + ts, :]
    xbuf[gi, 0:LRU_HIST, :] = xbuf[gi, ts:ts + LRU_HIST, :]

    xb = xr.astype(BF16)
    tiles = [jnp.dot(xb[:, lo:lo + LRU_GATE_WIN], gw_ref[n], preferred_element_type=F32)
             for n, lo in enumerate(_lru_gate_window_starts())]
    r_pre = jnp.concatenate([t[:, :LRU_GATE_TILE] for t in tiles], axis=1)[:, :c]
    i_pre = jnp.concatenate([t[:, LRU_GATE_TILE:] for t in tiles], axis=1)[:, :c]
    return hres, gate_branch, xr, r_pre, i_pre


def _lru_back(gi, hres, gate_branch, xr, r_pre, i_pre, gb_ref, lam_ref, wout_ref, o_ref, hstate):
    ts, c = xr.shape
    tt = ts // SUBLANES
    r_gate = _sigmoid(r_pre + gb_ref[0:1, :])
    i_gate = _sigmoid(i_pre + gb_ref[1:2, :])
    log_a = (-RG_LRU_C) * r_gate * _softplus(-lam_ref[...])
    a = jnp.exp(log_a)
    bt = jnp.sqrt(-jnp.tanh(log_a) * (a * a + 1.0)) * (i_gate * xr)

    h = hstate[gi]
    steps = []
    for t in range(tt):
        slab = slice(t * SUBLANES, (t + 1) * SUBLANES)
        h = a[slab, :] * h + bt[slab, :]
        steps.append(h)
    hstate[gi] = h
    hs = jnp.concatenate(steps, axis=0)

    y = jnp.dot((hs * gate_branch).astype(BF16), wout_ref[...], preferred_element_type=F32)
    yb = pltpu.einshape("tbd->btd", y.reshape(tt, SUBLANES, y.shape[1]))
    o_ref[gi * SUBLANES:(gi + 1) * SUBLANES] = DEEPNORM_ALPHA * hres + yb


def _lru(h, win, cw, cb, gw, gb, lam, wout, tt=32):
    bsz, seq, d = h.shape
    c = cw.shape[1]
    assert bsz % SUBLANES == 0
    tt = min(tt, seq)
    ng = bsz // SUBLANES
    resident = lambda w: pl.BlockSpec(w.shape, lambda j: (0,) * w.ndim, pipeline_mode=pl.Buffered(1))
    return pl.pallas_call(
        _lru_kernel,
        grid=(seq // tt,),
        in_specs=[pl.BlockSpec((bsz, tt, d), lambda j: (0, j, 0))]
                 + [resident(w) for w in (win, cw, cb, gw, gb, lam, wout)],
        out_specs=pl.BlockSpec((bsz, tt, d), lambda j: (0, j, 0)),
        out_shape=jax.ShapeDtypeStruct((bsz, seq, d), F32),
        scratch_shapes=[pltpu.VMEM((ng, LRU_HIST + tt * SUBLANES, c), F32), pltpu.VMEM((ng, SUBLANES, c), F32)],
        compiler_params=_cparams("arbitrary"),
        name="lru",
    )(h, win, cw, cb, gw, gb, lam, wout)


def _pad_last(x, n):
    return jnp.pad(x, [(0, 0)] * (x.ndim - 1) + [(0, n - x.shape[-1])])


def _rglru_sublayer(h, w_in, conv_w, conv_b, gate_w, gate_b, lam, w_out):
    bsz, seq, d = h.shape
    c = D_RNN_PAD
    w_cat = jnp.concatenate([_pad_last(w_in[:, :D_RNN], c), _pad_last(w_in[:, D_RNN:], c)], axis=1).astype(BF16)
    eye = jnp.eye(LRU_BLOCKS, dtype=F32)
    dense = jnp.einsum('gncd,nm->gncmd', gate_w, eye).reshape(2, D_RNN, D_RNN)
    starts = _lru_gate_window_starts()
    dense = jnp.pad(dense, ((0, 0), (0, c - D_RNN), (0, len(starts) * LRU_GATE_TILE - D_RNN)))
    gw = jnp.stack([jnp.concatenate([dense[gi, lo:lo + LRU_GATE_WIN, n * LRU_GATE_TILE:(n + 1) * LRU_GATE_TILE]
                                     for gi in range(2)], axis=1) for n, lo in enumerate(starts)]).astype(BF16)
    wout = jnp.pad(w_out, ((0, c - D_RNN), (0, 0))).astype(BF16)
    return _lru(h, w_cat, _pad_last(conv_w, c), _pad_last(conv_b[None, :], c), gw,
                _pad_last(gate_b, c), _pad_last(lam[None, :], c), wout)


def _rwkv_in_kernel(x_ref, xp_ref, mu_ref, wr_ref, wk_ref, wv_ref, w0_ref, w1_ref, w2_ref,
                    a0_ref, a1_ref, a2_ref, g1_ref, g2_ref,
                    r_ref, k_ref, v_ref, ld_ref, a_ref, g_ref, *, sub):
    i = pl.program_id(1)
    x_all = x_ref[0]
    row = lax.broadcasted_iota(jnp.int32, x_all.shape, 0)
    prev = jnp.where(i > 0, xp_ref[0, SUBLANES - 1:SUBLANES, :], 0.0)
    xx_all = jnp.where(row == 0, prev, pltpu.roll(x_all, 1, 0)) - x_all
    dot = lambda p, q: jnp.dot(p, q, preferred_element_type=F32)
    for r0 in range(0, x_all.shape[0], sub):
        rows = slice(r0, r0 + sub)
        x, xx = x_all[rows, :], xx_all[rows, :]
        mix = lambda c: (x + xx * mu_ref[c:c + 1, :]).astype(BF16)
        r_ref[0, rows, :] = dot(mix(0), wr_ref[...]).astype(r_ref.dtype)
        k_ref[0, rows, :] = dot(mix(2), wk_ref[...]).astype(k_ref.dtype)
        v_ref[0, rows, :] = dot(mix(3), wv_ref[...]).astype(v_ref.dtype)
        wl = w0_ref[...] + dot(jnp.tanh(dot(mix(1), w1_ref[...])).astype(BF16), w2_ref[...])
        w_log = -_softplus(-wl, wide=True) - 0.5
        ld_ref[0, rows, :] = -jnp.exp(w_log)
        a_ref[0, rows, :] = _sigmoid(a0_ref[...] + dot(dot(mix(4), a1_ref[...]).astype(BF16), a2_ref[...]))
        g_ref[0, rows, :] = dot(_sigmoid(dot(mix(5), g1_ref[...])).astype(BF16), g2_ref[...]).astype(g_ref.dtype)


def _rwkv_in(h, mu, wr, wk, wv, w0, w1, w2, a0, a1, a2, g1, g2, tm=512, sub=256):
    bsz, seq, d = h.shape
    tm = min(tm, seq)
    row = lambda b, i: (b, i, 0)
    fix = lambda b, i: (0, 0)
    prev = lambda b, i: (b, jnp.maximum(i * (tm // SUBLANES) - 1, 0), 0)
    ws = [mu, wr, wk, wv, w0, w1, w2, a0, a1, a2, g1, g2]
    return pl.pallas_call(
        functools.partial(_rwkv_in_kernel, sub=min(sub, tm)),
        grid=(bsz, seq // tm),
        in_specs=[pl.BlockSpec((1, tm, d), row), pl.BlockSpec((1, SUBLANES, d), prev)]
                 + [pl.BlockSpec(w.shape, fix) for w in ws],
        out_specs=[pl.BlockSpec((1, tm, d), row)] * 6,
        out_shape=[jax.ShapeDtypeStruct((bsz, seq, d), dt) for dt in (BF16, BF16, BF16, F32, F32, BF16)],
        compiler_params=_cparams("parallel", "parallel"),
        name="rwkv_in",
    )(h, h, *ws)


def _bdot(a, b, dims):
    return jnp.einsum(dims, a, b, preferred_element_type=F32)


def _rwkv_rec_kernel(r_ref, k_ref, v_ref, ld_ref, a_ref, g_ref, kk_ref, ka_ref, rk_ref, lg_ref, lb_ref,
                     o_ref, state):
    @pl.when(pl.program_id(1) == 0)
    def _():
        state[...] = jnp.zeros(state.shape, F32)

    L = RW_CHUNK
    P = RW_PAIRS
    N = RW_HEAD_SIZE
    nc = r_ref.shape[1] // L
    B = nc * P
    cp = [(c, p) for c in range(nc) for p in range(P)]
    pairs = lambda ref: jnp.stack([ref[0, c * L:(c + 1) * L, LANES * p:LANES * (p + 1)]
                                   for c, p in cp]).astype(F32)
    vec = lambda ref: jnp.stack([ref[:, LANES * p:LANES * (p + 1)] for _, p in cp])

    lane = lax.broadcasted_iota(jnp.int32, (1, 1, LANES), 2)
    head0 = lane < N
    ri = lax.broadcasted_iota(jnp.int32, (LANES, LANES), 0)
    ci = lax.broadcasted_iota(jnp.int32, (LANES, LANES), 1)
    seg = ((ri // N) == (ci // N)).astype(BF16)

    def segsum(x, terms=2):
        flat = x.reshape(B * L, LANES)
        hi = flat.astype(BF16)
        out = jnp.dot(hi, seg, preferred_element_type=F32)
        if terms == 2:
            lo = (flat - hi.astype(F32)).astype(BF16)
            out = out + jnp.dot(lo, seg, preferred_element_type=F32)
        return out.reshape(B, L, LANES)

    r = pairs(r_ref)
    k = pairs(k_ref)
    v = pairs(v_ref)
    a = pairs(a_ref)

    kk = k * vec(kk_ref)
    kk = kk * lax.rsqrt(jnp.maximum(segsum(kk * kk, terms=1), 1e-24))
    k = k * (1.0 + (a - 1.0) * vec(ka_ref))
    aa = -kk
    bb = kk * a

    ld_full = ld_ref[0]
    tr = lax.broadcasted_iota(jnp.int32, (nc * L, nc * L), 0)
    tc = lax.broadcasted_iota(jnp.int32, (nc * L, nc * L), 1)
    tri = ((tc <= tr) & (tc // L == tr // L)).astype(BF16)
    cum_full = sum(jnp.dot(tri, t, preferred_element_type=F32) for t in _split2(ld_full))
    cum = jnp.stack([cum_full[c * L:(c + 1) * L, LANES * p:LANES * (p + 1)] for c, p in cp])
    ld = pairs(ld_ref)
    gam = jnp.exp(cum)
    gam_prev = jnp.exp(cum - ld)
    cum_end = cum[:, L - 1:L, :]
    r_t = r * gam
    a_t = aa * gam_prev
    k_t = k * jnp.exp(-cum)
    b_t = bb * jnp.exp(-cum)
    k_e = k * jnp.exp(cum_end - cum)
    b_e = bb * jnp.exp(cum_end - cum)

    def stacked(x):
        return jnp.concatenate([jnp.where(head0, x, 0.0), jnp.where(head0, 0.0, x)], axis=1)

    ar = jnp.concatenate([stacked(a_t), stacked(r_t)], axis=1).astype(BF16)
    kb = jnp.concatenate([stacked(k_t), stacked(b_t)], axis=1).astype(BF16)
    v_s = stacked(v)

    s4 = _bdot(ar, kb, 'pik,pjk->pij')
    M = 2 * L
    si = lax.broadcasted_iota(jnp.int32, (1, M, M), 1)
    sj = lax.broadcasted_iota(jnp.int32, (1, M, M), 2)
    same = (si // L) == (sj // L)
    strict = same & ((sj % L) < (si % L))
    incl = same & ((sj % L) <= (si % L))
    a_k = jnp.where(strict, s4[:, :M, :M], 0.0)
    a_b = jnp.where(strict, s4[:, :M, M:], 0.0)
    r_k = jnp.where(incl, s4[:, M:, :M], 0.0)
    r_b = jnp.where(incl, s4[:, M:, M:], 0.0)

    eye = (si == sj).astype(F32)
    lower1 = (si // 2 == sj // 2) & (si % 2 == 1) & (sj % 2 == 0)
    T = eye + jnp.where(lower1, a_b, 0.0)
    ab16 = a_b.astype(BF16)
    s = 2
    hi_ = lax.broadcasted_iota(jnp.int32, (1, M // 2, M), 1)
    hj_ = lax.broadcasted_iota(jnp.int32, (1, M // 2, M), 2)
    while s < L:
        tb = T.astype(BF16)
        if s < SUBLANES:
            cm = (si // (2 * s) == sj // (2 * s)) & (si % (2 * s) >= s) & (sj % (2 * s) < s)
            tat = _bdot(_bdot(tb, ab16, 'pij,pjk->pik').astype(BF16), tb, 'pij,pjk->pik')
            T = T + jnp.where(cm, tat, 0.0)
        else:
            nblk = M // (2 * s)
            low = jnp.concatenate([T[:, (2 * k + 1) * s:(2 * k + 2) * s, :] for k in range(nblk)], axis=1)
            upd = _bdot(_bdot(low.astype(BF16), ab16, 'pij,pjk->pik').astype(BF16), tb, 'pij,pjk->pik')
            upd = jnp.where((hj_ // (2 * s) == hi_ // s) & (hj_ % (2 * s) < s), upd, 0.0)
            zero = jnp.zeros((B, s, M), F32)
            T = T + jnp.concatenate([blk for k in range(nblk) for blk in (zero, upd[:, k * s:(k + 1) * s, :])],
                                    axis=1)
        s *= 2

    v16 = v_s.astype(BF16)
    akv = _bdot(a_k.astype(BF16), v16, 'pij,pjl->pil')
    t16 = T.astype(BF16)
    rkb = jnp.concatenate([r_k, r_b], axis=2).astype(BF16)
    kb_e = jnp.concatenate([stacked(k_e), stacked(b_e)], axis=1).astype(BF16)
    gam_end = jnp.exp(cum_end)

    s_val = state[...]
    y_parts = []
    for c in range(nc):
        sl = slice(c * P, (c + 1) * P)
        ah = _bdot(ar[sl], s_val.astype(BF16), 'pik,plk->pil')
        x_s = ah[:, :M, :] + akv[sl]
        u_s = _bdot(t16[sl], x_s.astype(BF16), 'pij,pjl->pil')
        vu = jnp.concatenate([v16[sl], u_s.astype(BF16)], axis=1)
        y_parts.append(ah[:, M:, :] + _bdot(rkb[sl], vu, 'pij,pjl->pil'))
        s_val = s_val * gam_end[sl] + _bdot(vu, kb_e[sl], 'pti,ptj->pij')
    state[...] = s_val
    y_s = jnp.concatenate(y_parts, axis=0)
    y = y_s[:, :L, :] + y_s[:, L:, :]

    mean = segsum(y) * (1.0 / N)
    yc = y - mean
    var = segsum(yc * yc, terms=1) * (1.0 / N)
    yn = yc * lax.rsqrt(var + RW_GN_EPS) * vec(lg_ref) + vec(lb_ref)
    bonus = segsum(r * k * vec(rk_ref)) * v
    out = (yn + bonus) * pairs(g_ref)
    for i, (c, p) in enumerate(cp):
        o_ref[0, c * L:(c + 1) * L, LANES * p:LANES * (p + 1)] = out[i].astype(o_ref.dtype)


def _rwkv_rec(r, k, v, ld, a, g, k_k, k_a, r_k, lnx_g, lnx_b, chunks_per_step=4):
    bsz, seq, d = r.shape
    L = RW_CHUNK * min(chunks_per_step, seq // RW_CHUNK)
    row = lambda b, j: (b, j, 0)
    fix = lambda b, j: (0, 0)
    return pl.pallas_call(
        _rwkv_rec_kernel,
        grid=(bsz, seq // L),
        in_specs=[pl.BlockSpec((1, L, d), row)] * 6 + [pl.BlockSpec((1, d), fix)] * 5,
        out_specs=pl.BlockSpec((1, L, d), row),
        out_shape=jax.ShapeDtypeStruct((bsz, seq, d), BF16),
        scratch_shapes=[pltpu.VMEM((RW_PAIRS, LANES, LANES), F32)],
        compiler_params=_cparams("parallel", "arbitrary"),
        name="rwkv_rec",
    )(r, k, v, ld, a, g, k_k, k_a, r_k, lnx_g, lnx_b)


def _rwkv_sublayer(h, mu, w_r, w_k, w_v, w0, w1, w2, a0, a1, a2, g1, g2, k_k, k_a, r_k,
                   lnx_g, lnx_b, w_o):
    bsz, seq, d = h.shape
    bf = lambda w: w.astype(BF16)
    row = lambda w: w.reshape(1, d)
    mu8 = jnp.pad(mu, ((0, SUBLANES - mu.shape[0]), (0, 0)))
    r, k, v, ld, a, gg = _rwkv_in(h, mu8, bf(w_r), bf(w_k), bf(w_v), row(w0), bf(w1), bf(w2),
                                  row(a0), bf(a1), bf(a2), bf(g1), bf(g2))
    y = _rwkv_rec(r, k, v, ld, a, gg, row(k_k), row(k_a), row(r_k), row(lnx_g), row(lnx_b))
    return _proj_residual(y.reshape(bsz * seq, d), bf(w_o), h.reshape(bsz * seq, d)).reshape(bsz, seq, d)


def _chunk_attn_kernel(q_ref, k_ref, v_ref, tbl_ref, o_ref):
    seq = q_ref.shape[1]
    blocks_back = (ATT_WIN - ATT_TQ) // ATT_TQ
    nq = seq // ATT_TQ
    lane = lax.broadcasted_iota(jnp.int32, (1, 1, LANES), 2)
    first = lane < ATT_HEAD_DIM

    def window(qi):
        back = min(qi, blocks_back)
        ws = (qi - back) * ATT_TQ
        width = (back + 1) * ATT_TQ
        off = (blocks_back - back) * ATT_TQ
        return ws, width, off

    def attend(blocks):
        _, width, off = window(blocks[0])
        q = jnp.stack([q_ref[0, qi * ATT_TQ:(qi + 1) * ATT_TQ, :] for qi in blocks])
        kw = jnp.stack([k_ref[0, window(qi)[0]:window(qi)[0] + width, :] for qi in blocks])
        vw = jnp.stack([v_ref[0, window(qi)[0]:window(qi)[0] + width, :] for qi in blocks])
        pv = []
        for hh in range(2):
            own = first if hh == 0 else jnp.logical_not(first)
            qh = jnp.where(own, q, jnp.zeros_like(q))
            s = jnp.einsum('gqd,gkd->gqk', qh, kw, preferred_element_type=F32)
            s = s + tbl_ref[0, hh, :, off:off + width]
            m = jnp.max(s, axis=-1, keepdims=True)
            e = jnp.exp((s - m).astype(BF16))
            vh = jnp.where(own, vw, jnp.ones_like(vw))
            pv.append(jnp.einsum('gqk,gkd->gqd', e, vh, preferred_element_type=F32))
        num = jnp.where(first, pv[0], pv[1])
        den = jnp.where(first, pltpu.roll(pv[0], ATT_HEAD_DIM, 2), pltpu.roll(pv[1], ATT_HEAD_DIM, 2))
        out = (num / den).astype(o_ref.dtype)
        for i, qi in enumerate(blocks):
            o_ref[0, qi * ATT_TQ:(qi + 1) * ATT_TQ, :] = out[i]

    full = list(range(min(blocks_back, nq), nq))
    for qi in range(min(blocks_back, nq)):
        attend([qi])
    for g0 in range(0, len(full), ATT_GROUP):
        attend(full[g0:g0 + ATT_GROUP])


def _chunk_attn(qkv, tbl):
    bsz, seq, _ = qkv.shape
    return pl.pallas_call(
        _chunk_attn_kernel,
        grid=(bsz, ATT_PAIRS),
        in_specs=[pl.BlockSpec((1, seq, LANES), lambda b, p: (b, 0, p)),
                  pl.BlockSpec((1, seq, LANES), lambda b, p: (b, 0, ATT_PAIRS + p)),
                  pl.BlockSpec((1, seq, LANES), lambda b, p: (b, 0, 2 * ATT_PAIRS + p)),
                  pl.BlockSpec((1, 2, ATT_TQ, ATT_TBL), lambda b, p: (p, 0, 0, 0))],
        out_specs=pl.BlockSpec((1, seq, LANES), lambda b, p: (b, 0, p)),
        out_shape=jax.ShapeDtypeStruct((bsz, seq, D_MODEL), BF16),
        compiler_params=_cparams("parallel", "parallel"),
        name="chunk_attn",
    )(qkv, qkv, qkv, tbl)


ATT_PERIOD = ATT_TBL + ATT_TQ


def _attn_table_kernel(vec_ref, o_ref):
    p = pl.program_id(0)
    ic = lax.broadcasted_iota(jnp.int32, (ATT_TQ, ATT_TBL), 0) // CHUNK
    mc = lax.broadcasted_iota(jnp.int32, (ATT_TQ, ATT_TBL), 1) // CHUNK
    valid = (mc >= ic) & (mc <= ic + BAND_CHUNKS - 1)
    for hh in range(2):
        row = vec_ref[pl.ds(2 * p + hh, 1), :]
        skew = pltpu.roll(jnp.broadcast_to(row, (ATT_TQ, ATT_PERIOD)), 0, 1, stride=1, stride_axis=0)
        o_ref[0, hh] = jnp.where(valid, skew[:, :ATT_TBL], NEG_INF)


def _attn_bias_table(rel_bias):
    left = ATT_WIN - ATT_TQ
    j = jnp.arange(ATT_PERIOD)
    diff = jnp.where(j < ATT_TBL, j, j - ATT_PERIOD)
    vec = rel_bias[:, jnp.clip(left - diff, -MAX_REL, MAX_REL) + MAX_REL].astype(F32)
    return pl.pallas_call(
        _attn_table_kernel,
        grid=(ATT_PAIRS,),
        in_specs=[pl.BlockSpec(vec.shape, lambda p: (0, 0))],
        out_specs=pl.BlockSpec((1, 2, ATT_TQ, ATT_TBL), lambda p: (p, 0, 0, 0)),
        out_shape=jax.ShapeDtypeStruct((ATT_PAIRS, 2, ATT_TQ, ATT_TBL), F32),
        compiler_params=_cparams("parallel"),
        name="attn_table",
    )(vec)


def _attn_sublayer(h, w_qkv, rel_bias, w_o):
    bsz, seq, d = h.shape
    assert seq % ATT_TQ == 0 and seq >= ATT_WIN
    scale = jnp.concatenate([jnp.full((d,), ATT_HEAD_DIM ** -0.5, F32), jnp.ones((2 * d,), F32)])
    qkv = _proj(h.reshape(bsz * seq, d), (w_qkv * scale).astype(BF16), BF16)
    o = _chunk_attn(qkv.reshape(bsz, seq, 3 * d), _attn_bias_table(rel_bias))
    return _proj_residual(o.reshape(bsz * seq, d), w_o.astype(BF16), h.reshape(bsz * seq, d)).reshape(bsz, seq, d)


def _mem_attn_kernel(z_ref, gin_ref, bin_ref, kv_ref, wq_ref, wo_ref, o_ref, *, sub):
    tiles = [slice(r0, r0 + sub) for r0 in range(0, z_ref.shape[1], sub)]

    def front(rows):
        h = _ln(z_ref[0, rows, :], gin_ref[...], bin_ref[...])
        q = jnp.dot(h.astype(BF16), wq_ref[...], preferred_element_type=F32).astype(BF16)
        return h, [lax.dot_general(q[:, hh * MEM_HEAD_DIM:(hh + 1) * MEM_HEAD_DIM],
                                   kv_ref[0, :, hh * MEM_HEAD_DIM:(hh + 1) * MEM_HEAD_DIM],
                                   (((1,), (1,)), ((), ())), preferred_element_type=F32) for hh in range(MEM_HEADS)]

    def back(rows, h, scores):
        outs = []
        for hh, s in enumerate(scores):
            vh = kv_ref[0, :, D_MODEL + hh * MEM_HEAD_DIM:D_MODEL + (hh + 1) * MEM_HEAD_DIM]
            m = jnp.max(s, axis=-1, keepdims=True)
            e = jnp.exp(s - m)
            l = jnp.sum(e, axis=-1, keepdims=True)
            outs.append((jnp.dot(e.astype(BF16), vh, preferred_element_type=F32) / l).astype(BF16))
        y = jnp.dot(jnp.concatenate(outs, axis=1), wo_ref[...], preferred_element_type=F32)
        o_ref[0, rows, :] = DEEPNORM_ALPHA * h + y

    nxt = front(tiles[0])
    for i, rows in enumerate(tiles):
        cur = nxt
        if i + 1 < len(tiles):
            nxt = front(tiles[i + 1])
        back(rows, *cur)


def _mem_sublayer(z, g_in, b_in, mem, w_q, w_kv, w_o, tm=1024, sub=256):
    bsz, seq, d = z.shape
    nm = mem.shape[1]
    tm = min(tm, seq)
    kv = _proj(mem.reshape(bsz * nm, d), w_kv.astype(BF16), BF16).reshape(bsz, nm, 2 * d)
    wq = (w_q * (MEM_HEAD_DIM ** -0.5)).astype(BF16)
    row = lambda b_, i: (b_, i, 0)
    fix = lambda b_, i: (0, 0)
    return pl.pallas_call(
        functools.partial(_mem_attn_kernel, sub=min(sub, tm)),
        grid=(bsz, seq // tm),
        in_specs=[pl.BlockSpec((1, tm, d), row),
                  pl.BlockSpec((1, d), fix), pl.BlockSpec((1, d), fix),
                  pl.BlockSpec((1, nm, 2 * d), lambda b_, i: (b_, 0, 0)),
                  pl.BlockSpec((d, d), fix, pipeline_mode=pl.Buffered(1)),
                  pl.BlockSpec((d, d), fix, pipeline_mode=pl.Buffered(1))],
        out_specs=pl.BlockSpec((1, tm, d), row),
        out_shape=jax.ShapeDtypeStruct((bsz, seq, d), F32),
        compiler_params=_cparams("parallel", "parallel"),
        name="mem_attn",
    )(z, g_in, b_in, kv, wq, w_o.astype(BF16))


def _mlp_kernel(z_ref, gin_ref, bin_ref, w1_ref, w2_ref, g_ref, b_ref, o_ref, *, sub):
    for r0 in range(0, z_ref.shape[0], sub):
        rows = slice(r0, r0 + sub)
        h = _ln(z_ref[rows, :], gin_ref[...], bin_ref[...])
        u = jnp.dot(h.astype(BF16), w1_ref[...], preferred_element_type=F32)
        u = jnp.maximum(u, 0.0)
        y = jnp.dot((u * u).astype(BF16), w2_ref[...], preferred_element_type=F32)
        o_ref[rows, :] = _residual_ln(h, y, g_ref[...], b_ref[...])


def _mlp_sublayer(z, g_in, b_in, w1, w2, g, b, tm=1024, sub=256):
    bsz, seq, d = z.shape
    t = bsz * seq
    tm = min(tm, t)
    ff = w1.shape[1]
    resident = pl.Buffered(1)
    out = pl.pallas_call(
        functools.partial(_mlp_kernel, sub=min(sub, tm)),
        grid=(t // tm,),
        in_specs=[pl.BlockSpec((tm, d), lambda i: (i, 0)),
                  pl.BlockSpec((1, d), lambda i: (0, 0)),
                  pl.BlockSpec((1, d), lambda i: (0, 0)),
                  pl.BlockSpec((d, ff), lambda i: (0, 0), pipeline_mode=resident),
                  pl.BlockSpec((ff, d), lambda i: (0, 0), pipeline_mode=resident),
                  pl.BlockSpec((1, d), lambda i: (0, 0)),
                  pl.BlockSpec((1, d), lambda i: (0, 0))],
        out_specs=pl.BlockSpec((tm, d), lambda i: (i, 0)),
        out_shape=jax.ShapeDtypeStruct((t, d), F32),
        compiler_params=_cparams("parallel"),
        name="mlp",
    )(z.reshape(t, d), g_in, b_in, w1.astype(BF16), w2.astype(BF16), g, b)
    return out.reshape(bsz, seq, d)


def kernel(x, mem, ln_g, ln_b, lru_w_in, lru_conv_w, lru_conv_b, lru_gate_w, lru_gate_b, lru_lambda, lru_w_out, rw_mu, rw_w_r, rw_w_k, rw_w_v, rw_w0, rw_w1, rw_w2, rw_a0, rw_a1, rw_a2, rw_g1, rw_g2, rw_k_k, rw_k_a, rw_r_k, rw_lnx_g, rw_lnx_b, rw_w_o, ca_w_qkv, ca_rel_bias, ca_w_o, mx_w_q, mx_w_kv, mx_w_o, mlp_w1, mlp_w2):
    h = x
    for i in range(DEPTH):
        kind, j = i % N_MIXERS, i // N_MIXERS
        g = lambda s: ln_g[i, s][None, :]
        b = lambda s: ln_b[i, s][None, :]
        if kind == 0:
            z = _rglru_sublayer(h, lru_w_in[j], lru_conv_w[j], lru_conv_b[j], lru_gate_w[j], lru_gate_b[j],
                                lru_lambda[j], lru_w_out[j])
        elif kind == 1:
            z = _rwkv_sublayer(h, rw_mu[j], rw_w_r[j], rw_w_k[j], rw_w_v[j], rw_w0[j], rw_w1[j], rw_w2[j],
                               rw_a0[j], rw_a1[j], rw_a2[j], rw_g1[j], rw_g2[j], rw_k_k[j], rw_k_a[j],
                               rw_r_k[j], rw_lnx_g[j], rw_lnx_b[j], rw_w_o[j])
        else:
            z = _attn_sublayer(h, ca_w_qkv[j], ca_rel_bias[j], ca_w_o[j])
        z = _mem_sublayer(z, g(0), b(0), mem, mx_w_q[i], mx_w_kv[i], mx_w_o[i])
        h = _mlp_sublayer(z, g(1), b(1), mlp_w1[i], mlp_w2[i], g(2), b(2))
    return h
```
